```python
import math
import jax, jax.numpy as jnp
from jax import lax
import numpy as np

D_MODEL = 1024
BATCH = 8
SEQ = 2048
DEPTH = 2

N_EVEN = (DEPTH + 1) // 2
N_ODD = DEPTH // 2
D_MIX = 2 * D_MODEL
HEAD_DIM = 64
SSD_WIDTH = D_MIX // 2
SSD_HEADS = SSD_WIDTH // HEAD_DIM
SSD_GROUPS = 2
SSD_STATE = 128
SSD_CONV = 4
SSD_CHUNK = 128
SSD_CONV_CH = SSD_WIDTH + 2 * SSD_GROUPS * SSD_STATE
MOBA_WIDTH = D_MIX - SSD_WIDTH
MOBA_HEADS = MOBA_WIDTH // HEAD_DIM
MOBA_BLOCK = 256
MOBA_TOPK = 3
MOBA_Q_CHUNK = 32
FOX_WIDTH = 3 * D_MIX // 4
FOX_HEADS = FOX_WIDTH // HEAD_DIM
FOX_Q_BLOCK = 128
S5_WIDTH = D_MIX - FOX_WIDTH
S5_GROUP = 16
S5_GROUPS = S5_WIDTH // S5_GROUP
S5_STATE = 64
EVEN_PROJ = D_MIX + SSD_CONV_CH + SSD_HEADS + 3 * MOBA_WIDTH
ODD_PROJ = D_MIX + 3 * FOX_WIDTH + FOX_HEADS + S5_WIDTH
RMS_EPS = 1e-6

kernel_name = "hybrid_ssd_moba_fox_s5_trunk"


def rmsnorm(x, g):
    xf = x.astype(jnp.float32)
    xf = xf * lax.rsqrt(jnp.mean(xf * xf, axis=-1, keepdims=True) + RMS_EPS)
    return (xf * g.astype(jnp.float32)).astype(x.dtype)


def split_cols(t, widths):
    offs = np.cumsum(widths)[:-1].tolist()
    return jnp.split(t, offs, axis=-1)


def causal_dwconv(u, w, b):
    ch = u.shape[-1]
    out = lax.conv_general_dilated(u, w[:, None, :].astype(u.dtype), window_strides=(1,),
                                   padding=[(w.shape[0] - 1, 0)],
                                   dimension_numbers=('NWC', 'WIO', 'NWC'),
                                   feature_group_count=ch)
    return out + b.astype(u.dtype)


def segsum(a):
    T = a.shape[-1]
    ar = jnp.broadcast_to(a[..., None], a.shape + (T,))
    ar = jnp.where(jnp.tril(jnp.ones((T, T), bool), -1), ar, 0.0)
    cs = jnp.cumsum(ar, axis=-2)
    return jnp.where(jnp.tril(jnp.ones((T, T), bool)), cs, -jnp.inf)


def ssd_chunked(xs, dt, a, b_in, c_in):
    Bsz, L, H, P = xs.shape
    G, N = b_in.shape[2], b_in.shape[3]
    R = H // G
    nc = L // SSD_CHUNK
    x = (xs.astype(jnp.float32) * dt[..., None]).reshape(Bsz, nc, SSD_CHUNK, G, R, P)
    adt = (dt * a).reshape(Bsz, nc, SSD_CHUNK, G, R).transpose(0, 3, 4, 1, 2)
    Bc = b_in.astype(jnp.float32).reshape(Bsz, nc, SSD_CHUNK, G, N)
    Cc = c_in.astype(jnp.float32).reshape(Bsz, nc, SSD_CHUNK, G, N)
    a_cum = jnp.cumsum(adt, axis=-1)
    decay_in = jnp.exp(segsum(adt))
    cb = jnp.einsum('bclgn,bcsgn->bgcls', Cc, Bc)[:, :, None]
    y_diag = jnp.einsum('bgrcls,bcsgrp->bclgrp', cb * decay_in, x)
    decay_states = jnp.exp(a_cum[..., -1:] - a_cum)
    states = jnp.einsum('bclgn,bgrcl,bclgrp->bcgrpn', Bc, decay_states, x)
    a_last = jnp.pad(a_cum[..., -1], ((0, 0), (0, 0), (0, 0), (1, 0)))
    decay_chunk = jnp.exp(segsum(a_last))
    states0 = jnp.concatenate([jnp.zeros_like(states[:, :1]), states], axis=1)
    new_states = jnp.einsum('bgrzc,bcgrpn->bzgrpn', decay_chunk, states0)
    prev_states = new_states[:, :-1]
    y_off = jnp.einsum('bclgn,bcgrpn,bgrcl->bclgrp', Cc, prev_states, jnp.exp(a_cum))
    return (y_diag + y_off).reshape(Bsz, L, H, P)


def moba_attention(q, k, v):
    Bsz, L, H, Dh = q.shape
    nb = -(-L // MOBA_BLOCK)
    pad = nb * MOBA_BLOCK - L
    kp = jnp.pad(k, ((0, 0), (0, pad), (0, 0), (0, 0)))
    vp = jnp.pad(v, ((0, 0), (0, pad), (0, 0), (0, 0)))
    kb = kp.reshape(Bsz, nb, MOBA_BLOCK, H, Dh).transpose(0, 3, 1, 2, 4)
    vb = vp.reshape(Bsz, nb, MOBA_BLOCK, H, Dh).transpose(0, 3, 1, 2, 4)
    kbar = jnp.mean(kb.astype(jnp.float32), axis=3)
    qh = q.transpose(0, 2, 1, 3)
    gate = jnp.einsum('bhqd,bhnd->bhqn', qh.astype(jnp.float32), kbar)
    pos = jnp.arange(L, dtype=jnp.int32)
    qblk = pos // MOBA_BLOCK
    past = jnp.arange(nb, dtype=jnp.int32)[None, :] < qblk[:, None]
    gate = jnp.where(past, gate, -jnp.inf)
    n_top = min(MOBA_TOPK, nb)
    _, top_idx = lax.top_k(gate, n_top)
    own = jnp.broadcast_to(qblk[None, None, :, None], (Bsz, H, L, 1)).astype(top_idx.dtype)
    sel = jnp.concatenate([top_idx, own], axis=-1)
    nsel = n_top + 1
    sel_ok = jnp.concatenate([jnp.arange(n_top, dtype=jnp.int32)[None, :] < qblk[:, None],
                              jnp.ones((L, 1), bool)], axis=-1)
    nc = L // MOBA_Q_CHUNK
    scale = 1.0 / math.sqrt(Dh)
    gather = jax.vmap(jax.vmap(lambda blocks, ids: blocks[ids]))
    offs = jnp.arange(MOBA_BLOCK, dtype=jnp.int32)

    def chunk(args):
        qc, selc, okc, posc = args
        kg = gather(kb, selc)
        vg = gather(vb, selc)
        s = jnp.einsum('bhqd,bhqjpd->bhqjp', qc, kg).astype(jnp.float32) * scale
        kpos = selc[..., None] * MOBA_BLOCK + offs
        mask = okc[None, None, :, :, None] & (kpos <= posc[None, None, :, None, None])
        s = jnp.where(mask, s, -jnp.inf)
        p = jax.nn.softmax(s.reshape(Bsz, H, MOBA_Q_CHUNK, nsel * MOBA_BLOCK), axis=-1).reshape(s.shape)
        return jnp.einsum('bhqjp,bhqjpd->bhqd', p.astype(vg.dtype), vg)

    xs = (qh.reshape(Bsz, H, nc, MOBA_Q_CHUNK, Dh).transpose(2, 0, 1, 3, 4),
          sel.reshape(Bsz, H, nc, MOBA_Q_CHUNK, nsel).transpose(2, 0, 1, 3, 4),
          sel_ok.reshape(nc, MOBA_Q_CHUNK, nsel),
          pos.reshape(nc, MOBA_Q_CHUNK))
    out = lax.map(chunk, xs)
    return out.transpose(1, 0, 3, 2, 4).reshape(Bsz, L, H, Dh)


def forgetting_attention(q, k, v, log_f):
    Bsz, L, H, Dh = q.shape
    F = jnp.cumsum(log_f, axis=1).transpose(0, 2, 1)
    qh = q.transpose(0, 2, 1, 3)
    kh = k.transpose(0, 2, 1, 3)
    vh = v.transpose(0, 2, 1, 3)
    nq = L // FOX_Q_BLOCK
    kpos = jnp.arange(L, dtype=jnp.int32)
    scale = 1.0 / math.sqrt(Dh)

    def block(args):
        qc, Fc, posc = args
        s = jnp.einsum('bhqd,bhsd->bhqs', qc, kh).astype(jnp.float32) * scale
        s = s + Fc[..., None] - F[:, :, None, :]
        s = jnp.where(posc[:, None] >= kpos[None, :], s, -jnp.inf)
        p = jax.nn.softmax(s, axis=-1)
        return jnp.einsum('bhqs,bhsd->bhqd', p.astype(vh.dtype), vh)

    xs = (qh.reshape(Bsz, H, nq, FOX_Q_BLOCK, Dh).transpose(2, 0, 1, 3, 4),
          F.reshape(Bsz, H, nq, FOX_Q_BLOCK).transpose(2, 0, 1, 3),
          kpos.reshape(nq, FOX_Q_BLOCK))
    out = lax.map(block, xs)
    return out.transpose(1, 0, 3, 2, 4).reshape(Bsz, L, H, Dh)


def _complex_affine_combine(e1, e2):
    a1r, a1i, b1r, b1i = e1
    a2r, a2i, b2r, b2i = e2
    return (a2r * a1r - a2i * a1i,
            a2r * a1i + a2i * a1r,
            a2r * b1r - a2i * b1i + b2r,
            a2r * b1i + a2i * b1r + b2i)


def s5_ssm(u, lam_re, lam_im, log_dt, b_re, b_im, c_re, c_im, d_skip):
    Bsz, L, W = u.shape
    uf = u.astype(jnp.float32).reshape(Bsz, L, S5_GROUPS, S5_GROUP)
    dt = jnp.exp(log_dt.astype(jnp.float32))[:, None]
    lr = lam_re.astype(jnp.float32)
    li = lam_im.astype(jnp.float32)
    mag = jnp.exp(lr * dt)
    ar = mag * jnp.cos(li * dt)
    ai = mag * jnp.sin(li * dt)
    den = lr * lr + li * li
    qr = ((ar - 1.0) * lr + ai * li) / den
    qi = (ai * lr - (ar - 1.0) * li) / den
    br = b_re.astype(jnp.float32)
    bi = b_im.astype(jnp.float32)
    bbr = qr[..., None] * br - qi[..., None] * bi
    bbi = qr[..., None] * bi + qi[..., None] * br
    bu_r = jnp.einsum('blgc,gnc->blgn', uf, bbr)
    bu_i = jnp.einsum('blgc,gnc->blgn', uf, bbi)
    a_r = jnp.broadcast_to(ar[None, None], (1, L, S5_GROUPS, S5_STATE))
    a_i = jnp.broadcast_to(ai[None, None], (1, L, S5_GROUPS, S5_STATE))
    _, _, sr, si = lax.associative_scan(_complex_affine_combine, (a_r, a_i, bu_r, bu_i), axis=1)
    y = (jnp.einsum('blgn,gcn->blgc', sr, c_re.astype(jnp.float32))
         - jnp.einsum('blgn,gcn->blgc', si, c_im.astype(jnp.float32)))
    return y.reshape(Bsz, L, W) + d_skip.astype(jnp.float32) * uf.reshape(Bsz, L, W)


def even_mixer(h, in_w, conv_w, conv_b, dt_bias, a_log, d_skip, norm_g, out_w):
    Bsz, L, _ = h.shape
    proj = h @ in_w
    z_a, z_b, xbc, dt_raw, q, k, v = split_cols(
        proj, [SSD_WIDTH, MOBA_WIDTH, SSD_CONV_CH, SSD_HEADS, MOBA_WIDTH, MOBA_WIDTH, MOBA_WIDTH])
    xbc = jax.nn.silu(causal_dwconv(xbc, conv_w, conv_b))
    xs, b_in, c_in = split_cols(xbc, [SSD_WIDTH, SSD_GROUPS * SSD_STATE, SSD_GROUPS * SSD_STATE])
    xs = xs.reshape(Bsz, L, SSD_HEADS, HEAD_DIM)
    b_in = b_in.reshape(Bsz, L, SSD_GROUPS, SSD_STATE)
    c_in = c_in.reshape(Bsz, L, SSD_GROUPS, SSD_STATE)
    dt = jax.nn.softplus(dt_raw.astype(jnp.float32) + dt_bias.astype(jnp.float32))
    a = -jnp.exp(a_log.astype(jnp.float32))
    y_a = ssd_chunked(xs, dt, a, b_in, c_in) + d_skip.astype(jnp.float32)[:, None] * xs.astype(jnp.float32)
    y_a = y_a.reshape(Bsz, L, SSD_WIDTH) * jax.nn.silu(z_a.astype(jnp.float32))
    y_a = rmsnorm(y_a, norm_g).astype(h.dtype)
    y_b = moba_attention(q.reshape(Bsz, L, MOBA_HEADS, HEAD_DIM),
                         k.reshape(Bsz, L, MOBA_HEADS, HEAD_DIM),
                         v.reshape(Bsz, L, MOBA_HEADS, HEAD_DIM)).reshape(Bsz, L, MOBA_WIDTH)
    y_b = y_b.astype(h.dtype) * jax.nn.silu(z_b)
    return jnp.concatenate([y_a, y_b], axis=-1) @ out_w


def odd_mixer(h, in_w, fgate_b, lam_re, lam_im, log_dt, b_re, b_im, c_re, c_im, d_skip, glu_w, glu_b, out_w):
    Bsz, L, _ = h.shape
    proj = h @ in_w
    z_c, z_d, q, k, v, f_raw, u = split_cols(
        proj, [FOX_WIDTH, S5_WIDTH, FOX_WIDTH, FOX_WIDTH, FOX_WIDTH, FOX_HEADS, S5_WIDTH])
    log_f = jax.nn.log_sigmoid(f_raw.astype(jnp.float32) + fgate_b.astype(jnp.float32))
    y_c = forgetting_attention(q.reshape(Bsz, L, FOX_HEADS, HEAD_DIM),
                               k.reshape(Bsz, L, FOX_HEADS, HEAD_DIM),
                               v.reshape(Bsz, L, FOX_HEADS, HEAD_DIM), log_f).reshape(Bsz, L, FOX_WIDTH)
    y_c = y_c.astype(h.dtype) * jax.nn.silu(z_c)
    y_d = jax.nn.gelu(s5_ssm(u, lam_re, lam_im, log_dt, b_re, b_im, c_re, c_im, d_skip))
    y_d = y_d * jax.nn.sigmoid(y_d @ glu_w.astype(jnp.float32) + glu_b.astype(jnp.float32))
    y_d = y_d.astype(h.dtype) * jax.nn.silu(z_d)
    return jnp.concatenate([y_c, y_d], axis=-1) @ out_w


def setup_inputs(seed: int = 0) -> dict:
    key = jax.random.key(seed)
    ks = iter(jax.random.split(key, 40))
    f32 = jnp.float32

    def nrm(shape, s):
        return s * jax.random.normal(next(ks), shape, f32)

    def unif(shape, lo, hi):
        return jax.random.uniform(next(ks), shape, f32, lo, hi)

    x = nrm((BATCH, SEQ, D_MODEL), 1.0)
    c = nrm((BATCH, D_MODEL), 1.0)
    ada_w = nrm((DEPTH, D_MODEL, 3 * D_MODEL), 0.5 * D_MODEL ** -0.5)
    ada_b = nrm((DEPTH, 3 * D_MODEL), 0.02)
    pre_g = 1.0 + nrm((DEPTH, D_MODEL), 0.02)
    post_g = 1.0 + nrm((DEPTH, D_MODEL), 0.02)
    even_in_w = nrm((N_EVEN, D_MODEL, EVEN_PROJ), D_MODEL ** -0.5)
    even_conv_w = nrm((N_EVEN, SSD_CONV, SSD_CONV_CH), SSD_CONV ** -0.5)
    even_conv_b = nrm((N_EVEN, SSD_CONV_CH), 0.02)
    dt0 = jnp.exp(unif((N_EVEN, SSD_HEADS), math.log(1e-3), math.log(1e-1)))
    even_dt_bias = dt0 + jnp.log(-jnp.expm1(-dt0))
    even_a_log = jnp.log(unif((N_EVEN, SSD_HEADS), 1.0, 16.0))
    even_d_skip = 1.0 + nrm((N_EVEN, SSD_HEADS), 0.1)
    even_norm_g = 1.0 + nrm((N_EVEN, SSD_WIDTH), 0.02)
    even_out_w = nrm((N_EVEN, D_MIX, D_MODEL), D_MIX ** -0.5)
    odd_in_w = nrm((N_ODD, D_MODEL, ODD_PROJ), D_MODEL ** -0.5)
    odd_fgate_b = 1.0 + nrm((N_ODD, FOX_HEADS), 0.5)
    odd_lam_re = -0.5 + nrm((N_ODD, S5_GROUPS, S5_STATE), 0.01)
    odd_lam_im = jnp.pi * jnp.arange(S5_STATE, dtype=f32) + nrm((N_ODD, S5_GROUPS, S5_STATE), 0.01)
    odd_log_dt = unif((N_ODD, S5_GROUPS), math.log(1e-3), math.log(1e-1))
    odd_b_re = nrm((N_ODD, S5_GROUPS, S5_STATE, S5_GROUP), (2 * S5_GROUP) ** -0.5)
    odd_b_im = nrm((N_ODD, S5_GROUPS, S5_STATE, S5_GROUP), (2 * S5_GROUP) ** -0.5)
    odd_c_re = nrm((N_ODD, S5_GROUPS, S5_GROUP, S5_STATE), (2 * S5_STATE) ** -0.5)
    odd_c_im = nrm((N_ODD, S5_GROUPS, S5_GROUP, S5_STATE), (2 * S5_STATE) ** -0.5)
    odd_d_skip = nrm((N_ODD, S5_WIDTH), 1.0)
    odd_glu_w = nrm((N_ODD, S5_WIDTH, S5_WIDTH), S5_WIDTH ** -0.5)
    odd_glu_b = nrm((N_ODD, S5_WIDTH), 0.02)
    odd_out_w = nrm((N_ODD, D_MIX, D_MODEL), D_MIX ** -0.5)
    return {"x": x, "c": c, "ada_w": ada_w, "ada_b": ada_b, "pre_g": pre_g, "post_g": post_g,
            "even_in_w": even_in_w, "even_conv_w": even_conv_w, "even_conv_b": even_conv_b,
            "even_dt_bias": even_dt_bias, "even_a_log": even_a_log, "even_d_skip": even_d_skip,
            "even_norm_g": even_norm_g, "even_out_w": even_out_w,
            "odd_in_w": odd_in_w, "odd_fgate_b": odd_fgate_b, "odd_lam_re": odd_lam_re,
            "odd_lam_im": odd_lam_im, "odd_log_dt": odd_log_dt, "odd_b_re": odd_b_re,
            "odd_b_im": odd_b_im, "odd_c_re": odd_c_re, "odd_c_im": odd_c_im,
            "odd_d_skip": odd_d_skip, "odd_glu_w": odd_glu_w, "odd_glu_b": odd_glu_b,
            "odd_out_w": odd_out_w}


def reference(x, c, ada_w, ada_b, pre_g, post_g,
              even_in_w, even_conv_w, even_conv_b, even_dt_bias, even_a_log, even_d_skip,
              even_norm_g, even_out_w,
              odd_in_w, odd_fgate_b, odd_lam_re, odd_lam_im, odd_log_dt, odd_b_re, odd_b_im,
              odd_c_re, odd_c_im, odd_d_skip, odd_glu_w, odd_glu_b, odd_out_w):
    cond = jax.nn.silu(c)
    for layer in range(DEPTH):
        mod = cond @ ada_w[layer] + ada_b[layer]
        shift, scale, gate = jnp.split(mod, 3, axis=-1)
        h = rmsnorm(x, pre_g[layer]) * (1.0 + scale[:, None, :]) + shift[:, None, :]
        i = layer // 2
        if layer % 2 == 0:
            y = even_mixer(h, even_in_w[i], even_conv_w[i], even_conv_b[i], even_dt_bias[i],
                           even_a_log[i], even_d_skip[i], even_norm_g[i], even_out_w[i])
        else:
            y = odd_mixer(h, odd_in_w[i], odd_fgate_b[i], odd_lam_re[i], odd_lam_im[i], odd_log_dt[i],
                          odd_b_re[i], odd_b_im[i], odd_c_re[i], odd_c_im[i], odd_d_skip[i],
                          odd_glu_w[i], odd_glu_b[i], odd_out_w[i])
        x = x + gate[:, None, :] * rmsnorm(y, post_g[layer])
    return x
```

```python
import functools
import math

import jax
import jax.numpy as jnp
from jax import lax
from jax.experimental import pallas as pl
from jax.experimental.pallas import tpu as pltpu

F32 = jnp.float32
BF16 = jnp.bfloat16

D_MODEL = 1024
HEAD_DIM = 64
D_MIX = 2 * D_MODEL
SSD_WIDTH = 1024
SSD_HEADS = 16
SSD_GROUPS = 2
SSD_STATE = 128
SSD_CONV = 4
SSD_CHUNK = 128
MOBA_WIDTH = 1024
MOBA_BLOCK = 256
MOBA_TOPK = 3
FOX_WIDTH = 1536
FOX_HEADS = 24
S5_WIDTH = 512
S5_GROUP = 16
S5_GROUP_SHIFT = 4
S5_GROUPS = 32
S5_STATE = 64
S5_CHUNK = 64
RMS_EPS = 1e-6
ATT_BLOCK = 256
LANES = 128
NEG = -1e30
VMEM_LIMIT = 48 * 1024 * 1024

EV_ZA, EV_ZB, EV_XS, EV_Q, EV_K, EV_V, EV_BC, EV_N = 0, 1024, 2048, 3072, 4096, 5120, 6144, 6656
OD_ZC, OD_ZD, OD_Q, OD_K, OD_V, OD_U, OD_N = 0, 1536, 2048, 3584, 5120, 6656, 7168


def _nt(a, b):
    return lax.dot_general(a, b, (((1,), (1,)), ((), ())), preferred_element_type=F32)


def _dot(a, b):
    return jnp.dot(a, b, preferred_element_type=F32)


def _split3(x):
    hi = x.astype(BF16)
    r = x - hi.astype(F32)
    mid = r.astype(BF16)
    lo = (r - mid.astype(F32)).astype(BF16)
    return hi, mid, lo


def _dot_exact_rhs(x, m_bf16):
    hi, mid, lo = _split3(x)
    return _dot(hi, m_bf16) + _dot(mid, m_bf16) + _dot(lo, m_bf16)


def _silu(x):
    return x * jax.nn.sigmoid(x)


def _softplus(x):
    return jnp.maximum(x, 0.0) + jnp.log1p(jnp.exp(-jnp.abs(x)))


def _params(sem, limit=VMEM_LIMIT):
    return pltpu.CompilerParams(dimension_semantics=sem, vmem_limit_bytes=limit)


def _ada_kernel(c_ref, w_ref, b_ref, o_ref):
    cond = _silu(c_ref[...])
    hi, mid, lo = _split3(cond)
    w = w_ref[0]
    whi, wmid, wlo = _split3(w)
    acc = _dot(hi, whi) + _dot(hi, wmid) + _dot(mid, whi)
    acc = acc + _dot(hi, wlo) + _dot(mid, wmid) + _dot(lo, whi)
    o_ref[0] = acc + b_ref[0]


def _ada_mod(c, ada_w, ada_b):
    depth, d, d3 = ada_w.shape
    bsz = c.shape[0]
    nj = d3 // d
    return pl.pallas_call(
        _ada_kernel,
        grid=(depth, nj),
        in_specs=[
            pl.BlockSpec((bsz, d), lambda l, j: (0, 0)),
            pl.BlockSpec((1, d, d), lambda l, j: (l, 0, j)),
            pl.BlockSpec((1, 1, d), lambda l, j: (l, 0, j)),
        ],
        out_specs=pl.BlockSpec((1, bsz, d), lambda l, j: (l, 0, j)),
        out_shape=jax.ShapeDtypeStruct((depth, bsz, d3), F32),
        compiler_params=_params(("parallel", "parallel")),
        name="ada_mod",
    )(c, ada_w, ada_b.reshape(depth, 1, d3))


def _inproj_kernel(x_ref, sc_ref, sh_ref, g_ref, w_ref, ws_ref, o_ref, os_ref, h_scr):
    @pl.when(pl.program_id(1) == 0)
    def _():
        x = x_ref[...]
        ms = jnp.mean(x * x, axis=-1, keepdims=True)
        xn = x * lax.rsqrt(ms + RMS_EPS) * g_ref[...]
        h = xn * (1.0 + sc_ref[0]) + sh_ref[0]
        h_hi = h.astype(BF16)
        h_scr[...] = h_hi
        h_lo = (h - h_hi.astype(F32)).astype(BF16)
        ws = ws_ref[...]
        r = _dot(h_hi, ws)
        os_ref[...] = r[:, :LANES] + r[:, LANES:] + _dot(h_lo, ws[:, :LANES])

    o_ref[...] = _dot(h_scr[...], w_ref[...]).astype(BF16)


def _inproj(x2, scale, shift, g, w, ws, seq, tm=1024, tn=512):
    t, d = x2.shape
    n = w.shape[1]
    per = seq // tm
    return pl.pallas_call(
        _inproj_kernel,
        grid=(t // tm, n // tn),
        in_specs=[
            pl.BlockSpec((tm, d), lambda i, j: (i, 0)),
            pl.BlockSpec((1, 1, d), lambda i, j: (i // per, 0, 0)),
            pl.BlockSpec((1, 1, d), lambda i, j: (i // per, 0, 0)),
            pl.BlockSpec((1, d), lambda i, j: (0, 0)),
            pl.BlockSpec((d, tn), lambda i, j: (0, j)),
            pl.BlockSpec((d, 2 * LANES), lambda i, j: (0, 0)),
        ],
        out_specs=[
            pl.BlockSpec((tm, tn), lambda i, j: (i, j)),
            pl.BlockSpec((tm, LANES), lambda i, j: (i, 0)),
        ],
        out_shape=[
            jax.ShapeDtypeStruct((t, n), BF16),
            jax.ShapeDtypeStruct((t, LANES), F32),
        ],
        scratch_shapes=[pltpu.VMEM((tm, d), BF16)],
        compiler_params=_params(("parallel", "arbitrary")),
        name="in_proj",
    )(x2, scale, shift, g, w, ws)


def _outproj_kernel(a_ref, b_ref, wa_ref, wb_ref, x_ref, gate_ref, pg_ref, o_ref):
    y = _dot(a_ref[...], wa_ref[...]) + _dot(b_ref[...], wb_ref[...])
    ms = jnp.mean(y * y, axis=-1, keepdims=True)
    yn = y * lax.rsqrt(ms + RMS_EPS) * pg_ref[...]
    o_ref[...] = x_ref[...] + gate_ref[0] * yn


def _outproj(a, b, wa, wb, x2, gate, pg, seq, tm=512):
    t, d = x2.shape
    ka, kb = a.shape[1], b.shape[1]
    per = seq // tm
    return pl.pallas_call(
        _outproj_kernel,
        grid=(t // tm,),
        in_specs=[
            pl.BlockSpec((tm, ka), lambda i: (i, 0)),
            pl.BlockSpec((tm, kb), lambda i: (i, 0)),
            pl.BlockSpec((ka, d), lambda i: (0, 0)),
            pl.BlockSpec((kb, d), lambda i: (0, 0)),
            pl.BlockSpec((tm, d), lambda i: (i, 0)),
            pl.BlockSpec((1, 1, d), lambda i: (i // per, 0, 0)),
            pl.BlockSpec((1, d), lambda i: (0, 0)),
        ],
        out_specs=pl.BlockSpec((tm, d), lambda i: (i, 0)),
        out_shape=jax.ShapeDtypeStruct((t, d), F32),
        compiler_params=_params(("parallel",)),
        name="out_proj",
    )(a, b, wa, wb, x2, gate, pg)


def _causal_conv_silu(in_ref, ext_scr, tail_scr, w, b):
    lc = in_ref.shape[0]
    raw = in_ref[...].astype(F32)
    ext_scr[0:8, :] = tail_scr[...]
    ext_scr[8:8 + lc, :] = raw
    tail_scr[...] = raw[lc - 8:lc, :]
    acc = b
    for j in range(SSD_CONV):
        off = 8 - (SSD_CONV - 1) + j
        acc = acc + w[j:j + 1, :] * ext_scr[off:off + lc, :]
    return _silu(acc)


def _ssd_kernel(z_ref, xs_ref, bc_ref, dt_ref, cwx_ref, cwb_ref, cbx_ref, cbb_ref, dtb_ref, alog_ref,
                dexp_ref, ng_ref, e_ref, o_ref, tailx_scr, tailb_scr, extx_scr, extb_scr, state_scr):
    lc = SSD_CHUNK
    hw = SSD_HEADS // SSD_GROUPS * HEAD_DIM

    @pl.when(pl.program_id(1) == 0)
    def _():
        tailx_scr[...] = jnp.zeros_like(tailx_scr)
        tailb_scr[...] = jnp.zeros_like(tailb_scr)
        state_scr[...] = jnp.zeros_like(state_scr)

    xs = _causal_conv_silu(xs_ref, extx_scr, tailx_scr, cwx_ref[...], cbx_ref[...])
    bc = _causal_conv_silu(bc_ref, extb_scr, tailb_scr, cwb_ref[...], cbb_ref[...])

    dt = _softplus(dt_ref[...] + dtb_ref[...])
    adt = dt * (-jnp.exp(alog_ref[...]))
    rows = lax.broadcasted_iota(jnp.int32, (lc, lc), 0)
    cols = lax.broadcasted_iota(jnp.int32, (lc, lc), 1)
    lower = rows >= cols
    a_cum = _dot_exact_rhs_lhs(lower.astype(BF16), adt)
    a_cum_t = a_cum.T
    a_last = a_cum[lc - 1:lc, :]
    stacked = jnp.concatenate(
        [dt, jnp.exp(a_last - a_cum), jnp.exp(a_cum), jnp.broadcast_to(jnp.exp(a_last), (8, LANES))], axis=0)
    expanded = _dot_exact_rhs(stacked, e_ref[...])
    dt_e = expanded[0:lc]
    dec_e = expanded[lc:2 * lc]
    ea_e = expanded[2 * lc:3 * lc]
    al_e = expanded[3 * lc:3 * lc + 1]

    xdt = xs * dt_e
    xdt_b = xdt.astype(BF16)
    xd_b = (xdt * dec_e).astype(BF16)
    lane = lax.broadcasted_iota(jnp.int32, (1, LANES), 1)
    pieces = []
    for g in range(SSD_GROUPS):
        bg = bc[:, g * SSD_STATE:(g + 1) * SSD_STATE]
        cg = bc[:, (SSD_GROUPS + g) * SSD_STATE:(SSD_GROUPS + g + 1) * SSD_STATE]
        cg_b = cg.astype(BF16)
        cb = _nt(cg_b, bg.astype(BF16))
        st = state_scr[:, g * hw:(g + 1) * hw]
        y_off = _dot(cg_b, st.astype(BF16)) * ea_e[:, g * hw:(g + 1) * hw]
        new_st = _dot(bg.T.astype(BF16), xd_b[:, g * hw:(g + 1) * hw])
        state_scr[:, g * hw:(g + 1) * hw] = al_e[:, g * hw:(g + 1) * hw] * st + new_st
        for pair in range(SSD_HEADS // SSD_GROUPS // 2):
            acc = None
            c0 = g * hw + pair * LANES
            xp = xdt_b[:, c0:c0 + LANES]
            for half in range(2):
                h = g * (SSD_HEADS // SSD_GROUPS) + pair * 2 + half
                diff = a_cum[:, h:h + 1] - a_cum_t[h:h + 1, :]
                decay = jnp.exp(jnp.where(lower, diff, NEG))
                m = (cb * decay).astype(BF16)
                in_half = (lane >= half * HEAD_DIM) & (lane < (half + 1) * HEAD_DIM)
                part = _dot(m, jnp.where(in_half, xp, jnp.zeros_like(xp)))
                acc = part if acc is None else acc + part
            pieces.append(acc + y_off[:, pair * LANES:(pair + 1) * LANES])
    y = jnp.concatenate(pieces, axis=1) + dexp_ref[...] * xs
    y = y * _silu(z_ref[...].astype(F32))
    ms = jnp.mean(y * y, axis=-1, keepdims=True)
    o_ref[...] = (y * lax.rsqrt(ms + RMS_EPS) * ng_ref[...]).astype(BF16)


def _dot_exact_rhs_lhs(m_bf16, x):
    hi, mid, lo = _split3(x)
    return _dot(m_bf16, hi) + _dot(m_bf16, mid) + _dot(m_bf16, lo)


def _ssd(proj, small, conv_w, conv_b, dt_bias, a_log, d_skip, norm_g, bsz, seq):
    lc = SSD_CHUNK
    nc = seq // lc
    t = bsz * seq
    pad = LANES - SSD_HEADS
    dtb = jnp.pad(dt_bias, (0, pad)).reshape(1, LANES)
    alog = jnp.pad(a_log, (0, pad)).reshape(1, LANES)
    dexp = jnp.repeat(d_skip, HEAD_DIM).reshape(1, SSD_WIDTH)
    expand = (jnp.arange(LANES)[:, None] == (jnp.arange(SSD_WIDTH) // HEAD_DIM)[None, :]).astype(BF16)
    nbc = 2 * SSD_GROUPS * SSD_STATE
    row = lambda b, c: b * nc + c
    const = lambda b, c: (0, 0)
    return pl.pallas_call(
        _ssd_kernel,
        grid=(bsz, nc),
        in_specs=[
            pl.BlockSpec((lc, SSD_WIDTH), lambda b, c: (row(b, c), EV_ZA // SSD_WIDTH)),
            pl.BlockSpec((lc, SSD_WIDTH), lambda b, c: (row(b, c), EV_XS // SSD_WIDTH)),
            pl.BlockSpec((lc, nbc), lambda b, c: (row(b, c), EV_BC // nbc)),
            pl.BlockSpec((lc, LANES), lambda b, c: (row(b, c), 0)),
            pl.BlockSpec((SSD_CONV, SSD_WIDTH), const),
            pl.BlockSpec((SSD_CONV, nbc), const),
            pl.BlockSpec((1, SSD_WIDTH), const),
            pl.BlockSpec((1, nbc), const),
            pl.BlockSpec((1, LANES), const),
            pl.BlockSpec((1, LANES), const),
            pl.BlockSpec((1, SSD_WIDTH), const),
            pl.BlockSpec((1, SSD_WIDTH), const),
            pl.BlockSpec((LANES, SSD_WIDTH), const),
        ],
        out_specs=pl.BlockSpec((lc, SSD_WIDTH), lambda b, c: (row(b, c), 0)),
        out_shape=jax.ShapeDtypeStruct((t, SSD_WIDTH), BF16),
        scratch_shapes=[
            pltpu.VMEM((8, SSD_WIDTH), F32),
            pltpu.VMEM((8, nbc), F32),
            pltpu.VMEM((lc + 8, SSD_WIDTH), F32),
            pltpu.VMEM((lc + 8, nbc), F32),
            pltpu.VMEM((SSD_STATE, SSD_WIDTH), F32),
        ],
        compiler_params=_params(("parallel", "arbitrary")),
        name="ssd",
    )(proj, proj, proj, small,
      conv_w[:, :SSD_WIDTH], conv_w[:, SSD_WIDTH:], conv_b[:SSD_WIDTH].reshape(1, -1),
      conv_b[SSD_WIDTH:].reshape(1, -1), dtb, alog, dexp, norm_g.reshape(1, -1), expand)


def _attend_block(qi, k_ref, vt_scr, half, qb, diag_bias, past_bias):
    blk = ATT_BLOCK
    lo = half * HEAD_DIM
    s_idx = lax.broadcasted_iota(jnp.int32, (blk, blk), 0)
    t_idx = lax.broadcasted_iota(jnp.int32, (blk, blk), 1)

    k_d = k_ref[pl.ds(pl.multiple_of(qb * blk, blk), blk), :]
    s = _nt(k_d, qi)
    if diag_bias is not None:
        s = s + diag_bias
    s = jnp.where(s_idx <= t_idx, s, NEG)
    m = jnp.max(s, axis=0, keepdims=True)
    p = jnp.exp(s - m)
    l = jnp.sum(p, axis=0, keepdims=True)
    acc = _dot(vt_scr[qb, lo:lo + HEAD_DIM, :], p.astype(BF16))

    def body(n, carry):
        m, l, acc = carry
        k_n = k_ref[pl.ds(pl.multiple_of(n * blk, blk), blk), :]
        s = _nt(k_n, qi) + past_bias(n)
        m_new = jnp.maximum(m, jnp.max(s, axis=0, keepdims=True))
        alpha = jnp.exp(m - m_new)
        p = jnp.exp(s - m_new)
        l = alpha * l + jnp.sum(p, axis=0, keepdims=True)
        acc = alpha * acc + _dot(vt_scr[n, lo:lo + HEAD_DIM, :], p.astype(BF16))
        return m_new, l, acc

    m, l, acc = lax.fori_loop(0, qb, body, (m, l, acc))
    return acc / l


def _store_vt(v_ref, vt_scr):
    for n in range(vt_scr.shape[0]):
        vb = v_ref[n * ATT_BLOCK:(n + 1) * ATT_BLOCK, :].astype(F32)
        vt_scr[n] = vb.T.astype(BF16)


def _head_lanes(half):
    lane = lax.broadcasted_iota(jnp.int32, (1, LANES), 1)
    return (lane >= half * HEAD_DIM) & (lane < (half + 1) * HEAD_DIM)


def _moba_kernel(q_ref, k_ref, v_ref, z_ref, o_ref, vt_scr, kbar_scr, bias_scr, ot_scr):
    qb = pl.program_id(2)
    nb = vt_scr.shape[0]

    @pl.when(qb == 0)
    def _():
        _store_vt(v_ref, vt_scr)
        for n in range(nb):
            kb = k_ref[n * ATT_BLOCK:(n + 1) * ATT_BLOCK, :].astype(F32)
            kbar_scr[n:n + 1, :] = jnp.mean(kb, axis=0, keepdims=True)

    qs = q_ref[...].astype(F32) * (1.0 / math.sqrt(HEAD_DIM))
    n_idx = lax.broadcasted_iota(jnp.int32, (nb, ATT_BLOCK), 0)
    for half in range(2):
        in_half = _head_lanes(half)
        qi = jnp.where(in_half, qs, 0.0).astype(BF16)
        kbar = jnp.where(in_half, kbar_scr[...], 0.0)
        kb_hi = kbar.astype(BF16)
        kb_lo = (kbar - kb_hi.astype(F32)).astype(BF16)
        gate = _nt(kb_hi, qi) + _nt(kb_lo, qi)
        rank = jnp.zeros((nb, ATT_BLOCK), F32)
        for mth in range(nb):
            gm = gate[mth:mth + 1, :]
            beats = (gm > gate) | ((gm == gate) & (mth < n_idx))
            rank = rank + jnp.where(beats & (mth < qb), 1.0, 0.0)
        chosen = (rank < float(MOBA_TOPK)) & (n_idx < qb)
        bias_scr[half] = jnp.where(chosen, 0.0, NEG)
        ot_scr[half * HEAD_DIM:(half + 1) * HEAD_DIM, :] = _attend_block(
            qi, k_ref, vt_scr, half, qb, None,
            lambda n, half=half: bias_scr[half, pl.ds(n, 1), :])
    o = ot_scr[...].T
    o_ref[...] = (o * _silu(z_ref[...].astype(F32))).astype(BF16)


def _moba(proj, bsz, seq):
    blk = ATT_BLOCK
    nb = seq // blk
    pairs = MOBA_WIDTH // LANES
    t = bsz * seq
    return pl.pallas_call(
        _moba_kernel,
        grid=(bsz, pairs, nb),
        in_specs=[
            pl.BlockSpec((blk, LANES), lambda b, p, i: (b * nb + i, EV_Q // LANES + p)),
            pl.BlockSpec((seq, LANES), lambda b, p, i: (b, EV_K // LANES + p)),
            pl.BlockSpec((seq, LANES), lambda b, p, i: (b, EV_V // LANES + p)),
            pl.BlockSpec((blk, LANES), lambda b, p, i: (b * nb + i, EV_ZB // LANES + p)),
        ],
        out_specs=pl.BlockSpec((blk, LANES), lambda b, p, i: (b * nb + i, p)),
        out_shape=jax.ShapeDtypeStruct((t, MOBA_WIDTH), BF16),
        scratch_shapes=[
            pltpu.VMEM((nb, LANES, blk), BF16),
            pltpu.VMEM((nb, LANES), F32),
            pltpu.VMEM((2, nb, blk), F32),
            pltpu.VMEM((LANES, blk), F32),
        ],
        compiler_params=_params(("parallel", "parallel", "arbitrary")),
        name="moba",
    )(proj, proj, proj, proj)


def _fgate_kernel(f_ref, fb_ref, o_ref, carry_scr):
    @pl.when(pl.program_id(1) == 0)
    def _():
        carry_scr[...] = jnp.zeros_like(carry_scr)

    blk = f_ref.shape[0]
    nlf = _softplus(-(f_ref[...] + fb_ref[...]))
    rows = lax.broadcasted_iota(jnp.int32, (blk, blk), 0)
    cols = lax.broadcasted_iota(jnp.int32, (blk, blk), 1)
    csum = _dot_exact_rhs_lhs((rows >= cols).astype(BF16), nlf) + carry_scr[0:1, :]
    o_ref[...] = csum
    carry_scr[...] = jnp.broadcast_to(csum[blk - 1:blk, :], carry_scr.shape)


def _fgate(small, fgate_b, bsz, seq):
    blk = ATT_BLOCK
    nb = seq // blk
    fb = jnp.pad(fgate_b, (0, LANES - FOX_HEADS)).reshape(1, LANES)
    return pl.pallas_call(
        _fgate_kernel,
        grid=(bsz, nb),
        in_specs=[
            pl.BlockSpec((blk, LANES), lambda b, i: (b * nb + i, 0)),
            pl.BlockSpec((1, LANES), lambda b, i: (0, 0)),
        ],
        out_specs=pl.BlockSpec((blk, LANES), lambda b, i: (b * nb + i, 0)),
        out_shape=jax.ShapeDtypeStruct((bsz * seq, LANES), F32),
        scratch_shapes=[pltpu.VMEM((8, LANES), F32)],
        compiler_params=_params(("parallel", "arbitrary")),
        name="fox_gate",
    )(small, fb)


def _fox_kernel(q_ref, k_ref, v_ref, z_ref, nf_ref, o_ref, vt_scr, fb_scr, ot_scr):
    pair = pl.program_id(1)
    qb = pl.program_id(2)
    nb = vt_scr.shape[0]
    blk = ATT_BLOCK

    @pl.when(qb == 0)
    def _():
        _store_vt(v_ref, vt_scr)
        sel_row = lax.broadcasted_iota(jnp.int32, (LANES, blk), 0)
        for half in range(2):
            onehot = (sel_row == pair * 2 + half).astype(BF16)
            for n in range(nb):
                fb_scr[half, n] = _dot_exact_rhs(nf_ref[n * blk:(n + 1) * blk, :], onehot)

    qs = q_ref[...].astype(F32) * (1.0 / math.sqrt(HEAD_DIM))
    for half in range(2):
        qi = jnp.where(_head_lanes(half), qs, 0.0).astype(BF16)
        ot_scr[half * HEAD_DIM:(half + 1) * HEAD_DIM, :] = _attend_block(
            qi, k_ref, vt_scr, half, qb, fb_scr[half, qb],
            lambda n, half=half: fb_scr[half, n])
    o = ot_scr[...].T
    o_ref[...] = (o * _silu(z_ref[...].astype(F32))).astype(BF16)


def _fox(proj, negf, bsz, seq):
    blk = ATT_BLOCK
    nb = seq // blk
    pairs = FOX_WIDTH // LANES
    t = bsz * seq
    return pl.pallas_call(
        _fox_kernel,
        grid=(bsz, pairs, nb),
        in_specs=[
            pl.BlockSpec((blk, LANES), lambda b, p, i: (b * nb + i, OD_Q // LANES + p)),
            pl.BlockSpec((seq, LANES), lambda b, p, i: (b, OD_K // LANES + p)),
            pl.BlockSpec((seq, LANES), lambda b, p, i: (b, OD_V // LANES + p)),
            pl.BlockSpec((blk, LANES), lambda b, p, i: (b * nb + i, OD_ZC // LANES + p)),
            pl.BlockSpec((seq, LANES), lambda b, p, i: (b, 0)),
        ],
        out_specs=pl.BlockSpec((blk, LANES), lambda b, p, i: (b * nb + i, p)),
        out_shape=jax.ShapeDtypeStruct((t, FOX_WIDTH), BF16),
        scratch_shapes=[
            pltpu.VMEM((nb, LANES, blk), BF16),
            pltpu.VMEM((2, nb, blk, blk), F32),
            pltpu.VMEM((LANES, blk), F32),
        ],
        compiler_params=_params(("parallel", "parallel", "arbitrary")),
        name="fox",
    )(proj, proj, proj, proj, negf)


def _s5prep_kernel(lr_ref, li_ref, ldt_ref, btr_ref, bti_ref, cr_ref, ci_ref,
                   wt_ref, str_ref, sti_ref, or_ref, oi_ref, apr_ref, api_ref):
    lc = S5_CHUNK
    rows = lc * S5_GROUP
    lr = lr_ref[0]
    li = li_ref[0]
    dt = jnp.exp(ldt_ref[0])
    rate = lr * dt
    freq = li * dt

    def power(e):
        mag = jnp.exp(rate * e)
        ang = freq * e
        return mag * jnp.cos(ang), mag * jnp.sin(ang)

    ar, ai = power(jnp.ones((1, S5_STATE), F32))
    den = lr * lr + li * li
    qr = ((ar - 1.0) * lr + ai * li) / den
    qi = (ai * lr - (ar - 1.0) * li) / den
    btr = btr_ref[0]
    bti = bti_ref[0]
    bbr = qr * btr - qi * bti
    bbi = qr * bti + qi * btr
    tile = lambda a: jnp.concatenate([a] * lc, axis=0)
    bbr_t, bbi_t = tile(bbr), tile(bbi)
    cr_t, ci_t = tile(cr_ref[0]), tile(ci_ref[0])
    tt = (lax.broadcasted_iota(jnp.int32, (rows, S5_STATE), 0) >> S5_GROUP_SHIFT).astype(F32)
    mid = float(lc // 2)

    pr, pi = power(tt - mid)
    ctr = cr_t * pr - ci_t * pi
    cti = cr_t * pi + ci_t * pr
    pr, pi = power(mid - tt)
    bsr = bbr_t * pr - bbi_t * pi
    bsi = bbr_t * pi + bbi_t * pr
    wt = _nt(bsr.astype(BF16), ctr.astype(BF16)) - _nt(bsi.astype(BF16), cti.astype(BF16))
    s_of_row = lax.broadcasted_iota(jnp.int32, (rows, rows), 0) >> S5_GROUP_SHIFT
    t_of_col = lax.broadcasted_iota(jnp.int32, (rows, rows), 1) >> S5_GROUP_SHIFT
    wt_ref[0] = jnp.where(t_of_col >= s_of_row, wt, 0.0).astype(BF16)

    pr, pi = power(float(lc - 1) - tt)
    str_ref[0] = (bbr_t * pr - bbi_t * pi).astype(BF16)
    sti_ref[0] = (bbr_t * pi + bbi_t * pr).astype(BF16)
    pr, pi = power(tt + 1.0)
    or_ref[0] = (cr_t * pr - ci_t * pi).astype(BF16)
    oi_ref[0] = (-(cr_t * pi + ci_t * pr)).astype(BF16)
    pr, pi = power(jnp.full((1, S5_STATE), float(lc), F32))
    apr_ref[0] = pr
    api_ref[0] = pi


def _s5prep(lam_re, lam_im, log_dt, b_re, b_im, c_re, c_im):
    g, n = lam_re.shape
    rows = S5_CHUNK * S5_GROUP
    vec = pl.BlockSpec((1, 1, n), lambda i: (i, 0, 0))
    mat = pl.BlockSpec((1, S5_GROUP, n), lambda i: (i, 0, 0))
    tall = pl.BlockSpec((1, rows, n), lambda i: (i, 0, 0))
    return pl.pallas_call(
        _s5prep_kernel,
        grid=(g,),
        in_specs=[vec, vec, vec, mat, mat, mat, mat],
        out_specs=[pl.BlockSpec((1, rows, rows), lambda i: (i, 0, 0)), tall, tall, tall, tall, vec, vec],
        out_shape=[
            jax.ShapeDtypeStruct((g, rows, rows), BF16),
            jax.ShapeDtypeStruct((g, rows, n), BF16),
            jax.ShapeDtypeStruct((g, rows, n), BF16),
            jax.ShapeDtypeStruct((g, rows, n), BF16),
            jax.ShapeDtypeStruct((g, rows, n), BF16),
            jax.ShapeDtypeStruct((g, 1, n), F32),
            jax.ShapeDtypeStruct((g, 1, n), F32),
        ],
        compiler_params=_params(("parallel",)),
        name="s5_prep",
    )(lam_re.reshape(g, 1, n), lam_im.reshape(g, 1, n),
      jnp.broadcast_to(log_dt[:, None, None], (g, 1, n)),
      jnp.swapaxes(b_re, 1, 2), jnp.swapaxes(b_im, 1, 2), c_re, c_im)


def _s5main_kernel(u_ref, wt_ref, str_ref, sti_ref, or_ref, oi_ref, apr_ref, api_ref, d_ref, y_ref,
                   xr_scr, xi_scr, *, bsz):
    u = u_ref[0]
    sr = _dot(u, str_ref[0])
    si = _dot(u, sti_ref[0])
    ar = apr_ref[0]
    ai = api_ref[0]
    nk = u.shape[0] // bsz
    xr = jnp.zeros((bsz, S5_STATE), F32)
    xi = jnp.zeros((bsz, S5_STATE), F32)
    for k in range(nk):
        xr_scr[k * bsz:(k + 1) * bsz, :] = xr
        xi_scr[k * bsz:(k + 1) * bsz, :] = xi
        xr, xi = (ar * xr - ai * xi + sr[k * bsz:(k + 1) * bsz, :],
                  ar * xi + ai * xr + si[k * bsz:(k + 1) * bsz, :])
    y = _dot(u, wt_ref[0])
    y = y + _nt(xr_scr[...].astype(BF16), or_ref[0]) + _nt(xi_scr[...].astype(BF16), oi_ref[0])
    y_ref[0] = y + d_ref[0] * u.astype(F32)


def _s5main(u_g, wt, s_r, s_i, o_r, o_i, ap_r, ap_i, d_t, bsz):
    g, m, rows = u_g.shape
    n = S5_STATE
    big = pl.BlockSpec((1, m, rows), lambda i: (i, 0, 0))
    tall = pl.BlockSpec((1, rows, n), lambda i: (i, 0, 0))
    vec = pl.BlockSpec((1, 1, n), lambda i: (i, 0, 0))
    return pl.pallas_call(
        functools.partial(_s5main_kernel, bsz=bsz),
        grid=(g,),
        in_specs=[big, pl.BlockSpec((1, rows, rows), lambda i: (i, 0, 0)), tall, tall, tall, tall, vec, vec,
                  pl.BlockSpec((1, 1, rows), lambda i: (i, 0, 0))],
        out_specs=big,
        out_shape=jax.ShapeDtypeStruct((g, m, rows), F32),
        scratch_shapes=[pltpu.VMEM((m, n), F32), pltpu.VMEM((m, n), F32)],
        compiler_params=_params(("parallel",)),
        name="s5_scan",
    )(u_g, wt, s_r, s_i, o_r, o_i, ap_r, ap_i, d_t)


def _s5post_kernel(y_ref, z_ref, gw_ref, gb_ref, o_ref):
    y = y_ref[...]
    y = 0.5 * y * (1.0 + jnp.tanh(math.sqrt(2.0 / math.pi) * (y + 0.044715 * (y * y * y))))
    y = y * jax.nn.sigmoid(_dot(y.astype(BF16), gw_ref[...]) + gb_ref[...])
    o_ref[...] = (y * _silu(z_ref[...].astype(F32))).astype(BF16)


def _s5post(y_pre, proj, glu_w, glu_b, tm=512):
    t, w = y_pre.shape
    return pl.pallas_call(
        _s5post_kernel,
        grid=(t // tm,),
        in_specs=[
            pl.BlockSpec((tm, w), lambda i: (i, 0)),
            pl.BlockSpec((tm, w), lambda i: (i, OD_ZD // S5_WIDTH)),
            pl.BlockSpec((w, w), lambda i: (0, 0)),
            pl.BlockSpec((1, w), lambda i: (0, 0)),
        ],
        out_specs=pl.BlockSpec((tm, w), lambda i: (i, 0)),
        out_shape=jax.ShapeDtypeStruct((t, w), BF16),
        compiler_params=_params(("parallel",)),
        name="s5_post",
    )(y_pre, proj, glu_w.astype(BF16), glu_b.reshape(1, w))


def _small_weights(cols):
    padded = jnp.pad(cols, ((0, 0), (0, LANES - cols.shape[1])))
    hi = padded.astype(BF16)
    lo = (padded - hi.astype(F32)).astype(BF16)
    return jnp.concatenate([hi, lo], axis=1)


def _even_layer(x2, mod, pre_g, post_g, in_w, conv_w, conv_b, dt_bias, a_log, d_skip, norm_g, out_w, bsz, seq):
    d = D_MODEL
    shift, scale, gate = (mod[:, i * d:(i + 1) * d].reshape(bsz, 1, d) for i in range(3))
    o_xbc = 2 * SSD_WIDTH
    o_dt = o_xbc + SSD_WIDTH + 2 * SSD_GROUPS * SSD_STATE
    o_q = o_dt + SSD_HEADS
    w = jnp.concatenate([
        in_w[:, :o_xbc],
        in_w[:, o_xbc:o_xbc + SSD_WIDTH],
        in_w[:, o_q:],
        in_w[:, o_xbc + SSD_WIDTH:o_dt],
    ], axis=1).astype(BF16)
    ws = _small_weights(in_w[:, o_dt:o_q])
    proj, small = _inproj(x2, scale, shift, pre_g.reshape(1, d), w, ws, seq)
    y_a = _ssd(proj, small, conv_w, conv_b, dt_bias, a_log, d_skip, norm_g, bsz, seq)
    y_b = _moba(proj, bsz, seq)
    ow = out_w.astype(BF16)
    return _outproj(y_a, y_b, ow[:SSD_WIDTH], ow[SSD_WIDTH:], x2, gate, post_g.reshape(1, d), seq)


def _odd_layer(x2, mod, pre_g, post_g, in_w, fgate_b, lam_re, lam_im, log_dt, b_re, b_im, c_re, c_im,
               d_skip, glu_w, glu_b, out_w, bsz, seq):
    d = D_MODEL
    shift, scale, gate = (mod[:, i * d:(i + 1) * d].reshape(bsz, 1, d) for i in range(3))
    o_f = D_MIX + 3 * FOX_WIDTH
    o_u = o_f + FOX_HEADS
    w = jnp.concatenate([in_w[:, :o_f], in_w[:, o_u:]], axis=1).astype(BF16)
    ws = _small_weights(in_w[:, o_f:o_u])
    proj, small = _inproj(x2, scale, shift, pre_g.reshape(1, d), w, ws, seq)

    negf = _fgate(small, fgate_b, bsz, seq)
    y_c = _fox(proj, negf, bsz, seq)

    lc = S5_CHUNK
    nk = seq // lc
    wt, s_r, s_i, o_r, o_i, ap_r, ap_i = _s5prep(lam_re, lam_im, log_dt, b_re, b_im, c_re, c_im)
    u = proj[:, OD_U:OD_U + S5_WIDTH].reshape(bsz, nk, lc, S5_GROUPS, S5_GROUP)
    u_g = u.transpose(3, 1, 0, 2, 4).reshape(S5_GROUPS, nk * bsz, lc * S5_GROUP)
    d_t = jnp.tile(d_skip.reshape(S5_GROUPS, 1, S5_GROUP), (1, lc, 1)).reshape(S5_GROUPS, 1, lc * S5_GROUP)
    y_g = _s5main(u_g, wt, s_r, s_i, o_r, o_i, ap_r, ap_i, d_t, bsz)
    y_pre = y_g.reshape(S5_GROUPS, nk, bsz, lc, S5_GROUP).transpose(2, 1, 3, 0, 4).reshape(bsz * seq, S5_WIDTH)
    y_d = _s5post(y_pre, proj, glu_w, glu_b)

    ow = out_w.astype(BF16)
    return _outproj(y_c, y_d, ow[:FOX_WIDTH], ow[FOX_WIDTH:], x2, gate, post_g.reshape(1, d), seq)


def kernel(x, c, ada_w, ada_b, pre_g, post_g, even_in_w, even_conv_w, even_conv_b, even_dt_bias, even_a_log,
           even_d_skip, even_norm_g, even_out_w, odd_in_w, odd_fgate_b, odd_lam_re, odd_lam_im, odd_log_dt,
           odd_b_re, odd_b_im, odd_c_re, odd_c_im, odd_d_skip, odd_glu_w, odd_glu_b, odd_out_w):
    bsz, seq, d = x.shape
    depth = ada_w.shape[0]
    mod = _ada_mod(c, ada_w, ada_b)
    x2 = x.reshape(bsz * seq, d)
    for layer in range(depth):
        i = layer // 2
        if layer % 2 == 0:
            x2 = _even_layer(x2, mod[layer], pre_g[layer], post_g[layer], even_in_w[i], even_conv_w[i],
                             even_conv_b[i], even_dt_bias[i], even_a_log[i], even_d_skip[i], even_norm_g[i],
                             even_out_w[i], bsz, seq)
        else:
            x2 = _odd_layer(x2, mod[layer], pre_g[layer], post_g[layer], odd_in_w[i], odd_fgate_b[i],
                            odd_lam_re[i], odd_lam_im[i], odd_log_dt[i], odd_b_re[i], odd_b_im[i], odd_c_re[i],
                            odd_c_im[i], odd_d_skip[i], odd_glu_w[i], odd_glu_b[i], odd_out_w[i], bsz, seq)
    return x2.reshape(bsz, seq, d)
```

```python
import functools
import math

import jax
import jax.numpy as jnp
from jax import lax
from jax.experimental import pallas as pl
from jax.experimental.pallas import tpu as pltpu

F32 = jnp.float32
BF16 = jnp.bfloat16

D_MODEL = 1024
HEAD_DIM = 64
D_MIX = 2 * D_MODEL
SSD_WIDTH = 1024
SSD_HEADS = 16
SSD_GROUPS = 2
SSD_STATE = 128
SSD_CONV = 4
SSD_CHUNK = 128
MOBA_WIDTH = 1024
MOBA_BLOCK = 256
MOBA_TOPK = 3
FOX_WIDTH = 1536
FOX_HEADS = 24
S5_WIDTH = 512
S5_GROUP = 16
S5_GROUP_SHIFT = 4
S5_GROUPS = 32
S5_STATE = 64
S5_CHUNK = 64
RMS_EPS = 1e-6
ATT_BLOCK = 256
ATT_BLOCK_SHIFT = 8
ATT_SCALE = 1.0 / math.sqrt(HEAD_DIM)
LANES = 128
NEG = -1e30
VMEM_LIMIT = 48 * 1024 * 1024

EV_ZA, EV_ZB, EV_XS, EV_Q, EV_K, EV_V, EV_BC, EV_N = 0, 1024, 2048, 3072, 4096, 5120, 6144, 6656
OD_ZC, OD_ZD, OD_Q, OD_K, OD_V, OD_U, OD_N = 0, 1536, 2048, 3584, 5120, 6656, 7168


def _nt(a, b):
    return lax.dot_general(a, b, (((1,), (1,)), ((), ())), preferred_element_type=F32)


def _dot(a, b):
    return jnp.dot(a, b, preferred_element_type=F32)


def _split3(x):
    hi = x.astype(BF16)
    r = x - hi.astype(F32)
    mid = r.astype(BF16)
    lo = (r - mid.astype(F32)).astype(BF16)
    return hi, mid, lo


def _dot_exact_rhs(x, m_bf16):
    hi, mid, lo = _split3(x)
    return _dot(hi, m_bf16) + _dot(mid, m_bf16) + _dot(lo, m_bf16)


def _silu(x):
    return x * jax.nn.sigmoid(x)


def _softplus(x):
    return jnp.maximum(x, 0.0) + jnp.log1p(jnp.exp(-jnp.abs(x)))


def _params(sem, limit=VMEM_LIMIT):
    return pltpu.CompilerParams(dimension_semantics=sem, vmem_limit_bytes=limit)


def _ada_kernel(c_ref, w_ref, b_ref, o_ref):
    cond = _silu(c_ref[...])
    hi, mid, lo = _split3(cond)
    w = w_ref[0]
    whi, wmid, wlo = _split3(w)
    acc = _dot(hi, whi) + _dot(hi, wmid) + _dot(mid, whi)
    acc = acc + _dot(hi, wlo) + _dot(mid, wmid) + _dot(lo, whi)
    o_ref[0] = acc + b_ref[0]


def _ada_mod(c, ada_w, ada_b):
    depth, d, d3 = ada_w.shape
    bsz = c.shape[0]
    nj = d3 // d
    return pl.pallas_call(
        _ada_kernel,
        grid=(depth, nj),
        in_specs=[
            pl.BlockSpec((bsz, d), lambda l, j: (0, 0)),
            pl.BlockSpec((1, d, d), lambda l, j: (l, 0, j)),
            pl.BlockSpec((1, 1, d), lambda l, j: (l, 0, j)),
        ],
        out_specs=pl.BlockSpec((1, bsz, d), lambda l, j: (l, 0, j)),
        out_shape=jax.ShapeDtypeStruct((depth, bsz, d3), F32),
        compiler_params=_params(("parallel", "parallel")),
        name="ada_mod",
    )(c, ada_w, ada_b.reshape(depth, 1, d3))


def _inproj_kernel(x_ref, sc_ref, sh_ref, g_ref, w_ref, ws_ref, o_ref, os_ref, h_scr):
    @pl.when(pl.program_id(1) == 0)
    def _():
        x = x_ref[...]
        ms = jnp.mean(x * x, axis=-1, keepdims=True)
        xn = x * lax.rsqrt(ms + RMS_EPS) * g_ref[...]
        h = xn * (1.0 + sc_ref[0]) + sh_ref[0]
        h_hi = h.astype(BF16)
        h_scr[...] = h_hi
        h_lo = (h - h_hi.astype(F32)).astype(BF16)
        ws = ws_ref[...]
        r = _dot(h_hi, ws)
        os_ref[...] = r[:, :LANES] + r[:, LANES:] + _dot(h_lo, ws[:, :LANES])

    o_ref[...] = _dot(h_scr[...], w_ref[...]).astype(BF16)


def _inproj(x2, scale, shift, g, w, ws, seq, tm=1024, tn=512):
    t, d = x2.shape
    n = w.shape[1]
    per = seq // tm
    return pl.pallas_call(
        _inproj_kernel,
        grid=(t // tm, n // tn),
        in_specs=[
            pl.BlockSpec((tm, d), lambda i, j: (i, 0)),
            pl.BlockSpec((1, 1, d), lambda i, j: (i // per, 0, 0)),
            pl.BlockSpec((1, 1, d), lambda i, j: (i // per, 0, 0)),
            pl.BlockSpec((1, d), lambda i, j: (0, 0)),
            pl.BlockSpec((d, tn), lambda i, j: (0, j)),
            pl.BlockSpec((d, 2 * LANES), lambda i, j: (0, 0)),
        ],
        out_specs=[
            pl.BlockSpec((tm, tn), lambda i, j: (i, j)),
            pl.BlockSpec((tm, LANES), lambda i, j: (i, 0)),
        ],
        out_shape=[
            jax.ShapeDtypeStruct((t, n), BF16),
            jax.ShapeDtypeStruct((t, LANES), F32),
        ],
        scratch_shapes=[pltpu.VMEM((tm, d), BF16)],
        compiler_params=_params(("parallel", "arbitrary")),
        name="in_proj",
    )(x2, scale, shift, g, w, ws)


def _outproj_kernel(a_ref, b_ref, wa_ref, wb_ref, x_ref, gate_ref, pg_ref, o_ref):
    y = _dot(a_ref[...], wa_ref[...]) + _dot(b_ref[...], wb_ref[...])
    ms = jnp.mean(y * y, axis=-1, keepdims=True)
    yn = y * lax.rsqrt(ms + RMS_EPS) * pg_ref[...]
    o_ref[...] = x_ref[...] + gate_ref[0] * yn


def _outproj(a, b, wa, wb, x2, gate, pg, seq, tm=512):
    t, d = x2.shape
    ka, kb = a.shape[1], b.shape[1]
    per = seq // tm
    return pl.pallas_call(
        _outproj_kernel,
        grid=(t // tm,),
        in_specs=[
            pl.BlockSpec((tm, ka), lambda i: (i, 0)),
            pl.BlockSpec((tm, kb), lambda i: (i, 0)),
            pl.BlockSpec((ka, d), lambda i: (0, 0)),
            pl.BlockSpec((kb, d), lambda i: (0, 0)),
            pl.BlockSpec((tm, d), lambda i: (i, 0)),
            pl.BlockSpec((1, 1, d), lambda i: (i // per, 0, 0)),
            pl.BlockSpec((1, d), lambda i: (0, 0)),
        ],
        out_specs=pl.BlockSpec((tm, d), lambda i: (i, 0)),
        out_shape=jax.ShapeDtypeStruct((t, d), F32),
        compiler_params=_params(("parallel",)),
        name="out_proj",
    )(a, b, wa, wb, x2, gate, pg)


def _causal_conv_silu(in_ref, ext_scr, tail_scr, w, b):
    lc = in_ref.shape[0]
    raw = in_ref[...].astype(F32)
    ext_scr[0:8, :] = tail_scr[...]
    ext_scr[8:8 + lc, :] = raw
    tail_scr[...] = raw[lc - 8:lc, :]
    acc = b
    for j in range(SSD_CONV):
        off = 8 - (SSD_CONV - 1) + j
        acc = acc + w[j:j + 1, :] * ext_scr[off:off + lc, :]
    return _silu(acc)


def _ssd_kernel(z_ref, xs_ref, bc_ref, dt_ref, cwx_ref, cwb_ref, cbx_ref, cbb_ref, dtb_ref, alog_ref,
                dexp_ref, ng_ref, e_ref, o_ref, tailx_scr, tailb_scr, extx_scr, extb_scr, state_scr):
    lc = SSD_CHUNK
    hw = SSD_HEADS // SSD_GROUPS * HEAD_DIM

    @pl.when(pl.program_id(1) == 0)
    def _():
        tailx_scr[...] = jnp.zeros_like(tailx_scr)
        tailb_scr[...] = jnp.zeros_like(tailb_scr)
        state_scr[...] = jnp.zeros_like(state_scr)

    xs = _causal_conv_silu(xs_ref, extx_scr, tailx_scr, cwx_ref[...], cbx_ref[...])
    bc = _causal_conv_silu(bc_ref, extb_scr, tailb_scr, cwb_ref[...], cbb_ref[...])

    dt = _softplus(dt_ref[...] + dtb_ref[...])
    adt = dt * (-jnp.exp(alog_ref[...]))
    rows = lax.broadcasted_iota(jnp.int32, (lc, lc), 0)
    cols = lax.broadcasted_iota(jnp.int32, (lc, lc), 1)
    lower = rows >= cols
    a_cum = _dot_exact_rhs_lhs(lower.astype(BF16), adt)
    a_cum_t = a_cum.T
    a_last = a_cum[lc - 1:lc, :]
    stacked = jnp.concatenate(
        [dt, jnp.exp(a_last - a_cum), jnp.exp(a_cum), jnp.broadcast_to(jnp.exp(a_last), (8, LANES))], axis=0)
    expanded = _dot_exact_rhs(stacked, e_ref[...])
    dt_e = expanded[0:lc]
    dec_e = expanded[lc:2 * lc]
    ea_e = expanded[2 * lc:3 * lc]
    al_e = expanded[3 * lc:3 * lc + 1]

    xdt = xs * dt_e
    xdt_b = xdt.astype(BF16)
    xd_b = (xdt * dec_e).astype(BF16)
    lane = lax.broadcasted_iota(jnp.int32, (1, LANES), 1)
    pieces = []
    for g in range(SSD_GROUPS):
        bg = bc[:, g * SSD_STATE:(g + 1) * SSD_STATE]
        cg = bc[:, (SSD_GROUPS + g) * SSD_STATE:(SSD_GROUPS + g + 1) * SSD_STATE]
        cg_b = cg.astype(BF16)
        cb = _nt(cg_b, bg.astype(BF16))
        st = state_scr[:, g * hw:(g + 1) * hw]
        y_off = _dot(cg_b, st.astype(BF16)) * ea_e[:, g * hw:(g + 1) * hw]
        new_st = _dot(bg.T.astype(BF16), xd_b[:, g * hw:(g + 1) * hw])
        state_scr[:, g * hw:(g + 1) * hw] = al_e[:, g * hw:(g + 1) * hw] * st + new_st
        for pair in range(SSD_HEADS // SSD_GROUPS // 2):
            acc = None
            c0 = g * hw + pair * LANES
            xp = xdt_b[:, c0:c0 + LANES]
            for half in range(2):
                h = g * (SSD_HEADS // SSD_GROUPS) + pair * 2 + half
                diff = a_cum[:, h:h + 1] - a_cum_t[h:h + 1, :]
                decay = jnp.exp(jnp.where(lower, diff, NEG))
                m = (cb * decay).astype(BF16)
                in_half = (lane >= half * HEAD_DIM) & (lane < (half + 1) * HEAD_DIM)
                part = _dot(m, jnp.where(in_half, xp, jnp.zeros_like(xp)))
                acc = part if acc is None else acc + part
            pieces.append(acc + y_off[:, pair * LANES:(pair + 1) * LANES])
    y = jnp.concatenate(pieces, axis=1) + dexp_ref[...] * xs
    y = y * _silu(z_ref[...].astype(F32))
    ms = jnp.mean(y * y, axis=-1, keepdims=True)
    o_ref[...] = (y * lax.rsqrt(ms + RMS_EPS) * ng_ref[...]).astype(BF16)


def _dot_exact_rhs_lhs(m_bf16, x):
    hi, mid, lo = _split3(x)
    return _dot(m_bf16, hi) + _dot(m_bf16, mid) + _dot(m_bf16, lo)


def _ssd(proj, small, conv_w, conv_b, dt_bias, a_log, d_skip, norm_g, bsz, seq):
    lc = SSD_CHUNK
    nc = seq // lc
    t = bsz * seq
    pad = LANES - SSD_HEADS
    dtb = jnp.pad(dt_bias, (0, pad)).reshape(1, LANES)
    alog = jnp.pad(a_log, (0, pad)).reshape(1, LANES)
    dexp = jnp.repeat(d_skip, HEAD_DIM).reshape(1, SSD_WIDTH)
    expand = (jnp.arange(LANES)[:, None] == (jnp.arange(SSD_WIDTH) // HEAD_DIM)[None, :]).astype(BF16)
    nbc = 2 * SSD_GROUPS * SSD_STATE
    row = lambda b, c: b * nc + c
    const = lambda b, c: (0, 0)
    return pl.pallas_call(
        _ssd_kernel,
        grid=(bsz, nc),
        in_specs=[
            pl.BlockSpec((lc, SSD_WIDTH), lambda b, c: (row(b, c), EV_ZA // SSD_WIDTH)),
            pl.BlockSpec((lc, SSD_WIDTH), lambda b, c: (row(b, c), EV_XS // SSD_WIDTH)),
            pl.BlockSpec((lc, nbc), lambda b, c: (row(b, c), EV_BC // nbc)),
            pl.BlockSpec((lc, LANES), lambda b, c: (row(b, c), 0)),
            pl.BlockSpec((SSD_CONV, SSD_WIDTH), const),
            pl.BlockSpec((SSD_CONV, nbc), const),
            pl.BlockSpec((1, SSD_WIDTH), const),
            pl.BlockSpec((1, nbc), const),
            pl.BlockSpec((1, LANES), const),
            pl.BlockSpec((1, LANES), const),
            pl.BlockSpec((1, SSD_WIDTH), const),
            pl.BlockSpec((1, SSD_WIDTH), const),
            pl.BlockSpec((LANES, SSD_WIDTH), const),
        ],
        out_specs=pl.BlockSpec((lc, SSD_WIDTH), lambda b, c: (row(b, c), 0)),
        out_shape=jax.ShapeDtypeStruct((t, SSD_WIDTH), BF16),
        scratch_shapes=[
            pltpu.VMEM((8, SSD_WIDTH), F32),
            pltpu.VMEM((8, nbc), F32),
            pltpu.VMEM((lc + 8, SSD_WIDTH), F32),
            pltpu.VMEM((lc + 8, nbc), F32),
            pltpu.VMEM((SSD_STATE, SSD_WIDTH), F32),
        ],
        compiler_params=_params(("parallel", "arbitrary")),
        name="ssd",
    )(proj, proj, proj, small,
      conv_w[:, :SSD_WIDTH], conv_w[:, SSD_WIDTH:], conv_b[:SSD_WIDTH].reshape(1, -1),
      conv_b[SSD_WIDTH:].reshape(1, -1), dtb, alog, dexp, norm_g.reshape(1, -1), expand)


def _fold_rows(x, op):
    out = x[0:8, :]
    for i in range(1, x.shape[0] // 8):
        out = op(out, x[8 * i:8 * (i + 1), :])
    return out


def _attend_head(half, q_ref, k_ref, vt_scr, s_scr, ot_scr, bias_fn):
    blk = ATT_BLOCK
    nb = vt_scr.shape[0]
    lo = pl.multiple_of(half * HEAD_DIM, HEAD_DIM)
    in_half = _head_lanes(half)
    causal = (lax.broadcasted_iota(jnp.int32, (blk, blk), 0)
              <= lax.broadcasted_iota(jnp.int32, (blk, blk), 1))
    for qb in range(nb):
        slot = qb % 2
        q = q_ref[qb * blk:(qb + 1) * blk, :]
        qi = jnp.where(in_half, q, jnp.zeros_like(q))
        m8 = None
        for n in range(qb + 1):
            s = _nt(k_ref[n * blk:(n + 1) * blk, :], qi)
            bias = bias_fn(qb, n)
            if bias is not None:
                s = s + bias
            if n == qb:
                s = jnp.where(causal, s, NEG)
            s_scr[slot, n] = s
            part = _fold_rows(s, jnp.maximum)
            m8 = part if m8 is None else jnp.maximum(m8, part)
        m = jnp.max(m8, axis=0, keepdims=True)
        l8 = None
        acc = None
        for n in range(qb + 1):
            p = jnp.exp(s_scr[slot, n] - m)
            part = _fold_rows(p, jnp.add)
            l8 = part if l8 is None else l8 + part
            pv = _dot(vt_scr[n, pl.ds(lo, HEAD_DIM), :], p.astype(BF16))
            acc = pv if acc is None else acc + pv
        l = jnp.sum(l8, axis=0, keepdims=True)
        ot_scr[pl.ds(lo, HEAD_DIM), qb * blk:(qb + 1) * blk] = acc / l


def _store_vt(v_ref, vt_scr):
    for n in range(vt_scr.shape[0]):
        vb = v_ref[n * ATT_BLOCK:(n + 1) * ATT_BLOCK, :].astype(F32)
        vt_scr[n] = vb.T.astype(BF16)


def _gated_output(ot_scr, z_ref, o_ref):
    for i in range(ot_scr.shape[1] // ATT_BLOCK):
        rows = slice(i * ATT_BLOCK, (i + 1) * ATT_BLOCK)
        o = ot_scr[:, rows].T
        o_ref[rows, :] = (o * _silu(z_ref[rows, :].astype(F32))).astype(BF16)


def _head_lanes(half):
    lane = lax.broadcasted_iota(jnp.int32, (1, LANES), 1)
    return (lane >= half * HEAD_DIM) & (lane < (half + 1) * HEAD_DIM)


def _moba_kernel(q_ref, k_ref, v_ref, z_ref, o_ref, vt_scr, kbar_scr, bias_scr, s_scr, ot_scr):
    nb = vt_scr.shape[0]
    seq = q_ref.shape[0]
    _store_vt(v_ref, vt_scr)
    for n in range(nb):
        kb = k_ref[n * ATT_BLOCK:(n + 1) * ATT_BLOCK, :].astype(F32)
        kbar_scr[n:n + 1, :] = jnp.mean(kb, axis=0, keepdims=True)
    n_idx = lax.broadcasted_iota(jnp.int32, (nb, seq), 0)
    q_blk = lax.broadcasted_iota(jnp.int32, (nb, seq), 1) >> ATT_BLOCK_SHIFT

    def head(half, carry):
        in_half = _head_lanes(half)
        q = q_ref[...]
        qi = jnp.where(in_half, q, jnp.zeros_like(q))
        kbar = jnp.where(in_half, kbar_scr[...], 0.0)
        kb_hi = kbar.astype(BF16)
        kb_lo = (kbar - kb_hi.astype(F32)).astype(BF16)
        gate = _nt(kb_hi, qi) + _nt(kb_lo, qi)
        rank = jnp.zeros((nb, seq), F32)
        for mth in range(nb):
            gm = gate[mth:mth + 1, :]
            beats = (gm > gate) | ((gm == gate) & (mth < n_idx))
            rank = rank + jnp.where(beats & (mth < q_blk), 1.0, 0.0)
        chosen = (rank < float(MOBA_TOPK)) & (n_idx < q_blk)
        bias_scr[half] = jnp.where(chosen, 0.0, NEG)

        def bias_fn(qb, n):
            if n == qb or qb <= MOBA_TOPK:
                return None
            return bias_scr[half, n:n + 1, qb * ATT_BLOCK:(qb + 1) * ATT_BLOCK]

        _attend_head(half, q_ref, k_ref, vt_scr, s_scr, ot_scr, bias_fn)
        return carry

    lax.fori_loop(0, 2, head, 0)
    _gated_output(ot_scr, z_ref, o_ref)


def _moba(proj, bsz, seq):
    blk = ATT_BLOCK
    nb = seq // blk
    pairs = MOBA_WIDTH // LANES
    t = bsz * seq
    return pl.pallas_call(
        _moba_kernel,
        grid=(bsz, pairs),
        in_specs=[
            pl.BlockSpec((seq, LANES), lambda b, p: (b, EV_Q // LANES + p)),
            pl.BlockSpec((seq, LANES), lambda b, p: (b, EV_K // LANES + p)),
            pl.BlockSpec((seq, LANES), lambda b, p: (b, EV_V // LANES + p)),
            pl.BlockSpec((seq, LANES), lambda b, p: (b, EV_ZB // LANES + p)),
        ],
        out_specs=pl.BlockSpec((seq, LANES), lambda b, p: (b, p)),
        out_shape=jax.ShapeDtypeStruct((t, MOBA_WIDTH), BF16),
        scratch_shapes=[
            pltpu.VMEM((nb, LANES, blk), BF16),
            pltpu.VMEM((nb, LANES), F32),
            pltpu.VMEM((2, nb, seq), F32),
            pltpu.VMEM((2, nb, blk, blk), F32),
            pltpu.VMEM((LANES, seq), F32),
        ],
        compiler_params=_params(("parallel", "parallel")),
        name="moba",
    )(proj, proj, proj, proj)


def _fgate_kernel(f_ref, fb_ref, o_ref, carry_scr):
    @pl.when(pl.program_id(1) == 0)
    def _():
        carry_scr[...] = jnp.zeros_like(carry_scr)

    blk = f_ref.shape[0]
    nlf = _softplus(-(f_ref[...] + fb_ref[...]))
    rows = lax.broadcasted_iota(jnp.int32, (blk, blk), 0)
    cols = lax.broadcasted_iota(jnp.int32, (blk, blk), 1)
    csum = _dot_exact_rhs_lhs((rows >= cols).astype(BF16), nlf) + carry_scr[0:1, :]
    o_ref[...] = csum
    carry_scr[...] = jnp.broadcast_to(csum[blk - 1:blk, :], carry_scr.shape)


def _fgate(small, fgate_b, bsz, seq):
    blk = ATT_BLOCK
    nb = seq // blk
    fb = jnp.pad(fgate_b, (0, LANES - FOX_HEADS)).reshape(1, LANES)
    return pl.pallas_call(
        _fgate_kernel,
        grid=(bsz, nb),
        in_specs=[
            pl.BlockSpec((blk, LANES), lambda b, i: (b * nb + i, 0)),
            pl.BlockSpec((1, LANES), lambda b, i: (0, 0)),
        ],
        out_specs=pl.BlockSpec((blk, LANES), lambda b, i: (b * nb + i, 0)),
        out_shape=jax.ShapeDtypeStruct((bsz * seq, LANES), F32),
        scratch_shapes=[pltpu.VMEM((8, LANES), F32)],
        compiler_params=_params(("parallel", "arbitrary")),
        name="fox_gate",
    )(small, fb)


def _fox_kernel(q_ref, k_ref, v_ref, z_ref, nf_ref, o_ref, vt_scr, fb_scr, s_scr, ot_scr):
    pair = pl.program_id(1)
    nb = vt_scr.shape[0]
    blk = ATT_BLOCK
    _store_vt(v_ref, vt_scr)
    sel_row = lax.broadcasted_iota(jnp.int32, (LANES, LANES), 0)

    def head(half, carry):
        onehot = (sel_row == pair * 2 + half).astype(BF16)
        for n in range(nb):
            fb_scr[n] = _dot_exact_rhs(nf_ref[n * blk:(n + 1) * blk, :], onehot)

        def bias_fn(qb, n):
            fb = fb_scr[n]
            return jnp.concatenate([fb, fb], axis=1)

        _attend_head(half, q_ref, k_ref, vt_scr, s_scr, ot_scr, bias_fn)
        return carry

    lax.fori_loop(0, 2, head, 0)
    _gated_output(ot_scr, z_ref, o_ref)


def _fox(proj, negf, bsz, seq):
    blk = ATT_BLOCK
    nb = seq // blk
    pairs = FOX_WIDTH // LANES
    t = bsz * seq
    return pl.pallas_call(
        _fox_kernel,
        grid=(bsz, pairs),
        in_specs=[
            pl.BlockSpec((seq, LANES), lambda b, p: (b, OD_Q // LANES + p)),
            pl.BlockSpec((seq, LANES), lambda b, p: (b, OD_K // LANES + p)),
            pl.BlockSpec((seq, LANES), lambda b, p: (b, OD_V // LANES + p)),
            pl.BlockSpec((seq, LANES), lambda b, p: (b, OD_ZC // LANES + p)),
            pl.BlockSpec((seq, LANES), lambda b, p: (b, 0)),
        ],
        out_specs=pl.BlockSpec((seq, LANES), lambda b, p: (b, p)),
        out_shape=jax.ShapeDtypeStruct((t, FOX_WIDTH), BF16),
        scratch_shapes=[
            pltpu.VMEM((nb, LANES, blk), BF16),
            pltpu.VMEM((nb, blk, LANES), F32),
            pltpu.VMEM((2, nb, blk, blk), F32),
            pltpu.VMEM((LANES, seq), F32),
        ],
        compiler_params=_params(("parallel", "parallel")),
        name="fox",
    )(proj, proj, proj, proj, negf)


def _s5prep_kernel(lr_ref, li_ref, ldt_ref, btr_ref, bti_ref, cr_ref, ci_ref,
                   wt_ref, str_ref, sti_ref, or_ref, oi_ref, apr_ref, api_ref):
    lc = S5_CHUNK
    rows = lc * S5_GROUP
    lr = lr_ref[0]
    li = li_ref[0]
    dt = jnp.exp(ldt_ref[0])
    rate = lr * dt
    freq = li * dt

    def power(e):
        mag = jnp.exp(rate * e)
        ang = freq * e
        return mag * jnp.cos(ang), mag * jnp.sin(ang)

    ar, ai = power(jnp.ones((1, S5_STATE), F32))
    den = lr * lr + li * li
    qr = ((ar - 1.0) * lr + ai * li) / den
    qi = (ai * lr - (ar - 1.0) * li) / den
    btr = btr_ref[0]
    bti = bti_ref[0]
    bbr = qr * btr - qi * bti
    bbi = qr * bti + qi * btr
    tile = lambda a: jnp.concatenate([a] * lc, axis=0)
    bbr_t, bbi_t = tile(bbr), tile(bbi)
    cr_t, ci_t = tile(cr_ref[0]), tile(ci_ref[0])
    tt = (lax.broadcasted_iota(jnp.int32, (rows, S5_STATE), 0) >> S5_GROUP_SHIFT).astype(F32)
    mid = float(lc // 2)

    pr, pi = power(tt - mid)
    ctr = cr_t * pr - ci_t * pi
    cti = cr_t * pi + ci_t * pr
    pr, pi = power(mid - tt)
    bsr = bbr_t * pr - bbi_t * pi
    bsi = bbr_t * pi + bbi_t * pr
    wt = _nt(bsr.astype(BF16), ctr.astype(BF16)) - _nt(bsi.astype(BF16), cti.astype(BF16))
    s_of_row = lax.broadcasted_iota(jnp.int32, (rows, rows), 0) >> S5_GROUP_SHIFT
    t_of_col = lax.broadcasted_iota(jnp.int32, (rows, rows), 1) >> S5_GROUP_SHIFT
    wt_ref[0] = jnp.where(t_of_col >= s_of_row, wt, 0.0).astype(BF16)

    pr, pi = power(float(lc - 1) - tt)
    str_ref[0] = (bbr_t * pr - bbi_t * pi).astype(BF16)
    sti_ref[0] = (bbr_t * pi + bbi_t * pr).astype(BF16)
    pr, pi = power(tt + 1.0)
    or_ref[0] = (cr_t * pr - ci_t * pi).astype(BF16)
    oi_ref[0] = (-(cr_t * pi + ci_t * pr)).astype(BF16)
    pr, pi = power(jnp.full((1, S5_STATE), float(lc), F32))
    apr_ref[0] = pr
    api_ref[0] = pi


def _s5prep(lam_re, lam_im, log_dt, b_re, b_im, c_re, c_im):
    g, n = lam_re.shape
    rows = S5_CHUNK * S5_GROUP
    vec = pl.BlockSpec((1, 1, n), lambda i: (i, 0, 0))
    mat = pl.BlockSpec((1, S5_GROUP, n), lambda i: (i, 0, 0))
    tall = pl.BlockSpec((1, rows, n), lambda i: (i, 0, 0))
    return pl.pallas_call(
        _s5prep_kernel,
        grid=(g,),
        in_specs=[vec, vec, vec, mat, mat, mat, mat],
        out_specs=[pl.BlockSpec((1, rows, rows), lambda i: (i, 0, 0)), tall, tall, tall, tall, vec, vec],
        out_shape=[
            jax.ShapeDtypeStruct((g, rows, rows), BF16),
            jax.ShapeDtypeStruct((g, rows, n), BF16),
            jax.ShapeDtypeStruct((g, rows, n), BF16),
            jax.ShapeDtypeStruct((g, rows, n), BF16),
            jax.ShapeDtypeStruct((g, rows, n), BF16),
            jax.ShapeDtypeStruct((g, 1, n), F32),
            jax.ShapeDtypeStruct((g, 1, n), F32),
        ],
        compiler_params=_params(("parallel",)),
        name="s5_prep",
    )(lam_re.reshape(g, 1, n), lam_im.reshape(g, 1, n),
      jnp.broadcast_to(log_dt[:, None, None], (g, 1, n)),
      jnp.swapaxes(b_re, 1, 2), jnp.swapaxes(b_im, 1, 2), c_re, c_im)


def _s5main_kernel(u_ref, wt_ref, str_ref, sti_ref, or_ref, oi_ref, apr_ref, api_ref, d_ref, y_ref,
                   xr_scr, xi_scr, *, bsz):
    u = u_ref[0]
    sr = _dot(u, str_ref[0])
    si = _dot(u, sti_ref[0])
    ar = apr_ref[0]
    ai = api_ref[0]
    nk = u.shape[0] // bsz
    xr = jnp.zeros((bsz, S5_STATE), F32)
    xi = jnp.zeros((bsz, S5_STATE), F32)
    for k in range(nk):
        xr_scr[k * bsz:(k + 1) * bsz, :] = xr
        xi_scr[k * bsz:(k + 1) * bsz, :] = xi
        xr, xi = (ar * xr - ai * xi + sr[k * bsz:(k + 1) * bsz, :],
                  ar * xi + ai * xr + si[k * bsz:(k + 1) * bsz, :])
    y = _dot(u, wt_ref[0])
    y = y + _nt(xr_scr[...].astype(BF16), or_ref[0]) + _nt(xi_scr[...].astype(BF16), oi_ref[0])
    y_ref[0] = y + d_ref[0] * u.astype(F32)


def _s5main(u_g, wt, s_r, s_i, o_r, o_i, ap_r, ap_i, d_t, bsz):
    g, m, rows = u_g.shape
    n = S5_STATE
    big = pl.BlockSpec((1, m, rows), lambda i: (i, 0, 0))
    tall = pl.BlockSpec((1, rows, n), lambda i: (i, 0, 0))
    vec = pl.BlockSpec((1, 1, n), lambda i: (i, 0, 0))
    return pl.pallas_call(
        functools.partial(_s5main_kernel, bsz=bsz),
        grid=(g,),
        in_specs=[big, pl.BlockSpec((1, rows, rows), lambda i: (i, 0, 0)), tall, tall, tall, tall, vec, vec,
                  pl.BlockSpec((1, 1, rows), lambda i: (i, 0, 0))],
        out_specs=big,
        out_shape=jax.ShapeDtypeStruct((g, m, rows), F32),
        scratch_shapes=[pltpu.VMEM((m, n), F32), pltpu.VMEM((m, n), F32)],
        compiler_params=_params(("parallel",)),
        name="s5_scan",
    )(u_g, wt, s_r, s_i, o_r, o_i, ap_r, ap_i, d_t)


def _s5post_kernel(y_ref, z_ref, gw_ref, gb_ref, o_ref):
    y = y_ref[...]
    y = 0.5 * y * (1.0 + jnp.tanh(math.sqrt(2.0 / math.pi) * (y + 0.044715 * (y * y * y))))
    y = y * jax.nn.sigmoid(_dot(y.astype(BF16), gw_ref[...]) + gb_ref[...])
    o_ref[...] = (y * _silu(z_ref[...].astype(F32))).astype(BF16)


def _s5post(y_pre, proj, glu_w, glu_b, tm=512):
    t, w = y_pre.shape
    return pl.pallas_call(
        _s5post_kernel,
        grid=(t // tm,),
        in_specs=[
            pl.BlockSpec((tm, w), lambda i: (i, 0)),
            pl.BlockSpec((tm, w), lambda i: (i, OD_ZD // S5_WIDTH)),
            pl.BlockSpec((w, w), lambda i: (0, 0)),
            pl.BlockSpec((1, w), lambda i: (0, 0)),
        ],
        out_specs=pl.BlockSpec((tm, w), lambda i: (i, 0)),
        out_shape=jax.ShapeDtypeStruct((t, w), BF16),
        compiler_params=_params(("parallel",)),
        name="s5_post",
    )(y_pre, proj, glu_w.astype(BF16), glu_b.reshape(1, w))


def _small_weights(cols):
    padded = jnp.pad(cols, ((0, 0), (0, LANES - cols.shape[1])))
    hi = padded.astype(BF16)
    lo = (padded - hi.astype(F32)).astype(BF16)
    return jnp.concatenate([hi, lo], axis=1)


def _even_layer(x2, mod, pre_g, post_g, in_w, conv_w, conv_b, dt_bias, a_log, d_skip, norm_g, out_w, bsz, seq):
    d = D_MODEL
    shift, scale, gate = (mod[:, i * d:(i + 1) * d].reshape(bsz, 1, d) for i in range(3))
    o_xbc = 2 * SSD_WIDTH
    o_dt = o_xbc + SSD_WIDTH + 2 * SSD_GROUPS * SSD_STATE
    o_q = o_dt + SSD_HEADS
    w = jnp.concatenate([
        in_w[:, :o_xbc],
        in_w[:, o_xbc:o_xbc + SSD_WIDTH],
        in_w[:, o_q:o_q + MOBA_WIDTH] * ATT_SCALE,
        in_w[:, o_q + MOBA_WIDTH:],
        in_w[:, o_xbc + SSD_WIDTH:o_dt],
    ], axis=1).astype(BF16)
    ws = _small_weights(in_w[:, o_dt:o_q])
    proj, small = _inproj(x2, scale, shift, pre_g.reshape(1, d), w, ws, seq)
    y_a = _ssd(proj, small, conv_w, conv_b, dt_bias, a_log, d_skip, norm_g, bsz, seq)
    y_b = _moba(proj, bsz, seq)
    ow = out_w.astype(BF16)
    return _outproj(y_a, y_b, ow[:SSD_WIDTH], ow[SSD_WIDTH:], x2, gate, post_g.reshape(1, d), seq)


def _odd_layer(x2, mod, pre_g, post_g, in_w, fgate_b, lam_re, lam_im, log_dt, b_re, b_im, c_re, c_im,
               d_skip, glu_w, glu_b, out_w, bsz, seq):
    d = D_MODEL
    shift, scale, gate = (mod[:, i * d:(i + 1) * d].reshape(bsz, 1, d) for i in range(3))
    o_f = D_MIX + 3 * FOX_WIDTH
    o_u = o_f + FOX_HEADS
    w = jnp.concatenate([
        in_w[:, :D_MIX],
        in_w[:, D_MIX:D_MIX + FOX_WIDTH] * ATT_SCALE,
        in_w[:, D_MIX + FOX_WIDTH:o_f],
        in_w[:, o_u:],
    ], axis=1).astype(BF16)
    ws = _small_weights(in_w[:, o_f:o_u])
    proj, small = _inproj(x2, scale, shift, pre_g.reshape(1, d), w, ws, seq)

    negf = _fgate(small, fgate_b, bsz, seq)
    y_c = _fox(proj, negf, bsz, seq)

    lc = S5_CHUNK
    nk = seq // lc
    wt, s_r, s_i, o_r, o_i, ap_r, ap_i = _s5prep(lam_re, lam_im, log_dt, b_re, b_im, c_re, c_im)
    u = proj[:, OD_U:OD_U + S5_WIDTH].reshape(bsz, nk, lc, S5_GROUPS, S5_GROUP)
    u_g = u.transpose(3, 1, 0, 2, 4).reshape(S5_GROUPS, nk * bsz, lc * S5_GROUP)
    d_t = jnp.tile(d_skip.reshape(S5_GROUPS, 1, S5_GROUP), (1, lc, 1)).reshape(S5_GROUPS, 1, lc * S5_GROUP)
    y_g = _s5main(u_g, wt, s_r, s_i, o_r, o_i, ap_r, ap_i, d_t, bsz)
    y_pre = y_g.reshape(S5_GROUPS, nk, bsz, lc, S5_GROUP).transpose(2, 1, 3, 0, 4).reshape(bsz * seq, S5_WIDTH)
    y_d = _s5post(y_pre, proj, glu_w, glu_b)

    ow = out_w.astype(BF16)
    return _outproj(y_c, y_d, ow[:FOX_WIDTH], ow[FOX_WIDTH:], x2, gate, post_g.reshape(1, d), seq)


def kernel(x, c, ada_w, ada_b, pre_g, post_g, even_in_w, even_conv_w, even_conv_b, even_dt_bias, even_a_log,
           even_d_skip, even_norm_g, even_out_w, odd_in_w, odd_fgate_b, odd_lam_re, odd_lam_im, odd_log_dt,
           odd_b_re, odd_b_im, odd_c_re, odd_c_im, odd_d_skip, odd_glu_w, odd_glu_b, odd_out_w):
    bsz, seq, d = x.shape
    depth = ada_w.shape[0]
    mod = _ada_mod(c, ada_w, ada_b)
    x2 = x.reshape(bsz * seq, d)
    for layer in range(depth):
        i = layer // 2
        if layer % 2 == 0:
            x2 = _even_layer(x2, mod[layer], pre_g[layer], post_g[layer], even_in_w[i], even_conv_w[i],
                             even_conv_b[i], even_dt_bias[i], even_a_log[i], even_d_skip[i], even_norm_g[i],
                             even_out_w[i], bsz, seq)
        else:
            x2 = _odd_layer(x2, mod[layer], pre_g[layer], post_g[layer], odd_in_w[i], odd_fgate_b[i],
                            odd_lam_re[i], odd_lam_im[i], odd_log_dt[i], odd_b_re[i], odd_b_im[i], odd_c_re[i],
                            odd_c_im[i], odd_d_skip[i], odd_glu_w[i], odd_glu_b[i], odd_out_w[i], bsz, seq)
    return x2.reshape(bsz, seq, d)
```

```python
import functools
import math

import jax
import jax.numpy as jnp
from jax import lax
from jax.experimental import pallas as pl
from jax.experimental.pallas import tpu as pltpu

F32 = jnp.float32
BF16 = jnp.bfloat16

D_MODEL = 1024
HEAD_DIM = 64
D_MIX = 2 * D_MODEL
SSD_WIDTH = 1024
SSD_HEADS = 16
SSD_GROUPS = 2
SSD_STATE = 128
SSD_CONV = 4
SSD_CHUNK = 128
MOBA_WIDTH = 1024
MOBA_BLOCK = 256
MOBA_TOPK = 3
FOX_WIDTH = 1536
FOX_HEADS = 24
S5_WIDTH = 512
S5_GROUP = 16
S5_GROUP_SHIFT = 4
S5_GROUPS = 32
S5_STATE = 64
S5_STATE_SHIFT = 6
S5_SETS = 4
S5_STEPS = 64
RMS_EPS = 1e-6
ATT_BLOCK = 256
ATT_BLOCK_SHIFT = 8
LOG2E = math.log2(math.e)
ATT_SCALE = LOG2E / math.sqrt(HEAD_DIM)
VT_ROWS = HEAD_DIM + 16
LANES = 128
NEG = -1e30
VMEM_LIMIT = 48 * 1024 * 1024

EV_ZA, EV_ZB, EV_XS, EV_Q, EV_K, EV_V, EV_BC, EV_N = 0, 1024, 2048, 3072, 4096, 5120, 6144, 6656
OD_ZC, OD_ZD, OD_Q, OD_K, OD_V, OD_U, OD_N = 0, 1536, 2048, 3584, 5120, 6656, 7168


def _nt(a, b):
    return lax.dot_general(a, b, (((1,), (1,)), ((), ())), preferred_element_type=F32)


def _dot(a, b):
    return jnp.dot(a, b, preferred_element_type=F32)


def _split3(x):
    hi = x.astype(BF16)
    r = x - hi.astype(F32)
    mid = r.astype(BF16)
    lo = (r - mid.astype(F32)).astype(BF16)
    return hi, mid, lo


def _dot_exact_rhs(x, m_bf16):
    hi, mid, lo = _split3(x)
    return _dot(hi, m_bf16) + _dot(mid, m_bf16) + _dot(lo, m_bf16)


def _silu(x):
    return x * jax.nn.sigmoid(x)


def _softplus(x):
    return jnp.maximum(x, 0.0) + jnp.log1p(jnp.exp(-jnp.abs(x)))


def _params(sem, limit=VMEM_LIMIT):
    return pltpu.CompilerParams(dimension_semantics=sem, vmem_limit_bytes=limit)


def _ada_kernel(c_ref, w_ref, b_ref, o_ref):
    cond = _silu(c_ref[...])
    hi, mid, lo = _split3(cond)
    w = w_ref[0]
    whi, wmid, wlo = _split3(w)
    acc = _dot(hi, whi) + _dot(hi, wmid) + _dot(mid, whi)
    acc = acc + _dot(hi, wlo) + _dot(mid, wmid) + _dot(lo, whi)
    o_ref[0] = acc + b_ref[0]


def _ada_mod(c, ada_w, ada_b):
    depth, d, d3 = ada_w.shape
    bsz = c.shape[0]
    nj = d3 // d
    return pl.pallas_call(
        _ada_kernel,
        grid=(depth, nj),
        in_specs=[
            pl.BlockSpec((bsz, d), lambda l, j: (0, 0)),
            pl.BlockSpec((1, d, d), lambda l, j: (l, 0, j)),
            pl.BlockSpec((1, 1, d), lambda l, j: (l, 0, j)),
        ],
        out_specs=pl.BlockSpec((1, bsz, d), lambda l, j: (l, 0, j)),
        out_shape=jax.ShapeDtypeStruct((depth, bsz, d3), F32),
        compiler_params=_params(("parallel", "parallel")),
        name="ada_mod",
    )(c, ada_w, ada_b.reshape(depth, 1, d3))


def _inproj_kernel(x_ref, sc_ref, sh_ref, g_ref, w_ref, ws_ref, o_ref, os_ref, h_scr):
    @pl.when(pl.program_id(1) == 0)
    def _():
        x = x_ref[...]
        ms = jnp.mean(x * x, axis=-1, keepdims=True)
        xn = x * lax.rsqrt(ms + RMS_EPS) * g_ref[...]
        h = xn * (1.0 + sc_ref[0]) + sh_ref[0]
        h_b = h.astype(BF16)
        h_scr[...] = h_b
        r = _dot(h_b, ws_ref[...])
        os_ref[...] = r[:, :LANES] + r[:, LANES:]

    o_ref[...] = _dot(h_scr[...], w_ref[...]).astype(BF16)


def _inproj(x2, scale, shift, g, w, ws, seq, tm=1024):
    t, d = x2.shape
    n = w.shape[1]
    tn = n // 4
    per = seq // tm
    return pl.pallas_call(
        _inproj_kernel,
        grid=(t // tm, n // tn),
        in_specs=[
            pl.BlockSpec((tm, d), lambda i, j: (i, 0)),
            pl.BlockSpec((1, 1, d), lambda i, j: (i // per, 0, 0)),
            pl.BlockSpec((1, 1, d), lambda i, j: (i // per, 0, 0)),
            pl.BlockSpec((1, d), lambda i, j: (0, 0)),
            pl.BlockSpec((d, tn), lambda i, j: (0, j)),
            pl.BlockSpec((d, 2 * LANES), lambda i, j: (0, 0)),
        ],
        out_specs=[
            pl.BlockSpec((tm, tn), lambda i, j: (i, j)),
            pl.BlockSpec((tm, LANES), lambda i, j: (i, 0)),
        ],
        out_shape=[
            jax.ShapeDtypeStruct((t, n), BF16),
            jax.ShapeDtypeStruct((t, LANES), F32),
        ],
        scratch_shapes=[pltpu.VMEM((tm, d), BF16)],
        compiler_params=_params(("parallel", "arbitrary")),
        name="in_proj",
    )(x2, scale, shift, g, w, ws)


def _outproj_kernel(a_ref, b_ref, wa_ref, wb_ref, x_ref, gate_ref, pg_ref, o_ref):
    y = _dot(a_ref[...], wa_ref[...]) + _dot(b_ref[...], wb_ref[...])
    ms = jnp.mean(y * y, axis=-1, keepdims=True)
    yn = y * lax.rsqrt(ms + RMS_EPS) * pg_ref[...]
    o_ref[...] = x_ref[...] + gate_ref[0] * yn


def _outproj(a, b, wa, wb, x2, gate, pg, seq, tm=512):
    t, d = x2.shape
    ka, kb = a.shape[1], b.shape[1]
    per = seq // tm
    return pl.pallas_call(
        _outproj_kernel,
        grid=(t // tm,),
        in_specs=[
            pl.BlockSpec((tm, ka), lambda i: (i, 0)),
            pl.BlockSpec((tm, kb), lambda i: (i, 0)),
            pl.BlockSpec((ka, d), lambda i: (0, 0)),
            pl.BlockSpec((kb, d), lambda i: (0, 0)),
            pl.BlockSpec((tm, d), lambda i: (i, 0)),
            pl.BlockSpec((1, 1, d), lambda i: (i // per, 0, 0)),
            pl.BlockSpec((1, d), lambda i: (0, 0)),
        ],
        out_specs=pl.BlockSpec((tm, d), lambda i: (i, 0)),
        out_shape=jax.ShapeDtypeStruct((t, d), F32),
        compiler_params=_params(("parallel",)),
        name="out_proj",
    )(a, b, wa, wb, x2, gate, pg)


def _causal_conv_silu(in_ref, ext_scr, tail_scr, w, b):
    lc = in_ref.shape[0]
    raw = in_ref[...].astype(F32)
    ext_scr[0:8, :] = tail_scr[...]
    ext_scr[8:8 + lc, :] = raw
    tail_scr[...] = raw[lc - 8:lc, :]
    acc = b
    for j in range(SSD_CONV):
        off = 8 - (SSD_CONV - 1) + j
        acc = acc + w[j:j + 1, :] * ext_scr[off:off + lc, :]
    return _silu(acc)


def _ssd_kernel(z_ref, xs_ref, bc_ref, dt_ref, cwx_ref, cwb_ref, cbx_ref, cbb_ref, dtb_ref, alog_ref,
                dexp_ref, ng_ref, e_ref, o_ref, tailx_scr, tailb_scr, extx_scr, extb_scr, state_scr):
    lc = SSD_CHUNK
    hw = SSD_HEADS // SSD_GROUPS * HEAD_DIM

    @pl.when(pl.program_id(1) == 0)
    def _():
        tailx_scr[...] = jnp.zeros_like(tailx_scr)
        tailb_scr[...] = jnp.zeros_like(tailb_scr)
        state_scr[...] = jnp.zeros_like(state_scr)

    xs = _causal_conv_silu(xs_ref, extx_scr, tailx_scr, cwx_ref[...], cbx_ref[...])
    bc = _causal_conv_silu(bc_ref, extb_scr, tailb_scr, cwb_ref[...], cbb_ref[...])

    dt = _softplus(dt_ref[...] + dtb_ref[...])
    adt = dt * (-jnp.exp(alog_ref[...]))
    rows = lax.broadcasted_iota(jnp.int32, (lc, lc), 0)
    cols = lax.broadcasted_iota(jnp.int32, (lc, lc), 1)
    lower = rows >= cols
    a_cum = _dot_exact_rhs_lhs(lower.astype(BF16), adt)
    a_cum_t = a_cum.T
    a_last = a_cum[lc - 1:lc, :]
    stacked = jnp.concatenate([dt, jnp.exp(a_last - a_cum), jnp.exp(a_cum)], axis=0)
    expanded = _dot(stacked.astype(BF16), e_ref[...])
    dt_e = expanded[0:lc]
    dec_e = expanded[lc:2 * lc]
    ea_e = expanded[2 * lc:3 * lc]
    al_e = _dot_exact_rhs(jnp.broadcast_to(jnp.exp(a_last), (8, LANES)), e_ref[...])[0:1]

    xdt = xs * dt_e
    xdt_b = xdt.astype(BF16)
    xd_b = (xdt * dec_e).astype(BF16)
    lane = lax.broadcasted_iota(jnp.int32, (1, LANES), 1)
    pieces = []
    for g in range(SSD_GROUPS):
        bg = bc[:, g * SSD_STATE:(g + 1) * SSD_STATE]
        cg = bc[:, (SSD_GROUPS + g) * SSD_STATE:(SSD_GROUPS + g + 1) * SSD_STATE]
        cg_b = cg.astype(BF16)
        cb = _nt(cg_b, bg.astype(BF16))
        st = state_scr[:, g * hw:(g + 1) * hw]
        y_off = _dot(cg_b, st.astype(BF16)) * ea_e[:, g * hw:(g + 1) * hw]
        new_st = _dot(bg.T.astype(BF16), xd_b[:, g * hw:(g + 1) * hw])
        state_scr[:, g * hw:(g + 1) * hw] = al_e[:, g * hw:(g + 1) * hw] * st + new_st
        for pair in range(SSD_HEADS // SSD_GROUPS // 2):
            acc = None
            c0 = g * hw + pair * LANES
            xp = xdt_b[:, c0:c0 + LANES]
            for half in range(2):
                h = g * (SSD_HEADS // SSD_GROUPS) + pair * 2 + half
                diff = a_cum[:, h:h + 1] - a_cum_t[h:h + 1, :]
                decay = jnp.exp(jnp.where(lower, diff, NEG))
                m = (cb * decay).astype(BF16)
                in_half = (lane >= half * HEAD_DIM) & (lane < (half + 1) * HEAD_DIM)
                part = _dot(m, jnp.where(in_half, xp, jnp.zeros_like(xp)))
                acc = part if acc is None else acc + part
            pieces.append(acc + y_off[:, pair * LANES:(pair + 1) * LANES])
    y = jnp.concatenate(pieces, axis=1) + dexp_ref[...] * xs
    y = y * _silu(z_ref[...].astype(F32))
    ms = jnp.mean(y * y, axis=-1, keepdims=True)
    o_ref[...] = (y * lax.rsqrt(ms + RMS_EPS) * ng_ref[...]).astype(BF16)


def _dot_exact_rhs_lhs(m_bf16, x):
    hi, mid, lo = _split3(x)
    return _dot(m_bf16, hi) + _dot(m_bf16, mid) + _dot(m_bf16, lo)


def _ssd(proj, small, conv_w, conv_b, dt_bias, a_log, d_skip, norm_g, bsz, seq):
    lc = SSD_CHUNK
    nc = seq // lc
    t = bsz * seq
    pad = LANES - SSD_HEADS
    dtb = jnp.pad(dt_bias, (0, pad)).reshape(1, LANES)
    alog = jnp.pad(a_log, (0, pad)).reshape(1, LANES)
    dexp = jnp.repeat(d_skip, HEAD_DIM).reshape(1, SSD_WIDTH)
    expand = (jnp.arange(LANES)[:, None] == (jnp.arange(SSD_WIDTH) // HEAD_DIM)[None, :]).astype(BF16)
    nbc = 2 * SSD_GROUPS * SSD_STATE
    row = lambda b, c: b * nc + c
    const = lambda b, c: (0, 0)
    return pl.pallas_call(
        _ssd_kernel,
        grid=(bsz, nc),
        in_specs=[
            pl.BlockSpec((lc, SSD_WIDTH), lambda b, c: (row(b, c), EV_ZA // SSD_WIDTH)),
            pl.BlockSpec((lc, SSD_WIDTH), lambda b, c: (row(b, c), EV_XS // SSD_WIDTH)),
            pl.BlockSpec((lc, nbc), lambda b, c: (row(b, c), EV_BC // nbc)),
            pl.BlockSpec((lc, LANES), lambda b, c: (row(b, c), 0)),
            pl.BlockSpec((SSD_CONV, SSD_WIDTH), const),
            pl.BlockSpec((SSD_CONV, nbc), const),
            pl.BlockSpec((1, SSD_WIDTH), const),
            pl.BlockSpec((1, nbc), const),
            pl.BlockSpec((1, LANES), const),
            pl.BlockSpec((1, LANES), const),
            pl.BlockSpec((1, SSD_WIDTH), const),
            pl.BlockSpec((1, SSD_WIDTH), const),
            pl.BlockSpec((LANES, SSD_WIDTH), const),
        ],
        out_specs=pl.BlockSpec((lc, SSD_WIDTH), lambda b, c: (row(b, c), 0)),
        out_shape=jax.ShapeDtypeStruct((t, SSD_WIDTH), BF16),
        scratch_shapes=[
            pltpu.VMEM((8, SSD_WIDTH), F32),
            pltpu.VMEM((8, nbc), F32),
            pltpu.VMEM((lc + 8, SSD_WIDTH), F32),
            pltpu.VMEM((lc + 8, nbc), F32),
            pltpu.VMEM((SSD_STATE, SSD_WIDTH), F32),
        ],
        compiler_params=_params(("parallel", "arbitrary")),
        name="ssd",
    )(proj, proj, proj, small,
      conv_w[:, :SSD_WIDTH], conv_w[:, SSD_WIDTH:], conv_b[:SSD_WIDTH].reshape(1, -1),
      conv_b[SSD_WIDTH:].reshape(1, -1), dtb, alog, dexp, norm_g.reshape(1, -1), expand)


def _fold_rows(x, op):
    out = x[0:8, :]
    for i in range(1, x.shape[0] // 8):
        out = op(out, x[8 * i:8 * (i + 1), :])
    return out


def _attend_head(half, q_ref, k_ref, vt_scr, s_scr, ot_scr, bias_fn):
    blk = ATT_BLOCK
    nb = vt_scr.shape[0]
    lo = pl.multiple_of(half * HEAD_DIM, HEAD_DIM)
    in_half = _head_lanes(half)
    causal = (lax.broadcasted_iota(jnp.int32, (blk, blk), 0)
              <= lax.broadcasted_iota(jnp.int32, (blk, blk), 1))
    for qb in range(nb):
        slot = qb % 2
        q = q_ref[qb * blk:(qb + 1) * blk, :]
        qi = jnp.where(in_half, q, jnp.zeros_like(q))
        m8 = None
        for n in range(qb + 1):
            s = _nt(k_ref[n * blk:(n + 1) * blk, :], qi)
            bias = bias_fn(qb, n)
            if bias is not None:
                s = s + bias
            if n == qb:
                s = jnp.where(causal, s, NEG)
            s_scr[slot, n] = s
            part = _fold_rows(s, jnp.maximum)
            m8 = part if m8 is None else jnp.maximum(m8, part)
        m = jnp.max(m8, axis=0, keepdims=True)
        acc = None
        for n in range(qb + 1):
            p = jnp.exp2(s_scr[slot, n] - m)
            pv = _dot(vt_scr[n, half], p.astype(BF16))
            acc = pv if acc is None else acc + pv
        ot_scr[pl.ds(lo, HEAD_DIM), qb * blk:(qb + 1) * blk] = acc[0:HEAD_DIM] / acc[HEAD_DIM:HEAD_DIM + 1]


def _store_vt(v_ref, vt_scr):
    ones = jnp.ones((VT_ROWS - HEAD_DIM, ATT_BLOCK), BF16)
    for n in range(vt_scr.shape[0]):
        vt = v_ref[n * ATT_BLOCK:(n + 1) * ATT_BLOCK, :].astype(F32).T.astype(BF16)
        for half in range(2):
            vt_scr[n, half, 0:HEAD_DIM, :] = vt[half * HEAD_DIM:(half + 1) * HEAD_DIM, :]
            vt_scr[n, half, HEAD_DIM:VT_ROWS, :] = ones


def _gated_output(ot_scr, z_ref, o_ref):
    for i in range(ot_scr.shape[1] // ATT_BLOCK):
        rows = slice(i * ATT_BLOCK, (i + 1) * ATT_BLOCK)
        o = ot_scr[:, rows].T
        o_ref[rows, :] = (o * _silu(z_ref[rows, :].astype(F32))).astype(BF16)


def _head_lanes(half):
    lane = lax.broadcasted_iota(jnp.int32, (1, LANES), 1)
    return (lane >= half * HEAD_DIM) & (lane < (half + 1) * HEAD_DIM)


def _moba_kernel(q_ref, k_ref, v_ref, z_ref, o_ref, vt_scr, kbar_scr, bias_scr, s_scr, ot_scr):
    nb = vt_scr.shape[0]
    seq = q_ref.shape[0]
    _store_vt(v_ref, vt_scr)
    for n in range(nb):
        kb = k_ref[n * ATT_BLOCK:(n + 1) * ATT_BLOCK, :].astype(F32)
        kbar_scr[n:n + 1, :] = jnp.mean(kb, axis=0, keepdims=True)
    n_idx = lax.broadcasted_iota(jnp.int32, (nb, seq), 0)
    q_blk = lax.broadcasted_iota(jnp.int32, (nb, seq), 1) >> ATT_BLOCK_SHIFT

    def head(half, carry):
        in_half = _head_lanes(half)
        q = q_ref[...]
        qi = jnp.where(in_half, q, jnp.zeros_like(q))
        kbar = jnp.where(in_half, kbar_scr[...], 0.0)
        kb_hi = kbar.astype(BF16)
        kb_lo = (kbar - kb_hi.astype(F32)).astype(BF16)
        gate = _nt(kb_hi, qi) + _nt(kb_lo, qi)
        rank = jnp.zeros((nb, seq), F32)
        for mth in range(nb):
            gm = gate[mth:mth + 1, :]
            beats = (gm > gate) | ((gm == gate) & (mth < n_idx))
            rank = rank + jnp.where(beats & (mth < q_blk), 1.0, 0.0)
        chosen = (rank < float(MOBA_TOPK)) & (n_idx < q_blk)
        bias_scr[half] = jnp.where(chosen, 0.0, NEG)

        def bias_fn(qb, n):
            if n == qb or qb <= MOBA_TOPK:
                return None
            return bias_scr[half, n:n + 1, qb * ATT_BLOCK:(qb + 1) * ATT_BLOCK]

        _attend_head(half, q_ref, k_ref, vt_scr, s_scr, ot_scr, bias_fn)
        return carry

    lax.fori_loop(0, 2, head, 0)
    _gated_output(ot_scr, z_ref, o_ref)


def _moba(proj, bsz, seq):
    blk = ATT_BLOCK
    nb = seq // blk
    pairs = MOBA_WIDTH // LANES
    t = bsz * seq
    return pl.pallas_call(
        _moba_kernel,
        grid=(bsz, pairs),
        in_specs=[
            pl.BlockSpec((seq, LANES), lambda b, p: (b, EV_Q // LANES + p)),
            pl.BlockSpec((seq, LANES), lambda b, p: (b, EV_K // LANES + p)),
            pl.BlockSpec((seq, LANES), lambda b, p: (b, EV_V // LANES + p)),
            pl.BlockSpec((seq, LANES), lambda b, p: (b, EV_ZB // LANES + p)),
        ],
        out_specs=pl.BlockSpec((seq, LANES), lambda b, p: (b, p)),
        out_shape=jax.ShapeDtypeStruct((t, MOBA_WIDTH), BF16),
        scratch_shapes=[
            pltpu.VMEM((nb, 2, VT_ROWS, blk), BF16),
            pltpu.VMEM((nb, LANES), F32),
            pltpu.VMEM((2, nb, seq), F32),
            pltpu.VMEM((2, nb, blk, blk), F32),
            pltpu.VMEM((LANES, seq), F32),
        ],
        compiler_params=_params(("parallel", "parallel")),
        name="moba",
    )(proj, proj, proj, proj)


def _fgate_kernel(f_ref, fb_ref, o_ref, carry_scr):
    @pl.when(pl.program_id(1) == 0)
    def _():
        carry_scr[...] = jnp.zeros_like(carry_scr)

    blk = f_ref.shape[0]
    nlf = _softplus(-(f_ref[...] + fb_ref[...]))
    rows = lax.broadcasted_iota(jnp.int32, (blk, blk), 0)
    cols = lax.broadcasted_iota(jnp.int32, (blk, blk), 1)
    csum = _dot_exact_rhs_lhs((rows >= cols).astype(BF16), nlf) + carry_scr[0:1, :]
    o_ref[...] = csum
    carry_scr[...] = jnp.broadcast_to(csum[blk - 1:blk, :], carry_scr.shape)


def _fgate(small, fgate_b, bsz, seq):
    blk = ATT_BLOCK
    nb = seq // blk
    fb = jnp.pad(fgate_b, (0, LANES - FOX_HEADS)).reshape(1, LANES)
    return pl.pallas_call(
        _fgate_kernel,
        grid=(bsz, nb),
        in_specs=[
            pl.BlockSpec((blk, LANES), lambda b, i: (b * nb + i, 0)),
            pl.BlockSpec((1, LANES), lambda b, i: (0, 0)),
        ],
        out_specs=pl.BlockSpec((blk, LANES), lambda b, i: (b * nb + i, 0)),
        out_shape=jax.ShapeDtypeStruct((bsz * seq, LANES), F32),
        scratch_shapes=[pltpu.VMEM((8, LANES), F32)],
        compiler_params=_params(("parallel", "arbitrary")),
        name="fox_gate",
    )(small, fb)


def _fox_kernel(q_ref, k_ref, v_ref, z_ref, nf_ref, o_ref, vt_scr, fb_scr, s_scr, ot_scr):
    pair = pl.program_id(1)
    nb = vt_scr.shape[0]
    blk = ATT_BLOCK
    _store_vt(v_ref, vt_scr)
    sel_row = lax.broadcasted_iota(jnp.int32, (LANES, LANES), 0)

    def head(half, carry):
        onehot = (sel_row == pair * 2 + half).astype(BF16)
        for n in range(nb):
            fb_scr[n] = _dot_exact_rhs(nf_ref[n * blk:(n + 1) * blk, :] * LOG2E, onehot)

        def bias_fn(qb, n):
            fb = fb_scr[n]
            return jnp.concatenate([fb, fb], axis=1)

        _attend_head(half, q_ref, k_ref, vt_scr, s_scr, ot_scr, bias_fn)
        return carry

    lax.fori_loop(0, 2, head, 0)
    _gated_output(ot_scr, z_ref, o_ref)


def _fox(proj, negf, bsz, seq):
    blk = ATT_BLOCK
    nb = seq // blk
    pairs = FOX_WIDTH // LANES
    t = bsz * seq
    return pl.pallas_call(
        _fox_kernel,
        grid=(bsz, pairs),
        in_specs=[
            pl.BlockSpec((seq, LANES), lambda b, p: (b, OD_Q // LANES + p)),
            pl.BlockSpec((seq, LANES), lambda b, p: (b, OD_K // LANES + p)),
            pl.BlockSpec((seq, LANES), lambda b, p: (b, OD_V // LANES + p)),
            pl.BlockSpec((seq, LANES), lambda b, p: (b, OD_ZC // LANES + p)),
            pl.BlockSpec((seq, LANES), lambda b, p: (b, 0)),
        ],
        out_specs=pl.BlockSpec((seq, LANES), lambda b, p: (b, p)),
        out_shape=jax.ShapeDtypeStruct((t, FOX_WIDTH), BF16),
        scratch_shapes=[
            pltpu.VMEM((nb, 2, VT_ROWS, blk), BF16),
            pltpu.VMEM((nb, blk, LANES), F32),
            pltpu.VMEM((2, nb, blk, blk), F32),
            pltpu.VMEM((LANES, seq), F32),
        ],
        compiler_params=_params(("parallel", "parallel")),
        name="fox",
    )(proj, proj, proj, proj, negf)


def _s5scan_kernel(u_ref, lr_ref, li_ref, ldt_ref, bwr_ref, bwi_ref, cwr_ref, cwi_ref, d_ref, y_ref,
                   bre_scr, bim_scr, cre_scr, cim_scr, ar_scr, ai_scr, xr_scr, xi_scr, zr_scr, zi_scr, *, bsz):
    nset, cw, sw = bre_scr.shape

    @pl.when(pl.program_id(0) == 0)
    def _():
        chan_grp = lax.broadcasted_iota(jnp.int32, (cw, sw), 0) >> S5_GROUP_SHIFT
        state_grp = lax.broadcasted_iota(jnp.int32, (cw, sw), 1) >> S5_STATE_SHIFT
        same_b = chan_grp == state_grp
        same_c = ((lax.broadcasted_iota(jnp.int32, (sw, cw), 0) >> S5_STATE_SHIFT)
                  == (lax.broadcasted_iota(jnp.int32, (sw, cw), 1) >> S5_GROUP_SHIFT))
        for s in range(nset):
            lr = lr_ref[s]
            li = li_ref[s]
            dt = jnp.exp(ldt_ref[s])
            mag = jnp.exp(lr * dt)
            ar = mag * jnp.cos(li * dt)
            ai = mag * jnp.sin(li * dt)
            den = lr * lr + li * li
            qr = ((ar - 1.0) * lr + ai * li) / den
            qi = (ai * lr - (ar - 1.0) * li) / den
            ar_scr[s] = jnp.broadcast_to(ar, (bsz, sw))
            ai_scr[s] = jnp.broadcast_to(ai, (bsz, sw))
            bwr = bwr_ref[s]
            bwi = bwi_ref[s]
            bre_scr[s] = jnp.where(same_b, qr * bwr - qi * bwi, 0.0).astype(BF16)
            bim_scr[s] = jnp.where(same_b, qr * bwi + qi * bwr, 0.0).astype(BF16)
            cre_scr[s] = jnp.where(same_c, cwr_ref[s], 0.0).astype(BF16)
            cim_scr[s] = jnp.where(same_c, cwi_ref[s], 0.0).astype(BF16)
        xr_scr[...] = jnp.zeros_like(xr_scr)
        xi_scr[...] = jnp.zeros_like(xi_scr)

    for s in range(nset):
        us = u_ref[:, s * cw:(s + 1) * cw]
        zr_scr[s] = _dot(us, bre_scr[s])
        zi_scr[s] = _dot(us, bim_scr[s])

    def step(t, carry):
        r0 = pl.multiple_of(t * bsz, bsz)
        out = []
        for s in range(nset):
            xr, xi = carry[2 * s], carry[2 * s + 1]
            ar = ar_scr[s]
            ai = ai_scr[s]
            nr = ar * xr - ai * xi + zr_scr[s, pl.ds(r0, bsz), :]
            ni = ar * xi + ai * xr + zi_scr[s, pl.ds(r0, bsz), :]
            zr_scr[s, pl.ds(r0, bsz), :] = nr
            zi_scr[s, pl.ds(r0, bsz), :] = ni
            out += [nr, ni]
        return tuple(out)

    init = []
    for s in range(nset):
        init += [xr_scr[s], xi_scr[s]]
    final = lax.fori_loop(0, u_ref.shape[0] // bsz, step, tuple(init), unroll=2)
    for s in range(nset):
        xr_scr[s] = final[2 * s]
        xi_scr[s] = final[2 * s + 1]

    pieces = [_dot(zr_scr[s].astype(BF16), cre_scr[s]) - _dot(zi_scr[s].astype(BF16), cim_scr[s])
              for s in range(nset)]
    y = jnp.concatenate(pieces, axis=1) + d_ref[...] * u_ref[...].astype(F32)
    y_ref[...] = y.astype(BF16)


def _s5scan(u_tb, lam_re, lam_im, log_dt, b_re, b_im, c_re, c_im, d_skip, bsz):
    rows, width = u_tb.shape
    nset = S5_SETS
    cw = width // nset
    gs = S5_GROUPS // nset
    sw = gs * S5_STATE
    blk = S5_STEPS * bsz
    vec = lambda a: a.reshape(nset, 1, sw)
    bw = lambda b: jnp.tile(jnp.swapaxes(b, 1, 2).reshape(nset, cw, S5_STATE), (1, 1, gs))
    cw_t = lambda c: jnp.tile(
        jnp.swapaxes(c, 1, 2).reshape(nset, gs, S5_STATE, S5_GROUP).transpose(0, 2, 1, 3).reshape(nset, S5_STATE, cw),
        (1, gs, 1))
    full3 = lambda a, b, c: pl.BlockSpec((a, b, c), lambda i: (0, 0, 0))
    return pl.pallas_call(
        functools.partial(_s5scan_kernel, bsz=bsz),
        grid=(rows // blk,),
        in_specs=[
            pl.BlockSpec((blk, width), lambda i: (i, 0)),
            full3(nset, 1, sw), full3(nset, 1, sw), full3(nset, 1, sw),
            full3(nset, cw, sw), full3(nset, cw, sw),
            full3(nset, sw, cw), full3(nset, sw, cw),
            pl.BlockSpec((1, width), lambda i: (0, 0)),
        ],
        out_specs=pl.BlockSpec((blk, width), lambda i: (i, 0)),
        out_shape=jax.ShapeDtypeStruct((rows, width), BF16),
        scratch_shapes=[
            pltpu.VMEM((nset, cw, sw), BF16), pltpu.VMEM((nset, cw, sw), BF16),
            pltpu.VMEM((nset, sw, cw), BF16), pltpu.VMEM((nset, sw, cw), BF16),
            pltpu.VMEM((nset, bsz, sw), F32), pltpu.VMEM((nset, bsz, sw), F32),
            pltpu.VMEM((nset, bsz, sw), F32), pltpu.VMEM((nset, bsz, sw), F32),
            pltpu.VMEM((nset, blk, sw), F32), pltpu.VMEM((nset, blk, sw), F32),
        ],
        compiler_params=_params(("arbitrary",)),
        name="s5_scan",
    )(u_tb, vec(lam_re), vec(lam_im), vec(jnp.repeat(log_dt, S5_STATE)),
      bw(b_re), bw(b_im), cw_t(c_re), cw_t(c_im), d_skip.reshape(1, width))


def _s5post_kernel(y_ref, z_ref, gw_ref, gb_ref, o_ref):
    y = y_ref[...].astype(F32)
    y = 0.5 * y * (1.0 + jnp.tanh(math.sqrt(2.0 / math.pi) * (y + 0.044715 * (y * y * y))))
    y = y * jax.nn.sigmoid(_dot(y.astype(BF16), gw_ref[...]) + gb_ref[...])
    o_ref[...] = (y * _silu(z_ref[...].astype(F32))).astype(BF16)


def _s5post(y_pre, proj, glu_w, glu_b, tm=512):
    t, w = y_pre.shape
    return pl.pallas_call(
        _s5post_kernel,
        grid=(t // tm,),
        in_specs=[
            pl.BlockSpec((tm, w), lambda i: (i, 0)),
            pl.BlockSpec((tm, w), lambda i: (i, OD_ZD // S5_WIDTH)),
            pl.BlockSpec((w, w), lambda i: (0, 0)),
            pl.BlockSpec((1, w), lambda i: (0, 0)),
        ],
        out_specs=pl.BlockSpec((tm, w), lambda i: (i, 0)),
        out_shape=jax.ShapeDtypeStruct((t, w), BF16),
        compiler_params=_params(("parallel",)),
        name="s5_post",
    )(y_pre, proj, glu_w.astype(BF16), glu_b.reshape(1, w))


def _small_weights(cols):
    padded = jnp.pad(cols, ((0, 0), (0, LANES - cols.shape[1])))
    hi = padded.astype(BF16)
    lo = (padded - hi.astype(F32)).astype(BF16)
    return jnp.concatenate([hi, lo], axis=1)


def _even_layer(x2, mod, pre_g, post_g, in_w, conv_w, conv_b, dt_bias, a_log, d_skip, norm_g, out_w, bsz, seq):
    d = D_MODEL
    shift, scale, gate = (mod[:, i * d:(i + 1) * d].reshape(bsz, 1, d) for i in range(3))
    o_xbc = 2 * SSD_WIDTH
    o_dt = o_xbc + SSD_WIDTH + 2 * SSD_GROUPS * SSD_STATE
    o_q = o_dt + SSD_HEADS
    w = jnp.concatenate([
        in_w[:, :o_xbc],
        in_w[:, o_xbc:o_xbc + SSD_WIDTH],
        in_w[:, o_q:o_q + MOBA_WIDTH] * ATT_SCALE,
        in_w[:, o_q + MOBA_WIDTH:],
        in_w[:, o_xbc + SSD_WIDTH:o_dt],
    ], axis=1).astype(BF16)
    ws = _small_weights(in_w[:, o_dt:o_q])
    proj, small = _inproj(x2, scale, shift, pre_g.reshape(1, d), w, ws, seq)
    y_a = _ssd(proj, small, conv_w, conv_b, dt_bias, a_log, d_skip, norm_g, bsz, seq)
    y_b = _moba(proj, bsz, seq)
    ow = out_w.astype(BF16)
    return _outproj(y_a, y_b, ow[:SSD_WIDTH], ow[SSD_WIDTH:], x2, gate, post_g.reshape(1, d), seq)


def _odd_layer(x2, mod, pre_g, post_g, in_w, fgate_b, lam_re, lam_im, log_dt, b_re, b_im, c_re, c_im,
               d_skip, glu_w, glu_b, out_w, bsz, seq):
    d = D_MODEL
    shift, scale, gate = (mod[:, i * d:(i + 1) * d].reshape(bsz, 1, d) for i in range(3))
    o_f = D_MIX + 3 * FOX_WIDTH
    o_u = o_f + FOX_HEADS
    w = jnp.concatenate([
        in_w[:, :D_MIX],
        in_w[:, D_MIX:D_MIX + FOX_WIDTH] * ATT_SCALE,
        in_w[:, D_MIX + FOX_WIDTH:o_f],
        in_w[:, o_u:],
    ], axis=1).astype(BF16)
    ws = _small_weights(in_w[:, o_f:o_u])
    proj, small = _inproj(x2, scale, shift, pre_g.reshape(1, d), w, ws, seq)

    negf = _fgate(small, fgate_b, bsz, seq)
    y_c = _fox(proj, negf, bsz, seq)

    u_tb = proj[:, OD_U:OD_U + S5_WIDTH].reshape(bsz, seq, S5_WIDTH).transpose(1, 0, 2).reshape(seq * bsz, S5_WIDTH)
    y_tb = _s5scan(u_tb, lam_re, lam_im, log_dt, b_re, b_im, c_re, c_im, d_skip, bsz)
    y_pre = y_tb.reshape(seq, bsz, S5_WIDTH).transpose(1, 0, 2).reshape(bsz * seq, S5_WIDTH)
    y_d = _s5post(y_pre, proj, glu_w, glu_b)

    ow = out_w.astype(BF16)
    return _outproj(y_c, y_d, ow[:FOX_WIDTH], ow[FOX_WIDTH:], x2, gate, post_g.reshape(1, d), seq)


def kernel(x, c, ada_w, ada_b, pre_g, post_g, even_in_w, even_conv_w, even_conv_b, even_dt_bias, even_a_log,
           even_d_skip, even_norm_g, even_out_w, odd_in_w, odd_fgate_b, odd_lam_re, odd_lam_im, odd_log_dt,
           odd_b_re, odd_b_im, odd_c_re, odd_c_im, odd_d_skip, odd_glu_w, odd_glu_b, odd_out_w):
    bsz, seq, d = x.shape
    depth = ada_w.shape[0]
    mod = _ada_mod(c, ada_w, ada_b)
    x2 = x.reshape(bsz * seq, d)
    for layer in range(depth):
        i = layer // 2
        if layer % 2 == 0:
            x2 = _even_layer(x2, mod[layer], pre_g[layer], post_g[layer], even_in_w[i], even_conv_w[i],
                             even_conv_b[i], even_dt_bias[i], even_a_log[i], even_d_skip[i], even_norm_g[i],
                             even_out_w[i], bsz, seq)
        else:
            x2 = _odd_layer(x2, mod[layer], pre_g[layer], post_g[layer], odd_in_w[i], odd_fgate_b[i],
                            odd_lam_re[i], odd_lam_im[i], odd_log_dt[i], odd_b_re[i], odd_b_im[i], odd_c_re[i],
                            odd_c_im[i], odd_d_skip[i], odd_glu_w[i], odd_glu_b[i], odd_out_w[i], bsz, seq)
    return x2.reshape(bsz, seq, d)
```

```python
import functools
import math

import jax
import jax.numpy as jnp
from jax import lax
from jax.experimental import pallas as pl
from jax.experimental.pallas import tpu as pltpu

F32 = jnp.float32
BF16 = jnp.bfloat16

D_MODEL = 1024
HEAD_DIM = 64
D_MIX = 2 * D_MODEL
SSD_WIDTH = 1024
SSD_HEADS = 16
SSD_GROUPS = 2
SSD_STATE = 128
SSD_CONV = 4
SSD_CHUNK = 128
MOBA_WIDTH = 1024
MOBA_BLOCK = 256
MOBA_TOPK = 3
FOX_WIDTH = 1536
FOX_HEADS = 24
S5_WIDTH = 512
S5_GROUP = 16
S5_GROUP_SHIFT = 4
S5_GROUPS = 32
S5_STATE = 64
S5_STATE_SHIFT = 6
S5_SETS = 4
S5_STEPS = 64
RMS_EPS = 1e-6
ATT_BLOCK = 256
ATT_BLOCK_SHIFT = 8
LOG2E = math.log2(math.e)
ATT_SCALE = LOG2E / math.sqrt(HEAD_DIM)
VT_ROWS = HEAD_DIM + 16
LANES = 128
NEG = -1e30
VMEM_LIMIT = 48 * 1024 * 1024

EV_ZA, EV_ZB, EV_XS, EV_Q, EV_K, EV_V, EV_BC, EV_N = 0, 1024, 2048, 3072, 4096, 5120, 6144, 6656
OD_ZC, OD_ZD, OD_Q, OD_K, OD_V, OD_U, OD_N = 0, 1536, 2048, 3584, 5120, 6656, 7168


def _nt(a, b):
    return lax.dot_general(a, b, (((1,), (1,)), ((), ())), preferred_element_type=F32)


def _dot(a, b):
    return jnp.dot(a, b, preferred_element_type=F32)


def _split3(x):
    hi = x.astype(BF16)
    r = x - hi.astype(F32)
    mid = r.astype(BF16)
    lo = (r - mid.astype(F32)).astype(BF16)
    return hi, mid, lo


def _dot_exact_rhs(x, m_bf16):
    hi, mid, lo = _split3(x)
    return _dot(hi, m_bf16) + _dot(mid, m_bf16) + _dot(lo, m_bf16)


def _silu(x):
    return x * jax.nn.sigmoid(x)


def _softplus(x):
    return jnp.maximum(x, 0.0) + jnp.log1p(jnp.exp(-jnp.abs(x)))


def _params(sem, limit=VMEM_LIMIT):
    return pltpu.CompilerParams(dimension_semantics=sem, vmem_limit_bytes=limit)


def _ada_kernel(c_ref, w_ref, b_ref, o_ref):
    cond = _silu(c_ref[...])
    hi, mid, lo = _split3(cond)
    w = w_ref[0]
    whi, wmid, wlo = _split3(w)
    acc = _dot(hi, whi) + _dot(hi, wmid) + _dot(mid, whi)
    acc = acc + _dot(hi, wlo) + _dot(mid, wmid) + _dot(lo, whi)
    o_ref[0] = acc + b_ref[0]


def _ada_mod(c, ada_w, ada_b):
    depth, d, d3 = ada_w.shape
    bsz = c.shape[0]
    nj = d3 // d
    return pl.pallas_call(
        _ada_kernel,
        grid=(depth, nj),
        in_specs=[
            pl.BlockSpec((bsz, d), lambda l, j: (0, 0)),
            pl.BlockSpec((1, d, d), lambda l, j: (l, 0, j)),
            pl.BlockSpec((1, 1, d), lambda l, j: (l, 0, j)),
        ],
        out_specs=pl.BlockSpec((1, bsz, d), lambda l, j: (l, 0, j)),
        out_shape=jax.ShapeDtypeStruct((depth, bsz, d3), F32),
        compiler_params=_params(("parallel", "parallel")),
        name="ada_mod",
    )(c, ada_w, ada_b.reshape(depth, 1, d3))


def _inproj_kernel(x_ref, sc_ref, sh_ref, g_ref, w_ref, ws_ref, o_ref, os_ref, h_scr):
    @pl.when(pl.program_id(1) == 0)
    def _():
        x = x_ref[...]
        ms = jnp.mean(x * x, axis=-1, keepdims=True)
        xn = x * lax.rsqrt(ms + RMS_EPS) * g_ref[...]
        h = xn * (1.0 + sc_ref[0]) + sh_ref[0]
        h_b = h.astype(BF16)
        h_scr[...] = h_b
        r = _dot(h_b, ws_ref[...])
        os_ref[...] = r[:, :LANES] + r[:, LANES:]

    o_ref[...] = _dot(h_scr[...], w_ref[...]).astype(BF16)


def _inproj(x2, scale, shift, g, w, ws, seq, tm=1024):
    t, d = x2.shape
    n = w.shape[1]
    tn = n // 4
    per = seq // tm
    return pl.pallas_call(
        _inproj_kernel,
        grid=(t // tm, n // tn),
        in_specs=[
            pl.BlockSpec((tm, d), lambda i, j: (i, 0)),
            pl.BlockSpec((1, 1, d), lambda i, j: (i // per, 0, 0)),
            pl.BlockSpec((1, 1, d), lambda i, j: (i // per, 0, 0)),
            pl.BlockSpec((1, d), lambda i, j: (0, 0)),
            pl.BlockSpec((d, tn), lambda i, j: (0, j)),
            pl.BlockSpec((d, 2 * LANES), lambda i, j: (0, 0)),
        ],
        out_specs=[
            pl.BlockSpec((tm, tn), lambda i, j: (i, j)),
            pl.BlockSpec((tm, LANES), lambda i, j: (i, 0)),
        ],
        out_shape=[
            jax.ShapeDtypeStruct((t, n), BF16),
            jax.ShapeDtypeStruct((t, LANES), F32),
        ],
        scratch_shapes=[pltpu.VMEM((tm, d), BF16)],
        compiler_params=_params(("parallel", "arbitrary")),
        name="in_proj",
    )(x2, scale, shift, g, w, ws)


def _outproj_kernel(a_ref, b_ref, wa_ref, wb_ref, x_ref, gate_ref, pg_ref, o_ref):
    y = _dot(a_ref[...], wa_ref[...]) + _dot(b_ref[...], wb_ref[...])
    ms = jnp.mean(y * y, axis=-1, keepdims=True)
    yn = y * lax.rsqrt(ms + RMS_EPS) * pg_ref[...]
    o_ref[...] = x_ref[...] + gate_ref[0] * yn


def _outproj(a, b, wa, wb, x2, gate, pg, seq, tm=512):
    t, d = x2.shape
    ka, kb = a.shape[1], b.shape[1]
    per = seq // tm
    return pl.pallas_call(
        _outproj_kernel,
        grid=(t // tm,),
        in_specs=[
            pl.BlockSpec((tm, ka), lambda i: (i, 0)),
            pl.BlockSpec((tm, kb), lambda i: (i, 0)),
            pl.BlockSpec((ka, d), lambda i: (0, 0)),
            pl.BlockSpec((kb, d), lambda i: (0, 0)),
            pl.BlockSpec((tm, d), lambda i: (i, 0)),
            pl.BlockSpec((1, 1, d), lambda i: (i // per, 0, 0)),
            pl.BlockSpec((1, d), lambda i: (0, 0)),
        ],
        out_specs=pl.BlockSpec((tm, d), lambda i: (i, 0)),
        out_shape=jax.ShapeDtypeStruct((t, d), F32),
        compiler_params=_params(("parallel",)),
        name="out_proj",
    )(a, b, wa, wb, x2, gate, pg)


def _causal_conv_silu(in_ref, ext_scr, tail_scr, w, b):
    lc = in_ref.shape[0]
    raw = in_ref[...].astype(F32)
    ext_scr[0:8, :] = tail_scr[...]
    ext_scr[8:8 + lc, :] = raw
    tail_scr[...] = raw[lc - 8:lc, :]
    acc = b
    for j in range(SSD_CONV):
        off = 8 - (SSD_CONV - 1) + j
        acc = acc + w[j:j + 1, :] * ext_scr[off:off + lc, :]
    return _silu(acc)


def _ssd_kernel(z_ref, xs_ref, bc_ref, dt_ref, cwx_ref, cwb_ref, cbx_ref, cbb_ref, dtb_ref, alog_ref,
                dexp_ref, ng_ref, e_ref, o_ref, tailx_scr, tailb_scr, extx_scr, extb_scr, state_scr):
    lc = SSD_CHUNK
    hw = SSD_HEADS // SSD_GROUPS * HEAD_DIM

    @pl.when(pl.program_id(1) == 0)
    def _():
        tailx_scr[...] = jnp.zeros_like(tailx_scr)
        tailb_scr[...] = jnp.zeros_like(tailb_scr)
        state_scr[...] = jnp.zeros_like(state_scr)

    xs = _causal_conv_silu(xs_ref, extx_scr, tailx_scr, cwx_ref[...], cbx_ref[...])
    bc = _causal_conv_silu(bc_ref, extb_scr, tailb_scr, cwb_ref[...], cbb_ref[...])

    dt = _softplus(dt_ref[...] + dtb_ref[...])
    adt = dt * (-jnp.exp(alog_ref[...]))
    rows = lax.broadcasted_iota(jnp.int32, (lc, lc), 0)
    cols = lax.broadcasted_iota(jnp.int32, (lc, lc), 1)
    lower = rows >= cols
    a_cum = _dot_exact_rhs_lhs(lower.astype(BF16), adt)
    a_cum_t = a_cum.T
    a_last = a_cum[lc - 1:lc, :]
    stacked = jnp.concatenate([dt, jnp.exp(a_last - a_cum), jnp.exp(a_cum)], axis=0)
    expanded = _dot(stacked.astype(BF16), e_ref[...])
    dt_e = expanded[0:lc]
    dec_e = expanded[lc:2 * lc]
    ea_e = expanded[2 * lc:3 * lc]
    al_e = _dot_exact_rhs(jnp.broadcast_to(jnp.exp(a_last), (8, LANES)), e_ref[...])[0:1]

    xdt = xs * dt_e
    xdt_b = xdt.astype(BF16)
    xd_b = (xdt * dec_e).astype(BF16)
    lane = lax.broadcasted_iota(jnp.int32, (1, LANES), 1)
    pieces = []
    for g in range(SSD_GROUPS):
        bg = bc[:, g * SSD_STATE:(g + 1) * SSD_STATE]
        cg = bc[:, (SSD_GROUPS + g) * SSD_STATE:(SSD_GROUPS + g + 1) * SSD_STATE]
        cg_b = cg.astype(BF16)
        cb = _nt(cg_b, bg.astype(BF16))
        st = state_scr[:, g * hw:(g + 1) * hw]
        y_off = _dot(cg_b, st.astype(BF16)) * ea_e[:, g * hw:(g + 1) * hw]
        new_st = _dot(bg.T.astype(BF16), xd_b[:, g * hw:(g + 1) * hw])
        state_scr[:, g * hw:(g + 1) * hw] = al_e[:, g * hw:(g + 1) * hw] * st + new_st
        for pair in range(SSD_HEADS // SSD_GROUPS // 2):
            acc = None
            c0 = g * hw + pair * LANES
            xp = xdt_b[:, c0:c0 + LANES]
            for half in range(2):
                h = g * (SSD_HEADS // SSD_GROUPS) + pair * 2 + half
                diff = a_cum[:, h:h + 1] - a_cum_t[h:h + 1, :]
                decay = jnp.exp(jnp.where(lower, diff, NEG))
                m = (cb * decay).astype(BF16)
                in_half = (lane >= half * HEAD_DIM) & (lane < (half + 1) * HEAD_DIM)
                part = _dot(m, jnp.where(in_half, xp, jnp.zeros_like(xp)))
                acc = part if acc is None else acc + part
            pieces.append(acc + y_off[:, pair * LANES:(pair + 1) * LANES])
    y = jnp.concatenate(pieces, axis=1) + dexp_ref[...] * xs
    y = y * _silu(z_ref[...].astype(F32))
    ms = jnp.mean(y * y, axis=-1, keepdims=True)
    o_ref[...] = (y * lax.rsqrt(ms + RMS_EPS) * ng_ref[...]).astype(BF16)


def _dot_exact_rhs_lhs(m_bf16, x):
    hi, mid, lo = _split3(x)
    return _dot(m_bf16, hi) + _dot(m_bf16, mid) + _dot(m_bf16, lo)


def _ssd(proj, small, conv_w, conv_b, dt_bias, a_log, d_skip, norm_g, bsz, seq):
    lc = SSD_CHUNK
    nc = seq // lc
    t = bsz * seq
    pad = LANES - SSD_HEADS
    dtb = jnp.pad(dt_bias, (0, pad)).reshape(1, LANES)
    alog = jnp.pad(a_log, (0, pad)).reshape(1, LANES)
    dexp = jnp.repeat(d_skip, HEAD_DIM).reshape(1, SSD_WIDTH)
    expand = (jnp.arange(LANES)[:, None] == (jnp.arange(SSD_WIDTH) // HEAD_DIM)[None, :]).astype(BF16)
    nbc = 2 * SSD_GROUPS * SSD_STATE
    row = lambda b, c: b * nc + c
    const = lambda b, c: (0, 0)
    return pl.pallas_call(
        _ssd_kernel,
        grid=(bsz, nc),
        in_specs=[
            pl.BlockSpec((lc, SSD_WIDTH), lambda b, c: (row(b, c), EV_ZA // SSD_WIDTH)),
            pl.BlockSpec((lc, SSD_WIDTH), lambda b, c: (row(b, c), EV_XS // SSD_WIDTH)),
            pl.BlockSpec((lc, nbc), lambda b, c: (row(b, c), EV_BC // nbc)),
            pl.BlockSpec((lc, LANES), lambda b, c: (row(b, c), 0)),
            pl.BlockSpec((SSD_CONV, SSD_WIDTH), const),
            pl.BlockSpec((SSD_CONV, nbc), const),
            pl.BlockSpec((1, SSD_WIDTH), const),
            pl.BlockSpec((1, nbc), const),
            pl.BlockSpec((1, LANES), const),
            pl.BlockSpec((1, LANES), const),
            pl.BlockSpec((1, SSD_WIDTH), const),
            pl.BlockSpec((1, SSD_WIDTH), const),
            pl.BlockSpec((LANES, SSD_WIDTH), const),
        ],
        out_specs=pl.BlockSpec((lc, SSD_WIDTH), lambda b, c: (row(b, c), 0)),
        out_shape=jax.ShapeDtypeStruct((t, SSD_WIDTH), BF16),
        scratch_shapes=[
            pltpu.VMEM((8, SSD_WIDTH), F32),
            pltpu.VMEM((8, nbc), F32),
            pltpu.VMEM((lc + 8, SSD_WIDTH), F32),
            pltpu.VMEM((lc + 8, nbc), F32),
            pltpu.VMEM((SSD_STATE, SSD_WIDTH), F32),
        ],
        compiler_params=_params(("parallel", "arbitrary")),
        name="ssd",
    )(proj, proj, proj, small,
      conv_w[:, :SSD_WIDTH], conv_w[:, SSD_WIDTH:], conv_b[:SSD_WIDTH].reshape(1, -1),
      conv_b[SSD_WIDTH:].reshape(1, -1), dtb, alog, dexp, norm_g.reshape(1, -1), expand)


def _fold_rows(x, op):
    out = x[0:8, :]
    for i in range(1, x.shape[0] // 8):
        out = op(out, x[8 * i:8 * (i + 1), :])
    return out


def _attend_pair(q_ref, k_ref, vt_scr, s_scr, p_scr, ot_scr, bias_fn):
    blk = ATT_BLOCK
    nb = q_ref.shape[0] // blk
    heads = (0, 1)
    in_half = [_head_lanes(half) for half in heads]
    causal = (lax.broadcasted_iota(jnp.int32, (blk, blk), 0)
              <= lax.broadcasted_iota(jnp.int32, (blk, blk), 1))

    def masked_queries(half, qb):
        q = q_ref[qb * blk:(qb + 1) * blk, :]
        return jnp.where(in_half[half], q, jnp.zeros_like(q))

    def score_tile(half, qb, n, qi, m8):
        s = _nt(k_ref[n * blk:(n + 1) * blk, :], qi)
        bias = bias_fn(half, qb, n)
        if bias is not None:
            s = s + bias
        if n == qb:
            s = jnp.where(causal, s, NEG)
        s_scr[half, qb % 2, n] = s
        part = _fold_rows(s, jnp.maximum)
        return part if m8 is None else jnp.maximum(m8, part)

    def prob_tile(half, qb, n, m):
        p = jnp.exp2(s_scr[half, qb % 2, n] - m)
        p_scr[half, qb % 2, n * blk:(n + 1) * blk, :] = p.astype(BF16)

    m8 = [score_tile(half, 0, 0, masked_queries(half, 0), None) for half in heads]
    for qb in range(nb):
        m = [jnp.max(m8[half], axis=0, keepdims=True) for half in heads]
        nxt = qb + 1
        qi = [masked_queries(half, nxt) if nxt < nb else None for half in heads]
        m8 = [None, None]
        for n in range(nxt + 1):
            for half in heads:
                if nxt < nb:
                    m8[half] = score_tile(half, nxt, n, qi[half], m8[half])
            for half in heads:
                if n <= qb:
                    prob_tile(half, qb, n, m[half])
        keys = (qb + 1) * blk
        for half in heads:
            acc = _dot(vt_scr[half, :, 0:keys], p_scr[half, qb % 2, 0:keys, :])
            out = acc[0:HEAD_DIM] / acc[HEAD_DIM:HEAD_DIM + 1]
            ot_scr[half * HEAD_DIM:(half + 1) * HEAD_DIM, qb * blk:(qb + 1) * blk] = out


def _store_vt(v_ref, vt_scr):
    ones = jnp.ones((VT_ROWS - HEAD_DIM, ATT_BLOCK), BF16)
    for n in range(v_ref.shape[0] // ATT_BLOCK):
        cols = slice(n * ATT_BLOCK, (n + 1) * ATT_BLOCK)
        vt = v_ref[cols, :].astype(F32).T.astype(BF16)
        for half in range(2):
            vt_scr[half, 0:HEAD_DIM, cols] = vt[half * HEAD_DIM:(half + 1) * HEAD_DIM, :]
            vt_scr[half, HEAD_DIM:VT_ROWS, cols] = ones


def _gated_output(ot_scr, z_ref, o_ref):
    for i in range(ot_scr.shape[1] // ATT_BLOCK):
        rows = slice(i * ATT_BLOCK, (i + 1) * ATT_BLOCK)
        o = ot_scr[:, rows].T
        o_ref[rows, :] = (o * _silu(z_ref[rows, :].astype(F32))).astype(BF16)


def _head_lanes(half):
    lane = lax.broadcasted_iota(jnp.int32, (1, LANES), 1)
    return (lane >= half * HEAD_DIM) & (lane < (half + 1) * HEAD_DIM)


def _moba_kernel(q_ref, k_ref, v_ref, z_ref, o_ref, vt_scr, kbar_scr, bias_scr, s_scr, p_scr, ot_scr):
    seq = q_ref.shape[0]
    nb = seq // ATT_BLOCK
    _store_vt(v_ref, vt_scr)
    for n in range(nb):
        kb = k_ref[n * ATT_BLOCK:(n + 1) * ATT_BLOCK, :].astype(F32)
        kbar_scr[n:n + 1, :] = jnp.mean(kb, axis=0, keepdims=True)
    n_idx = lax.broadcasted_iota(jnp.int32, (nb, seq), 0)
    q_blk = lax.broadcasted_iota(jnp.int32, (nb, seq), 1) >> ATT_BLOCK_SHIFT

    for half in range(2):
        in_half = _head_lanes(half)
        q = q_ref[...]
        qi = jnp.where(in_half, q, jnp.zeros_like(q))
        kbar = jnp.where(in_half, kbar_scr[...], 0.0)
        kb_hi = kbar.astype(BF16)
        kb_lo = (kbar - kb_hi.astype(F32)).astype(BF16)
        gate = _nt(kb_hi, qi) + _nt(kb_lo, qi)
        rank = jnp.zeros((nb, seq), F32)
        for mth in range(nb):
            gm = gate[mth:mth + 1, :]
            beats = (gm > gate) | ((gm == gate) & (mth < n_idx))
            rank = rank + jnp.where(beats & (mth < q_blk), 1.0, 0.0)
        chosen = (rank < float(MOBA_TOPK)) & (n_idx < q_blk)
        bias_scr[half] = jnp.where(chosen, 0.0, NEG)

    def bias_fn(half, qb, n):
        if n == qb or qb <= MOBA_TOPK:
            return None
        return bias_scr[half, n:n + 1, qb * ATT_BLOCK:(qb + 1) * ATT_BLOCK]

    _attend_pair(q_ref, k_ref, vt_scr, s_scr, p_scr, ot_scr, bias_fn)
    _gated_output(ot_scr, z_ref, o_ref)


def _moba(proj, bsz, seq):
    blk = ATT_BLOCK
    nb = seq // blk
    pairs = MOBA_WIDTH // LANES
    t = bsz * seq
    return pl.pallas_call(
        _moba_kernel,
        grid=(bsz, pairs),
        in_specs=[
            pl.BlockSpec((seq, LANES), lambda b, p: (b, EV_Q // LANES + p)),
            pl.BlockSpec((seq, LANES), lambda b, p: (b, EV_K // LANES + p)),
            pl.BlockSpec((seq, LANES), lambda b, p: (b, EV_V // LANES + p)),
            pl.BlockSpec((seq, LANES), lambda b, p: (b, EV_ZB // LANES + p)),
        ],
        out_specs=pl.BlockSpec((seq, LANES), lambda b, p: (b, p)),
        out_shape=jax.ShapeDtypeStruct((t, MOBA_WIDTH), BF16),
        scratch_shapes=[
            pltpu.VMEM((2, VT_ROWS, seq), BF16),
            pltpu.VMEM((nb, LANES), F32),
            pltpu.VMEM((2, nb, seq), F32),
            pltpu.VMEM((2, 2, nb, blk, blk), F32),
            pltpu.VMEM((2, 2, seq, blk), BF16),
            pltpu.VMEM((LANES, seq), F32),
        ],
        compiler_params=_params(("parallel", "parallel")),
        name="moba",
    )(proj, proj, proj, proj)


def _fgate_kernel(f_ref, fb_ref, o_ref, carry_scr):
    @pl.when(pl.program_id(1) == 0)
    def _():
        carry_scr[...] = jnp.zeros_like(carry_scr)

    blk = f_ref.shape[0]
    nlf = _softplus(-(f_ref[...] + fb_ref[...]))
    rows = lax.broadcasted_iota(jnp.int32, (blk, blk), 0)
    cols = lax.broadcasted_iota(jnp.int32, (blk, blk), 1)
    csum = _dot_exact_rhs_lhs((rows >= cols).astype(BF16), nlf) + carry_scr[0:1, :]
    o_ref[...] = csum
    carry_scr[...] = jnp.broadcast_to(csum[blk - 1:blk, :], carry_scr.shape)


def _fgate(small, fgate_b, bsz, seq):
    blk = ATT_BLOCK
    nb = seq // blk
    fb = jnp.pad(fgate_b, (0, LANES - FOX_HEADS)).reshape(1, LANES)
    return pl.pallas_call(
        _fgate_kernel,
        grid=(bsz, nb),
        in_specs=[
            pl.BlockSpec((blk, LANES), lambda b, i: (b * nb + i, 0)),
            pl.BlockSpec((1, LANES), lambda b, i: (0, 0)),
        ],
        out_specs=pl.BlockSpec((blk, LANES), lambda b, i: (b * nb + i, 0)),
        out_shape=jax.ShapeDtypeStruct((bsz * seq, LANES), F32),
        scratch_shapes=[pltpu.VMEM((8, LANES), F32)],
        compiler_params=_params(("parallel", "arbitrary")),
        name="fox_gate",
    )(small, fb)


def _fox_kernel(q_ref, k_ref, v_ref, z_ref, nf_ref, o_ref, vt_scr, fb_scr, s_scr, p_scr, ot_scr):
    pair = pl.program_id(1)
    nb = q_ref.shape[0] // ATT_BLOCK
    blk = ATT_BLOCK
    _store_vt(v_ref, vt_scr)
    lane = lax.broadcasted_iota(jnp.int32, (1, LANES), 1)
    for half in range(2):
        own_lane = lane == pair * 2 + half
        for n in range(nb):
            col = jnp.sum(jnp.where(own_lane, nf_ref[n * blk:(n + 1) * blk, :], 0.0), axis=1, keepdims=True)
            fb_scr[half, n] = jnp.broadcast_to(col * LOG2E, (blk, LANES))

    def bias_fn(half, qb, n):
        fb = fb_scr[half, n]
        return jnp.concatenate([fb, fb], axis=1)

    _attend_pair(q_ref, k_ref, vt_scr, s_scr, p_scr, ot_scr, bias_fn)
    _gated_output(ot_scr, z_ref, o_ref)


def _fox(proj, negf, bsz, seq):
    blk = ATT_BLOCK
    nb = seq // blk
    pairs = FOX_WIDTH // LANES
    t = bsz * seq
    return pl.pallas_call(
        _fox_kernel,
        grid=(bsz, pairs),
        in_specs=[
            pl.BlockSpec((seq, LANES), lambda b, p: (b, OD_Q // LANES + p)),
            pl.BlockSpec((seq, LANES), lambda b, p: (b, OD_K // LANES + p)),
            pl.BlockSpec((seq, LANES), lambda b, p: (b, OD_V // LANES + p)),
            pl.BlockSpec((seq, LANES), lambda b, p: (b, OD_ZC // LANES + p)),
            pl.BlockSpec((seq, LANES), lambda b, p: (b, 0)),
        ],
        out_specs=pl.BlockSpec((seq, LANES), lambda b, p: (b, p)),
        out_shape=jax.ShapeDtypeStruct((t, FOX_WIDTH), BF16),
        scratch_shapes=[
            pltpu.VMEM((2, VT_ROWS, seq), BF16),
            pltpu.VMEM((2, nb, blk, LANES), F32),
            pltpu.VMEM((2, 2, nb, blk, blk), F32),
            pltpu.VMEM((2, 2, seq, blk), BF16),
            pltpu.VMEM((LANES, seq), F32),
        ],
        compiler_params=_params(("parallel", "parallel")),
        name="fox",
    )(proj, proj, proj, proj, negf)


def _s5scan_kernel(u_ref, lr_ref, li_ref, ldt_ref, bwr_ref, bwi_ref, cwr_ref, cwi_ref, d_ref, y_ref,
                   bre_scr, bim_scr, cre_scr, cim_scr, ar_scr, ai_scr, xr_scr, xi_scr, zr_scr, zi_scr, *, bsz):
    nset, cw, sw = bre_scr.shape

    @pl.when(pl.program_id(0) == 0)
    def _():
        chan_grp = lax.broadcasted_iota(jnp.int32, (cw, sw), 0) >> S5_GROUP_SHIFT
        state_grp = lax.broadcasted_iota(jnp.int32, (cw, sw), 1) >> S5_STATE_SHIFT
        same_b = chan_grp == state_grp
        same_c = ((lax.broadcasted_iota(jnp.int32, (sw, cw), 0) >> S5_STATE_SHIFT)
                  == (lax.broadcasted_iota(jnp.int32, (sw, cw), 1) >> S5_GROUP_SHIFT))
        for s in range(nset):
            lr = lr_ref[s]
            li = li_ref[s]
            dt = jnp.exp(ldt_ref[s])
            mag = jnp.exp(lr * dt)
            ar = mag * jnp.cos(li * dt)
            ai = mag * jnp.sin(li * dt)
            den = lr * lr + li * li
            qr = ((ar - 1.0) * lr + ai * li) / den
            qi = (ai * lr - (ar - 1.0) * li) / den
            ar_scr[s] = jnp.broadcast_to(ar, (bsz, sw))
            ai_scr[s] = jnp.broadcast_to(ai, (bsz, sw))
            bwr = bwr_ref[s]
            bwi = bwi_ref[s]
            bre_scr[s] = jnp.where(same_b, qr * bwr - qi * bwi, 0.0).astype(BF16)
            bim_scr[s] = jnp.where(same_b, qr * bwi + qi * bwr, 0.0).astype(BF16)
            cre_scr[s] = jnp.where(same_c, cwr_ref[s], 0.0).astype(BF16)
            cim_scr[s] = jnp.where(same_c, cwi_ref[s], 0.0).astype(BF16)
        xr_scr[...] = jnp.zeros_like(xr_scr)
        xi_scr[...] = jnp.zeros_like(xi_scr)

    for s in range(nset):
        us = u_ref[:, s * cw:(s + 1) * cw]
        zr_scr[s] = _dot(us, bre_scr[s])
        zi_scr[s] = _dot(us, bim_scr[s])

    def step(t, carry):
        r0 = pl.multiple_of(t * bsz, bsz)
        out = []
        for s in range(nset):
            xr, xi = carry[2 * s], carry[2 * s + 1]
            ar = ar_scr[s]
            ai = ai_scr[s]
            nr = ar * xr - ai * xi + zr_scr[s, pl.ds(r0, bsz), :]
            ni = ar * xi + ai * xr + zi_scr[s, pl.ds(r0, bsz), :]
            zr_scr[s, pl.ds(r0, bsz), :] = nr
            zi_scr[s, pl.ds(r0, bsz), :] = ni
            out += [nr, ni]
        return tuple(out)

    init = []
    for s in range(nset):
        init += [xr_scr[s], xi_scr[s]]
    final = lax.fori_loop(0, u_ref.shape[0] // bsz, step, tuple(init), unroll=2)
    for s in range(nset):
        xr_scr[s] = final[2 * s]
        xi_scr[s] = final[2 * s + 1]

    pieces = [_dot(zr_scr[s].astype(BF16), cre_scr[s]) - _dot(zi_scr[s].astype(BF16), cim_scr[s])
              for s in range(nset)]
    y = jnp.concatenate(pieces, axis=1) + d_ref[...] * u_ref[...].astype(F32)
    y_ref[...] = y.astype(BF16)


def _s5scan(u_tb, lam_re, lam_im, log_dt, b_re, b_im, c_re, c_im, d_skip, bsz):
    rows, width = u_tb.shape
    nset = S5_SETS
    cw = width // nset
    gs = S5_GROUPS // nset
    sw = gs * S5_STATE
    blk = S5_STEPS * bsz
    vec = lambda a: a.reshape(nset, 1, sw)
    bw = lambda b: jnp.tile(jnp.swapaxes(b, 1, 2).reshape(nset, cw, S5_STATE), (1, 1, gs))
    cw_t = lambda c: jnp.tile(
        jnp.swapaxes(c, 1, 2).reshape(nset, gs, S5_STATE, S5_GROUP).transpose(0, 2, 1, 3).reshape(nset, S5_STATE, cw),
        (1, gs, 1))
    full3 = lambda a, b, c: pl.BlockSpec((a, b, c), lambda i: (0, 0, 0))
    return pl.pallas_call(
        functools.partial(_s5scan_kernel, bsz=bsz),
        grid=(rows // blk,),
        in_specs=[
            pl.BlockSpec((blk, width), lambda i: (i, 0)),
            full3(nset, 1, sw), full3(nset, 1, sw), full3(nset, 1, sw),
            full3(nset, cw, sw), full3(nset, cw, sw),
            full3(nset, sw, cw), full3(nset, sw, cw),
            pl.BlockSpec((1, width), lambda i: (0, 0)),
        ],
        out_specs=pl.BlockSpec((blk, width), lambda i: (i, 0)),
        out_shape=jax.ShapeDtypeStruct((rows, width), BF16),
        scratch_shapes=[
            pltpu.VMEM((nset, cw, sw), BF16), pltpu.VMEM((nset, cw, sw), BF16),
            pltpu.VMEM((nset, sw, cw), BF16), pltpu.VMEM((nset, sw, cw), BF16),
            pltpu.VMEM((nset, bsz, sw), F32), pltpu.VMEM((nset, bsz, sw), F32),
            pltpu.VMEM((nset, bsz, sw), F32), pltpu.VMEM((nset, bsz, sw), F32),
            pltpu.VMEM((nset, blk, sw), F32), pltpu.VMEM((nset, blk, sw), F32),
        ],
        compiler_params=_params(("arbitrary",)),
        name="s5_scan",
    )(u_tb, vec(lam_re), vec(lam_im), vec(jnp.repeat(log_dt, S5_STATE)),
      bw(b_re), bw(b_im), cw_t(c_re), cw_t(c_im), d_skip.reshape(1, width))


def _s5post_kernel(y_ref, z_ref, gw_ref, gb_ref, o_ref):
    y = y_ref[...].astype(F32)
    y = 0.5 * y * (1.0 + jnp.tanh(math.sqrt(2.0 / math.pi) * (y + 0.044715 * (y * y * y))))
    y = y * jax.nn.sigmoid(_dot(y.astype(BF16), gw_ref[...]) + gb_ref[...])
    o_ref[...] = (y * _silu(z_ref[...].astype(F32))).astype(BF16)


def _s5post(y_pre, proj, glu_w, glu_b, tm=512):
    t, w = y_pre.shape
    return pl.pallas_call(
        _s5post_kernel,
        grid=(t // tm,),
        in_specs=[
            pl.BlockSpec((tm, w), lambda i: (i, 0)),
            pl.BlockSpec((tm, w), lambda i: (i, OD_ZD // S5_WIDTH)),
            pl.BlockSpec((w, w), lambda i: (0, 0)),
            pl.BlockSpec((1, w), lambda i: (0, 0)),
        ],
        out_specs=pl.BlockSpec((tm, w), lambda i: (i, 0)),
        out_shape=jax.ShapeDtypeStruct((t, w), BF16),
        compiler_params=_params(("parallel",)),
        name="s5_post",
    )(y_pre, proj, glu_w.astype(BF16), glu_b.reshape(1, w))


def _small_weights(cols):
    padded = jnp.pad(cols, ((0, 0), (0, LANES - cols.shape[1])))
    hi = padded.astype(BF16)
    lo = (padded - hi.astype(F32)).astype(BF16)
    return jnp.concatenate([hi, lo], axis=1)


def _even_layer(x2, mod, pre_g, post_g, in_w, conv_w, conv_b, dt_bias, a_log, d_skip, norm_g, out_w, bsz, seq):
    d = D_MODEL
    shift, scale, gate = (mod[:, i * d:(i + 1) * d].reshape(bsz, 1, d) for i in range(3))
    o_xbc = 2 * SSD_WIDTH
    o_dt = o_xbc + SSD_WIDTH + 2 * SSD_GROUPS * SSD_STATE
    o_q = o_dt + SSD_HEADS
    w = jnp.concatenate([
        in_w[:, :o_xbc],
        in_w[:, o_xbc:o_xbc + SSD_WIDTH],
        in_w[:, o_q:o_q + MOBA_WIDTH] * ATT_SCALE,
        in_w[:, o_q + MOBA_WIDTH:],
        in_w[:, o_xbc + SSD_WIDTH:o_dt],
    ], axis=1).astype(BF16)
    ws = _small_weights(in_w[:, o_dt:o_q])
    proj, small = _inproj(x2, scale, shift, pre_g.reshape(1, d), w, ws, seq)
    y_a = _ssd(proj, small, conv_w, conv_b, dt_bias, a_log, d_skip, norm_g, bsz, seq)
    y_b = _moba(proj, bsz, seq)
    ow = out_w.astype(BF16)
    return _outproj(y_a, y_b, ow[:SSD_WIDTH], ow[SSD_WIDTH:], x2, gate, post_g.reshape(1, d), seq)


def _odd_layer(x2, mod, pre_g, post_g, in_w, fgate_b, lam_re, lam_im, log_dt, b_re, b_im, c_re, c_im,
               d_skip, glu_w, glu_b, out_w, bsz, seq):
    d = D_MODEL
    shift, scale, gate = (mod[:, i * d:(i + 1) * d].reshape(bsz, 1, d) for i in range(3))
    o_f = D_MIX + 3 * FOX_WIDTH
    o_u = o_f + FOX_HEADS
    w = jnp.concatenate([
        in_w[:, :D_MIX],
        in_w[:, D_MIX:D_MIX + FOX_WIDTH] * ATT_SCALE,
        in_w[:, D_MIX + FOX_WIDTH:o_f],
        in_w[:, o_u:],
    ], axis=1).astype(BF16)
    ws = _small_weights(in_w[:, o_f:o_u])
    proj, small = _inproj(x2, scale, shift, pre_g.reshape(1, d), w, ws, seq)

    negf = _fgate(small, fgate_b, bsz, seq)
    y_c = _fox(proj, negf, bsz, seq)

    u_tb = proj[:, OD_U:OD_U + S5_WIDTH].reshape(bsz, seq, S5_WIDTH).transpose(1, 0, 2).reshape(seq * bsz, S5_WIDTH)
    y_tb = _s5scan(u_tb, lam_re, lam_im, log_dt, b_re, b_im, c_re, c_im, d_skip, bsz)
    y_pre = y_tb.reshape(seq, bsz, S5_WIDTH).transpose(1, 0, 2).reshape(bsz * seq, S5_WIDTH)
    y_d = _s5post(y_pre, proj, glu_w, glu_b)

    ow = out_w.astype(BF16)
    return _outproj(y_c, y_d, ow[:FOX_WIDTH], ow[FOX_WIDTH:], x2, gate, post_g.reshape(1, d), seq)


def kernel(x, c, ada_w, ada_b, pre_g, post_g, even_in_w, even_conv_w, even_conv_b, even_dt_bias, even_a_log,
           even_d_skip, even_norm_g, even_out_w, odd_in_w, odd_fgate_b, odd_lam_re, odd_lam_im, odd_log_dt,
           odd_b_re, odd_b_im, odd_c_re, odd_c_im, odd_d_skip, odd_glu_w, odd_glu_b, odd_out_w):
    bsz, seq, d = x.shape
    depth = ada_w.shape[0]
    mod = _ada_mod(c, ada_w, ada_b)
    x2 = x.reshape(bsz * seq, d)
    for layer in range(depth):
        i = layer // 2
        if layer % 2 == 0:
            x2 = _even_layer(x2, mod[layer], pre_g[layer], post_g[layer], even_in_w[i], even_conv_w[i],
                             even_conv_b[i], even_dt_bias[i], even_a_log[i], even_d_skip[i], even_norm_g[i],
                             even_out_w[i], bsz, seq)
        else:
            x2 = _odd_layer(x2, mod[layer], pre_g[layer], post_g[layer], odd_in_w[i], odd_fgate_b[i],
                            odd_lam_re[i], odd_lam_im[i], odd_log_dt[i], odd_b_re[i], odd_b_im[i], odd_c_re[i],
                            odd_c_im[i], odd_d_skip[i], odd_glu_w[i], odd_glu_b[i], odd_out_w[i], bsz, seq)
    return x2.reshape(bsz, seq, d)
```

```python
import functools
import math

import jax
import jax.numpy as jnp
from jax import lax
from jax.experimental import pallas as pl
from jax.experimental.pallas import tpu as pltpu

F32 = jnp.float32
BF16 = jnp.bfloat16

D_MODEL = 1024
HEAD_DIM = 64
D_MIX = 2 * D_MODEL
SSD_WIDTH = 1024
SSD_HEADS = 16
SSD_GROUPS = 2
SSD_STATE = 128
SSD_CONV = 4
SSD_CHUNK = 128
MOBA_WIDTH = 1024
MOBA_BLOCK = 256
MOBA_TOPK = 3
FOX_WIDTH = 1536
FOX_HEADS = 24
S5_WIDTH = 512
S5_GROUP = 16
S5_GROUP_SHIFT = 4
S5_GROUPS = 32
S5_STATE = 64
S5_STATE_SHIFT = 6
S5_SETS = 4
S5_STEPS = 64
RMS_EPS = 1e-6
ATT_BLOCK = 256
ATT_BLOCK_SHIFT = 8
LOG2E = math.log2(math.e)
ATT_SCALE = LOG2E / math.sqrt(HEAD_DIM)
VT_ROWS = HEAD_DIM + 16
LANES = 128
NEG = -1e30
VMEM_LIMIT = 48 * 1024 * 1024

EV_ZA, EV_ZB, EV_XS, EV_Q, EV_K, EV_V, EV_BC, EV_N = 0, 1024, 2048, 3072, 4096, 5120, 6144, 6656
OD_ZC, OD_ZD, OD_Q, OD_K, OD_V, OD_U, OD_N = 0, 1536, 2048, 3584, 5120, 6656, 7168


def _nt(a, b):
    return lax.dot_general(a, b, (((1,), (1,)), ((), ())), preferred_element_type=F32)


def _dot(a, b):
    return jnp.dot(a, b, preferred_element_type=F32)


def _split3(x):
    hi = x.astype(BF16)
    r = x - hi.astype(F32)
    mid = r.astype(BF16)
    lo = (r - mid.astype(F32)).astype(BF16)
    return hi, mid, lo


def _dot_exact_rhs(x, m_bf16):
    hi, mid, lo = _split3(x)
    return _dot(hi, m_bf16) + _dot(mid, m_bf16) + _dot(lo, m_bf16)


def _silu(x):
    return x * jax.nn.sigmoid(x)


def _softplus(x):
    return jnp.maximum(x, 0.0) + jnp.log1p(jnp.exp(-jnp.abs(x)))


def _params(sem, limit=VMEM_LIMIT):
    return pltpu.CompilerParams(dimension_semantics=sem, vmem_limit_bytes=limit)


def _ada_kernel(c_ref, w_ref, b_ref, o_ref):
    cond = _silu(c_ref[...])
    hi, mid, lo = _split3(cond)
    w = w_ref[0]
    whi, wmid, wlo = _split3(w)
    acc = _dot(hi, whi) + _dot(hi, wmid) + _dot(mid, whi)
    acc = acc + _dot(hi, wlo) + _dot(mid, wmid) + _dot(lo, whi)
    o_ref[0] = acc + b_ref[0]


def _ada_mod(c, ada_w, ada_b):
    depth, d, d3 = ada_w.shape
    bsz = c.shape[0]
    nj = d3 // d
    return pl.pallas_call(
        _ada_kernel,
        grid=(depth, nj),
        in_specs=[
            pl.BlockSpec((bsz, d), lambda l, j: (0, 0)),
            pl.BlockSpec((1, d, d), lambda l, j: (l, 0, j)),
            pl.BlockSpec((1, 1, d), lambda l, j: (l, 0, j)),
        ],
        out_specs=pl.BlockSpec((1, bsz, d), lambda l, j: (l, 0, j)),
        out_shape=jax.ShapeDtypeStruct((depth, bsz, d3), F32),
        compiler_params=_params(("parallel", "parallel")),
        name="ada_mod",
    )(c, ada_w, ada_b.reshape(depth, 1, d3))


def _inproj_kernel(x_ref, sc_ref, sh_ref, g_ref, w_ref, ws_ref, o_ref, os_ref, h_scr):
    @pl.when(pl.program_id(1) == 0)
    def _():
        x = x_ref[...]
        ms = jnp.mean(x * x, axis=-1, keepdims=True)
        xn = x * lax.rsqrt(ms + RMS_EPS) * g_ref[...]
        h = xn * (1.0 + sc_ref[0]) + sh_ref[0]
        h_b = h.astype(BF16)
        h_scr[...] = h_b
        r = _dot(h_b, ws_ref[...])
        os_ref[...] = r[:, :LANES] + r[:, LANES:]

    o_ref[...] = _dot(h_scr[...], w_ref[...]).astype(BF16)


def _inproj(x2, scale, shift, g, w, ws, seq, tm=1024):
    t, d = x2.shape
    n = w.shape[1]
    tn = n // 2
    per = seq // tm
    return pl.pallas_call(
        _inproj_kernel,
        grid=(t // tm, n // tn),
        in_specs=[
            pl.BlockSpec((tm, d), lambda i, j: (i, 0)),
            pl.BlockSpec((1, 1, d), lambda i, j: (i // per, 0, 0)),
            pl.BlockSpec((1, 1, d), lambda i, j: (i // per, 0, 0)),
            pl.BlockSpec((1, d), lambda i, j: (0, 0)),
            pl.BlockSpec((d, tn), lambda i, j: (0, j)),
            pl.BlockSpec((d, 2 * LANES), lambda i, j: (0, 0)),
        ],
        out_specs=[
            pl.BlockSpec((tm, tn), lambda i, j: (i, j)),
            pl.BlockSpec((tm, LANES), lambda i, j: (i, 0)),
        ],
        out_shape=[
            jax.ShapeDtypeStruct((t, n), BF16),
            jax.ShapeDtypeStruct((t, LANES), F32),
        ],
        scratch_shapes=[pltpu.VMEM((tm, d), BF16)],
        compiler_params=_params(("parallel", "arbitrary")),
        name="in_proj",
    )(x2, scale, shift, g, w, ws)


def _outproj_kernel(a_ref, b_ref, wa_ref, wb_ref, x_ref, gate_ref, pg_ref, o_ref):
    y = _dot(a_ref[...], wa_ref[...]) + _dot(b_ref[...], wb_ref[...])
    ms = jnp.mean(y * y, axis=-1, keepdims=True)
    yn = y * lax.rsqrt(ms + RMS_EPS) * pg_ref[...]
    o_ref[...] = x_ref[...] + gate_ref[0] * yn


def _outproj(a, b, wa, wb, x2, gate, pg, seq, tm=512):
    t, d = x2.shape
    ka, kb = a.shape[1], b.shape[1]
    per = seq // tm
    return pl.pallas_call(
        _outproj_kernel,
        grid=(t // tm,),
        in_specs=[
            pl.BlockSpec((tm, ka), lambda i: (i, 0)),
            pl.BlockSpec((tm, kb), lambda i: (i, 0)),
            pl.BlockSpec((ka, d), lambda i: (0, 0)),
            pl.BlockSpec((kb, d), lambda i: (0, 0)),
            pl.BlockSpec((tm, d), lambda i: (i, 0)),
            pl.BlockSpec((1, 1, d), lambda i: (i // per, 0, 0)),
            pl.BlockSpec((1, d), lambda i: (0, 0)),
        ],
        out_specs=pl.BlockSpec((tm, d), lambda i: (i, 0)),
        out_shape=jax.ShapeDtypeStruct((t, d), F32),
        compiler_params=_params(("parallel",)),
        name="out_proj",
    )(a, b, wa, wb, x2, gate, pg)


def _causal_conv_silu(in_ref, prev_ref, first, shift, w, b):
    lc = in_ref.shape[0]
    raw = in_ref[...]
    prev = prev_ref[...]
    prev = jnp.where(first, jnp.zeros_like(prev), prev)
    ext = jnp.concatenate([prev, raw], axis=0)
    shifted = _dot(shift, ext)
    acc = b + w[SSD_CONV - 1:SSD_CONV, :] * raw.astype(F32)
    for j in range(SSD_CONV - 1):
        acc = acc + w[j:j + 1, :] * shifted[j * lc:(j + 1) * lc, :]
    return _silu(acc)


def _ssd_kernel(z_ref, xs_ref, bc_ref, xsp_ref, bcp_ref, dt_ref, cwx_ref, cwb_ref, cbx_ref, cbb_ref, dtb_ref,
                alog_ref, dexp_ref, ng_ref, e_ref, sh_ref, o_ref, state_scr):
    lc = SSD_CHUNK
    hw = SSD_HEADS // SSD_GROUPS * HEAD_DIM
    first = pl.program_id(1) == 0

    @pl.when(first)
    def _():
        state_scr[...] = jnp.zeros_like(state_scr)

    xs = _causal_conv_silu(xs_ref, xsp_ref, first, sh_ref[...], cwx_ref[...], cbx_ref[...])
    bc = _causal_conv_silu(bc_ref, bcp_ref, first, sh_ref[...], cwb_ref[...], cbb_ref[...])

    dt = _softplus(dt_ref[...] + dtb_ref[...])
    adt = dt * (-jnp.exp(alog_ref[...]))
    rows = lax.broadcasted_iota(jnp.int32, (lc, lc), 0)
    cols = lax.broadcasted_iota(jnp.int32, (lc, lc), 1)
    lower = rows >= cols
    a_cum = _dot_exact_rhs_lhs(lower.astype(BF16), adt)
    a_cum_t = a_cum.T
    a_last = a_cum[lc - 1:lc, :]
    stacked = jnp.concatenate([dt, jnp.exp(a_last - a_cum), jnp.exp(a_cum)], axis=0)
    expanded = _dot(stacked.astype(BF16), e_ref[...])
    dt_e = expanded[0:lc]
    dec_e = expanded[lc:2 * lc]
    ea_e = expanded[2 * lc:3 * lc]
    al_e = _dot_exact_rhs(jnp.broadcast_to(jnp.exp(a_last), (8, LANES)), e_ref[...])[0:1]

    xdt = xs * dt_e
    xdt_b = xdt.astype(BF16)
    xd_b = (xdt * dec_e).astype(BF16)
    lane = lax.broadcasted_iota(jnp.int32, (1, LANES), 1)
    pieces = []
    for g in range(SSD_GROUPS):
        bg = bc[:, g * SSD_STATE:(g + 1) * SSD_STATE]
        cg = bc[:, (SSD_GROUPS + g) * SSD_STATE:(SSD_GROUPS + g + 1) * SSD_STATE]
        cg_b = cg.astype(BF16)
        cb = _nt(cg_b, bg.astype(BF16))
        st = state_scr[:, g * hw:(g + 1) * hw]
        y_off = _dot(cg_b, st.astype(BF16)) * ea_e[:, g * hw:(g + 1) * hw]
        new_st = _dot(bg.T.astype(BF16), xd_b[:, g * hw:(g + 1) * hw])
        state_scr[:, g * hw:(g + 1) * hw] = al_e[:, g * hw:(g + 1) * hw] * st + new_st
        for pair in range(SSD_HEADS // SSD_GROUPS // 2):
            acc = None
            c0 = g * hw + pair * LANES
            xp = xdt_b[:, c0:c0 + LANES]
            for half in range(2):
                h = g * (SSD_HEADS // SSD_GROUPS) + pair * 2 + half
                diff = a_cum[:, h:h + 1] - a_cum_t[h:h + 1, :]
                decay = jnp.exp(jnp.where(lower, diff, NEG))
                m = (cb * decay).astype(BF16)
                in_half = (lane >= half * HEAD_DIM) & (lane < (half + 1) * HEAD_DIM)
                part = _dot(m, jnp.where(in_half, xp, jnp.zeros_like(xp)))
                acc = part if acc is None else acc + part
            pieces.append(acc + y_off[:, pair * LANES:(pair + 1) * LANES])
    y = jnp.concatenate(pieces, axis=1) + dexp_ref[...] * xs
    y = y * _silu(z_ref[...].astype(F32))
    ms = jnp.mean(y * y, axis=-1, keepdims=True)
    o_ref[...] = (y * lax.rsqrt(ms + RMS_EPS) * ng_ref[...]).astype(BF16)


def _dot_exact_rhs_lhs(m_bf16, x):
    hi, mid, lo = _split3(x)
    return _dot(m_bf16, hi) + _dot(m_bf16, mid) + _dot(m_bf16, lo)


def _ssd(proj, small, conv_w, conv_b, dt_bias, a_log, d_skip, norm_g, bsz, seq):
    lc = SSD_CHUNK
    nc = seq // lc
    t = bsz * seq
    pad = LANES - SSD_HEADS
    dtb = jnp.pad(dt_bias, (0, pad)).reshape(1, LANES)
    alog = jnp.pad(a_log, (0, pad)).reshape(1, LANES)
    dexp = jnp.repeat(d_skip, HEAD_DIM).reshape(1, SSD_WIDTH)
    expand = (jnp.arange(LANES)[:, None] == (jnp.arange(SSD_WIDTH) // HEAD_DIM)[None, :]).astype(BF16)
    nbc = 2 * SSD_GROUPS * SSD_STATE
    src = lc + jnp.arange(lc)[None, :] - (SSD_CONV - 1) + jnp.arange(SSD_CONV - 1)[:, None]
    shift = (src.reshape(-1, 1) == jnp.arange(2 * lc)[None, :]).astype(BF16)
    row = lambda b, c: b * nc + c
    prev = lambda b, c: b * nc + jnp.maximum(c - 1, 0)
    const = lambda b, c: (0, 0)
    return pl.pallas_call(
        _ssd_kernel,
        grid=(bsz, nc),
        in_specs=[
            pl.BlockSpec((lc, SSD_WIDTH), lambda b, c: (row(b, c), EV_ZA // SSD_WIDTH)),
            pl.BlockSpec((lc, SSD_WIDTH), lambda b, c: (row(b, c), EV_XS // SSD_WIDTH)),
            pl.BlockSpec((lc, nbc), lambda b, c: (row(b, c), EV_BC // nbc)),
            pl.BlockSpec((lc, SSD_WIDTH), lambda b, c: (prev(b, c), EV_XS // SSD_WIDTH)),
            pl.BlockSpec((lc, nbc), lambda b, c: (prev(b, c), EV_BC // nbc)),
            pl.BlockSpec((lc, LANES), lambda b, c: (row(b, c), 0)),
            pl.BlockSpec((SSD_CONV, SSD_WIDTH), const),
            pl.BlockSpec((SSD_CONV, nbc), const),
            pl.BlockSpec((1, SSD_WIDTH), const),
            pl.BlockSpec((1, nbc), const),
            pl.BlockSpec((1, LANES), const),
            pl.BlockSpec((1, LANES), const),
            pl.BlockSpec((1, SSD_WIDTH), const),
            pl.BlockSpec((1, SSD_WIDTH), const),
            pl.BlockSpec((LANES, SSD_WIDTH), const),
            pl.BlockSpec(((SSD_CONV - 1) * lc, 2 * lc), const),
        ],
        out_specs=pl.BlockSpec((lc, SSD_WIDTH), lambda b, c: (row(b, c), 0)),
        out_shape=jax.ShapeDtypeStruct((t, SSD_WIDTH), BF16),
        scratch_shapes=[pltpu.VMEM((SSD_STATE, SSD_WIDTH), F32)],
        compiler_params=_params(("parallel", "arbitrary")),
        name="ssd",
    )(proj, proj, proj, proj, proj, small,
      conv_w[:, :SSD_WIDTH], conv_w[:, SSD_WIDTH:], conv_b[:SSD_WIDTH].reshape(1, -1),
      conv_b[SSD_WIDTH:].reshape(1, -1), dtb, alog, dexp, norm_g.reshape(1, -1), expand, shift)


def _fold_rows(x, op):
    out = x[0:8, :]
    for i in range(1, x.shape[0] // 8):
        out = op(out, x[8 * i:8 * (i + 1), :])
    return out


def _attend_pair(qa_scr, ka_scr, vt_scr, s_scr, p_scr, ot_scr, bias_fn):
    blk = ATT_BLOCK
    nb = qa_scr.shape[1] // blk
    heads = (0, 1)
    causal = (lax.broadcasted_iota(jnp.int32, (blk, blk), 0)
              <= lax.broadcasted_iota(jnp.int32, (blk, blk), 1))

    def score_tile(half, qb, n, m8):
        s = _nt(ka_scr[half, n * blk:(n + 1) * blk, :], qa_scr[half, qb * blk:(qb + 1) * blk, :])
        bias = bias_fn(half, qb, n)
        if bias is not None:
            s = s + bias
        if n == qb:
            s = jnp.where(causal, s, NEG)
        s_scr[half, qb % 2, n] = s
        part = _fold_rows(s, jnp.maximum)
        return part if m8 is None else jnp.maximum(m8, part)

    def prob_tile(half, qb, n, m):
        p = jnp.exp2(s_scr[half, qb % 2, n] - m)
        p_scr[half, qb % 2, n * blk:(n + 1) * blk, :] = p.astype(BF16)

    m8 = [score_tile(half, 0, 0, None) for half in heads]
    for qb in range(nb):
        m = [jnp.max(m8[half], axis=0, keepdims=True) for half in heads]
        nxt = qb + 1
        m8 = [None, None]
        for n in range(nxt + 1):
            for half in heads:
                if nxt < nb:
                    m8[half] = score_tile(half, nxt, n, m8[half])
            for half in heads:
                if n <= qb:
                    prob_tile(half, qb, n, m[half])
        keys = (qb + 1) * blk
        for half in heads:
            acc = _dot(vt_scr[half, :, 0:keys], p_scr[half, qb % 2, 0:keys, :])
            out = acc[0:HEAD_DIM] / acc[HEAD_DIM:HEAD_DIM + 1]
            ot_scr[half * HEAD_DIM:(half + 1) * HEAD_DIM, qb * blk:(qb + 1) * blk] = out


def _store_vt(v_ref, vt_scr):
    ones = jnp.ones((VT_ROWS - HEAD_DIM, ATT_BLOCK), BF16)
    for n in range(v_ref.shape[0] // ATT_BLOCK):
        cols = slice(n * ATT_BLOCK, (n + 1) * ATT_BLOCK)
        vt = v_ref[cols, :].astype(F32).T.astype(BF16)
        for half in range(2):
            vt_scr[half, 0:HEAD_DIM, cols] = vt[half * HEAD_DIM:(half + 1) * HEAD_DIM, :]
            vt_scr[half, HEAD_DIM:VT_ROWS, cols] = ones


def _gated_output(ot_scr, z_ref, o_ref):
    for i in range(ot_scr.shape[1] // ATT_BLOCK):
        rows = slice(i * ATT_BLOCK, (i + 1) * ATT_BLOCK)
        o = ot_scr[:, rows].T
        o_ref[rows, :] = (o * _silu(z_ref[rows, :].astype(F32))).astype(BF16)


def _head_lanes(half):
    lane = lax.broadcasted_iota(jnp.int32, (1, LANES), 1)
    return (lane >= half * HEAD_DIM) & (lane < (half + 1) * HEAD_DIM)


def _moba_kernel(q_ref, k_ref, v_ref, z_ref, o_ref, vt_scr, kbar_scr, bias_scr, qa_scr, ka_scr, s_scr, p_scr,
                 ot_scr):
    blk = ATT_BLOCK
    seq = q_ref.shape[0]
    nb = seq // blk
    _store_vt(v_ref, vt_scr)
    for n in range(nb):
        kb = k_ref[n * blk:(n + 1) * blk, :].astype(F32)
        kbar_scr[n:n + 1, :] = jnp.mean(kb, axis=0, keepdims=True)
    n_idx = lax.broadcasted_iota(jnp.int32, (nb, seq), 0)
    q_blk = lax.broadcasted_iota(jnp.int32, (nb, seq), 1) >> ATT_BLOCK_SHIFT

    for half in range(2):
        in_half = _head_lanes(half)
        q = q_ref[...]
        qi = jnp.where(in_half, q, jnp.zeros_like(q))
        qa_scr[half] = qi
        ka_scr[half] = k_ref[...]
        kbar = jnp.where(in_half, kbar_scr[...], 0.0)
        kb_hi = kbar.astype(BF16)
        kb_lo = (kbar - kb_hi.astype(F32)).astype(BF16)
        gate = _nt(kb_hi, qi) + _nt(kb_lo, qi)
        rank = jnp.zeros((nb, seq), F32)
        for mth in range(nb):
            gm = gate[mth:mth + 1, :]
            beats = (gm > gate) | ((gm == gate) & (mth < n_idx))
            rank = rank + jnp.where(beats & (mth < q_blk), 1.0, 0.0)
        chosen = (rank < float(MOBA_TOPK)) & (n_idx < q_blk)
        bias_scr[half] = jnp.where(chosen, 0.0, NEG)

    def bias_fn(half, qb, n):
        if n == qb or qb <= MOBA_TOPK:
            return None
        return bias_scr[half, n:n + 1, qb * blk:(qb + 1) * blk]

    _attend_pair(qa_scr, ka_scr, vt_scr, s_scr, p_scr, ot_scr, bias_fn)
    _gated_output(ot_scr, z_ref, o_ref)


def _moba(proj, bsz, seq):
    blk = ATT_BLOCK
    nb = seq // blk
    pairs = MOBA_WIDTH // LANES
    t = bsz * seq
    return pl.pallas_call(
        _moba_kernel,
        grid=(bsz, pairs),
        in_specs=[
            pl.BlockSpec((seq, LANES), lambda b, p: (b, EV_Q // LANES + p)),
            pl.BlockSpec((seq, LANES), lambda b, p: (b, EV_K // LANES + p)),
            pl.BlockSpec((seq, LANES), lambda b, p: (b, EV_V // LANES + p)),
            pl.BlockSpec((seq, LANES), lambda b, p: (b, EV_ZB // LANES + p)),
        ],
        out_specs=pl.BlockSpec((seq, LANES), lambda b, p: (b, p)),
        out_shape=jax.ShapeDtypeStruct((t, MOBA_WIDTH), BF16),
        scratch_shapes=[
            pltpu.VMEM((2, VT_ROWS, seq), BF16),
            pltpu.VMEM((nb, LANES), F32),
            pltpu.VMEM((2, nb, seq), F32),
            pltpu.VMEM((2, seq, LANES), BF16),
            pltpu.VMEM((2, seq, LANES), BF16),
            pltpu.VMEM((2, 2, nb, blk, blk), F32),
            pltpu.VMEM((2, 2, seq, blk), BF16),
            pltpu.VMEM((LANES, seq), F32),
        ],
        compiler_params=_params(("parallel", "parallel")),
        name="moba",
    )(proj, proj, proj, proj)


def _fgate_kernel(f_ref, fb_ref, o_ref):
    blk = ATT_BLOCK
    rows = lax.broadcasted_iota(jnp.int32, (blk, blk), 0)
    cols = lax.broadcasted_iota(jnp.int32, (blk, blk), 1)
    lower = (rows >= cols).astype(BF16)
    carry = jnp.zeros((1, LANES), F32)
    for i in range(f_ref.shape[0] // blk):
        nlf = _softplus(-(f_ref[i * blk:(i + 1) * blk, :] + fb_ref[...]))
        csum = _dot_exact_rhs_lhs(lower, nlf) + carry
        o_ref[i * blk:(i + 1) * blk, :] = csum
        carry = csum[blk - 1:blk, :]


def _fgate(small, fgate_b, bsz, seq):
    fb = jnp.pad(fgate_b, (0, LANES - FOX_HEADS)).reshape(1, LANES)
    return pl.pallas_call(
        _fgate_kernel,
        grid=(bsz,),
        in_specs=[
            pl.BlockSpec((seq, LANES), lambda b: (b, 0)),
            pl.BlockSpec((1, LANES), lambda b: (0, 0)),
        ],
        out_specs=pl.BlockSpec((seq, LANES), lambda b: (b, 0)),
        out_shape=jax.ShapeDtypeStruct((bsz * seq, LANES), F32),
        compiler_params=_params(("parallel",)),
        name="fox_gate",
    )(small, fb)


def _fox_kernel(q_ref, k_ref, v_ref, z_ref, nf_ref, o_ref, vt_scr, fb_scr, qa_scr, ka_scr, s_scr, p_scr, ot_scr):
    pair = pl.program_id(1)
    nb = q_ref.shape[0] // ATT_BLOCK
    blk = ATT_BLOCK
    _store_vt(v_ref, vt_scr)
    lane = lax.broadcasted_iota(jnp.int32, (1, LANES), 1)
    for half in range(2):
        own_lane = lane == pair * 2 + half
        for n in range(nb):
            col = jnp.sum(jnp.where(own_lane, nf_ref[n * blk:(n + 1) * blk, :], 0.0), axis=1, keepdims=True)
            fb_scr[half, n] = jnp.broadcast_to(col * LOG2E, (blk, LANES))
        q = q_ref[...]
        qa_scr[half] = jnp.where(_head_lanes(half), q, jnp.zeros_like(q))
        ka_scr[half] = k_ref[...]

    def bias_fn(half, qb, n):
        fb = fb_scr[half, n]
        return jnp.concatenate([fb, fb], axis=1)

    _attend_pair(qa_scr, ka_scr, vt_scr, s_scr, p_scr, ot_scr, bias_fn)
    _gated_output(ot_scr, z_ref, o_ref)


def _fox(proj, negf, bsz, seq):
    blk = ATT_BLOCK
    nb = seq // blk
    pairs = FOX_WIDTH // LANES
    t = bsz * seq
    return pl.pallas_call(
        _fox_kernel,
        grid=(bsz, pairs),
        in_specs=[
            pl.BlockSpec((seq, LANES), lambda b, p: (b, OD_Q // LANES + p)),
            pl.BlockSpec((seq, LANES), lambda b, p: (b, OD_K // LANES + p)),
            pl.BlockSpec((seq, LANES), lambda b, p: (b, OD_V // LANES + p)),
            pl.BlockSpec((seq, LANES), lambda b, p: (b, OD_ZC // LANES + p)),
            pl.BlockSpec((seq, LANES), lambda b, p: (b, 0)),
        ],
        out_specs=pl.BlockSpec((seq, LANES), lambda b, p: (b, p)),
        out_shape=jax.ShapeDtypeStruct((t, FOX_WIDTH), BF16),
        scratch_shapes=[
            pltpu.VMEM((2, VT_ROWS, seq), BF16),
            pltpu.VMEM((2, nb, blk, LANES), F32),
            pltpu.VMEM((2, seq, LANES), BF16),
            pltpu.VMEM((2, seq, LANES), BF16),
            pltpu.VMEM((2, 2, nb, blk, blk), F32),
            pltpu.VMEM((2, 2, seq, blk), BF16),
            pltpu.VMEM((LANES, seq), F32),
        ],
        compiler_params=_params(("parallel", "parallel")),
        name="fox",
    )(proj, proj, proj, proj, negf)


def _s5scan_kernel(u_ref, lr_ref, li_ref, ldt_ref, bwr_ref, bwi_ref, cwr_ref, cwi_ref, d_ref, y_ref,
                   bre_scr, bim_scr, cre_scr, cim_scr, ar_scr, ai_scr, xr_scr, xi_scr, zr_scr, zi_scr, *, bsz):
    nset, cw, sw = bre_scr.shape

    @pl.when(pl.program_id(0) == 0)
    def _():
        chan_grp = lax.broadcasted_iota(jnp.int32, (cw, sw), 0) >> S5_GROUP_SHIFT
        state_grp = lax.broadcasted_iota(jnp.int32, (cw, sw), 1) >> S5_STATE_SHIFT
        same_b = chan_grp == state_grp
        same_c = ((lax.broadcasted_iota(jnp.int32, (sw, cw), 0) >> S5_STATE_SHIFT)
                  == (lax.broadcasted_iota(jnp.int32, (sw, cw), 1) >> S5_GROUP_SHIFT))
        for s in range(nset):
            lr = lr_ref[s]
            li = li_ref[s]
            dt = jnp.exp(ldt_ref[s])
            mag = jnp.exp(lr * dt)
            ar = mag * jnp.cos(li * dt)
            ai = mag * jnp.sin(li * dt)
            den = lr * lr + li * li
            qr = ((ar - 1.0) * lr + ai * li) / den
            qi = (ai * lr - (ar - 1.0) * li) / den
            ar_scr[s] = jnp.broadcast_to(ar, (bsz, sw))
            ai_scr[s] = jnp.broadcast_to(ai, (bsz, sw))
            bwr = bwr_ref[s]
            bwi = bwi_ref[s]
            bre_scr[s] = jnp.where(same_b, qr * bwr - qi * bwi, 0.0).astype(BF16)
            bim_scr[s] = jnp.where(same_b, qr * bwi + qi * bwr, 0.0).astype(BF16)
            cre_scr[s] = jnp.where(same_c, cwr_ref[s], 0.0).astype(BF16)
            cim_scr[s] = jnp.where(same_c, cwi_ref[s], 0.0).astype(BF16)
        xr_scr[...] = jnp.zeros_like(xr_scr)
        xi_scr[...] = jnp.zeros_like(xi_scr)

    for s in range(nset):
        us = u_ref[:, s * cw:(s + 1) * cw]
        zr_scr[s] = _dot(us, bre_scr[s])
        zi_scr[s] = _dot(us, bim_scr[s])

    def step(t, carry):
        r0 = pl.multiple_of(t * bsz, bsz)
        out = []
        for s in range(nset):
            xr, xi = carry[2 * s], carry[2 * s + 1]
            ar = ar_scr[s]
            ai = ai_scr[s]
            nr = ar * xr - ai * xi + zr_scr[s, pl.ds(r0, bsz), :]
            ni = ar * xi + ai * xr + zi_scr[s, pl.ds(r0, bsz), :]
            zr_scr[s, pl.ds(r0, bsz), :] = nr
            zi_scr[s, pl.ds(r0, bsz), :] = ni
            out += [nr, ni]
        return tuple(out)

    init = []
    for s in range(nset):
        init += [xr_scr[s], xi_scr[s]]
    final = lax.fori_loop(0, u_ref.shape[0] // bsz, step, tuple(init), unroll=2)
    for s in range(nset):
        xr_scr[s] = final[2 * s]
        xi_scr[s] = final[2 * s + 1]

    pieces = [_dot(zr_scr[s].astype(BF16), cre_scr[s]) - _dot(zi_scr[s].astype(BF16), cim_scr[s])
              for s in range(nset)]
    y = jnp.concatenate(pieces, axis=1) + d_ref[...] * u_ref[...].astype(F32)
    y_ref[...] = y.astype(BF16)


def _s5scan(u_tb, lam_re, lam_im, log_dt, b_re, b_im, c_re, c_im, d_skip, bsz):
    rows, width = u_tb.shape
    nset = S5_SETS
    cw = width // nset
    gs = S5_GROUPS // nset
    sw = gs * S5_STATE
    blk = S5_STEPS * bsz
    vec = lambda a: a.reshape(nset, 1, sw)
    bw = lambda b: jnp.tile(jnp.swapaxes(b, 1, 2).reshape(nset, cw, S5_STATE), (1, 1, gs))
    cw_t = lambda c: jnp.tile(
        jnp.swapaxes(c, 1, 2).reshape(nset, gs, S5_STATE, S5_GROUP).transpose(0, 2, 1, 3).reshape(nset, S5_STATE, cw),
        (1, gs, 1))
    full3 = lambda a, b, c: pl.BlockSpec((a, b, c), lambda i: (0, 0, 0))
    return pl.pallas_call(
        functools.partial(_s5scan_kernel, bsz=bsz),
        grid=(rows // blk,),
        in_specs=[
            pl.BlockSpec((blk, width), lambda i: (i, 0)),
            full3(nset, 1, sw), full3(nset, 1, sw), full3(nset, 1, sw),
            full3(nset, cw, sw), full3(nset, cw, sw),
            full3(nset, sw, cw), full3(nset, sw, cw),
            pl.BlockSpec((1, width), lambda i: (0, 0)),
        ],
        out_specs=pl.BlockSpec((blk, width), lambda i: (i, 0)),
        out_shape=jax.ShapeDtypeStruct((rows, width), BF16),
        scratch_shapes=[
            pltpu.VMEM((nset, cw, sw), BF16), pltpu.VMEM((nset, cw, sw), BF16),
            pltpu.VMEM((nset, sw, cw), BF16), pltpu.VMEM((nset, sw, cw), BF16),
            pltpu.VMEM((nset, bsz, sw), F32), pltpu.VMEM((nset, bsz, sw), F32),
            pltpu.VMEM((nset, bsz, sw), F32), pltpu.VMEM((nset, bsz, sw), F32),
            pltpu.VMEM((nset, blk, sw), F32), pltpu.VMEM((nset, blk, sw), F32),
        ],
        compiler_params=_params(("arbitrary",)),
        name="s5_scan",
    )(u_tb, vec(lam_re), vec(lam_im), vec(jnp.repeat(log_dt, S5_STATE)),
      bw(b_re), bw(b_im), cw_t(c_re), cw_t(c_im), d_skip.reshape(1, width))


def _s5post_kernel(y_ref, z_ref, gw_ref, gb_ref, o_ref):
    y = y_ref[...].astype(F32)
    y = 0.5 * y * (1.0 + jnp.tanh(math.sqrt(2.0 / math.pi) * (y + 0.044715 * (y * y * y))))
    y = y * jax.nn.sigmoid(_dot(y.astype(BF16), gw_ref[...]) + gb_ref[...])
    o_ref[...] = (y * _silu(z_ref[...].astype(F32))).astype(BF16)


def _s5post(y_pre, proj, glu_w, glu_b, tm=1024):
    t, w = y_pre.shape
    return pl.pallas_call(
        _s5post_kernel,
        grid=(t // tm,),
        in_specs=[
            pl.BlockSpec((tm, w), lambda i: (i, 0)),
            pl.BlockSpec((tm, w), lambda i: (i, OD_ZD // S5_WIDTH)),
            pl.BlockSpec((w, w), lambda i: (0, 0)),
            pl.BlockSpec((1, w), lambda i: (0, 0)),
        ],
        out_specs=pl.BlockSpec((tm, w), lambda i: (i, 0)),
        out_shape=jax.ShapeDtypeStruct((t, w), BF16),
        compiler_params=_params(("parallel",)),
        name="s5_post",
    )(y_pre, proj, glu_w.astype(BF16), glu_b.reshape(1, w))


def _small_weights(cols):
    padded = jnp.pad(cols, ((0, 0), (0, LANES - cols.shape[1])))
    hi = padded.astype(BF16)
    lo = (padded - hi.astype(F32)).astype(BF16)
    return jnp.concatenate([hi, lo], axis=1)


def _even_layer(x2, mod, pre_g, post_g, in_w, conv_w, conv_b, dt_bias, a_log, d_skip, norm_g, out_w, bsz, seq):
    d = D_MODEL
    shift, scale, gate = (mod[:, i * d:(i + 1) * d].reshape(bsz, 1, d) for i in range(3))
    o_xbc = 2 * SSD_WIDTH
    o_dt = o_xbc + SSD_WIDTH + 2 * SSD_GROUPS * SSD_STATE
    o_q = o_dt + SSD_HEADS
    w = jnp.concatenate([
        in_w[:, :o_xbc],
        in_w[:, o_xbc:o_xbc + SSD_WIDTH],
        in_w[:, o_q:o_q + MOBA_WIDTH] * ATT_SCALE,
        in_w[:, o_q + MOBA_WIDTH:],
        in_w[:, o_xbc + SSD_WIDTH:o_dt],
    ], axis=1).astype(BF16)
    ws = _small_weights(in_w[:, o_dt:o_q])
    proj, small = _inproj(x2, scale, shift, pre_g.reshape(1, d), w, ws, seq)
    y_a = _ssd(proj, small, conv_w, conv_b, dt_bias, a_log, d_skip, norm_g, bsz, seq)
    y_b = _moba(proj, bsz, seq)
    ow = out_w.astype(BF16)
    return _outproj(y_a, y_b, ow[:SSD_WIDTH], ow[SSD_WIDTH:], x2, gate, post_g.reshape(1, d), seq)


def _odd_layer(x2, mod, pre_g, post_g, in_w, fgate_b, lam_re, lam_im, log_dt, b_re, b_im, c_re, c_im,
               d_skip, glu_w, glu_b, out_w, bsz, seq):
    d = D_MODEL
    shift, scale, gate = (mod[:, i * d:(i + 1) * d].reshape(bsz, 1, d) for i in range(3))
    o_f = D_MIX + 3 * FOX_WIDTH
    o_u = o_f + FOX_HEADS
    w = jnp.concatenate([
        in_w[:, :D_MIX],
        in_w[:, D_MIX:D_MIX + FOX_WIDTH] * ATT_SCALE,
        in_w[:, D_MIX + FOX_WIDTH:o_f],
        in_w[:, o_u:],
    ], axis=1).astype(BF16)
    ws = _small_weights(in_w[:, o_f:o_u])
    proj, small = _inproj(x2, scale, shift, pre_g.reshape(1, d), w, ws, seq)

    negf = _fgate(small, fgate_b, bsz, seq)
    y_c = _fox(proj, negf, bsz, seq)

    u_tb = proj[:, OD_U:OD_U + S5_WIDTH].reshape(bsz, seq, S5_WIDTH).transpose(1, 0, 2).reshape(seq * bsz, S5_WIDTH)
    y_tb = _s5scan(u_tb, lam_re, lam_im, log_dt, b_re, b_im, c_re, c_im, d_skip, bsz)
    y_pre = y_tb.reshape(seq, bsz, S5_WIDTH).transpose(1, 0, 2).reshape(bsz * seq, S5_WIDTH)
    y_d = _s5post(y_pre, proj, glu_w, glu_b)

    ow = out_w.astype(BF16)
    return _outproj(y_c, y_d, ow[:FOX_WIDTH], ow[FOX_WIDTH:], x2, gate, post_g.reshape(1, d), seq)


def kernel(x, c, ada_w, ada_b, pre_g, post_g, even_in_w, even_conv_w, even_conv_b, even_dt_bias, even_a_log,
           even_d_skip, even_norm_g, even_out_w, odd_in_w, odd_fgate_b, odd_lam_re, odd_lam_im, odd_log_dt,
           odd_b_re, odd_b_im, odd_c_re, odd_c_im, odd_d_skip, odd_glu_w, odd_glu_b, odd_out_w):
    bsz, seq, d = x.shape
    depth = ada_w.shape[0]
    mod = _ada_mod(c, ada_w, ada_b)
    x2 = x.reshape(bsz * seq, d)
    for layer in range(depth):
        i = layer // 2
        if layer % 2 == 0:
            x2 = _even_layer(x2, mod[layer], pre_g[layer], post_g[layer], even_in_w[i], even_conv_w[i],
                             even_conv_b[i], even_dt_bias[i], even_a_log[i], even_d_skip[i], even_norm_g[i],
                             even_out_w[i], bsz, seq)
        else:
            x2 = _odd_layer(x2, mod[layer], pre_g[layer], post_g[layer], odd_in_w[i], odd_fgate_b[i],
                            odd_lam_re[i], odd_lam_im[i], odd_log_dt[i], odd_b_re[i], odd_b_im[i], odd_c_re[i],
                            odd_c_im[i], odd_d_skip[i], odd_glu_w[i], odd_glu_b[i], odd_out_w[i], bsz, seq)
    return x2.reshape(bsz, seq, d)
```

```python
import functools
import math

import jax
import jax.numpy as jnp
from jax import lax
from jax.experimental import pallas as pl
from jax.experimental.pallas import tpu as pltpu

F32 = jnp.float32
BF16 = jnp.bfloat16

D_MODEL = 1024
HEAD_DIM = 64
D_MIX = 2 * D_MODEL
SSD_WIDTH = 1024
SSD_HEADS = 16
SSD_GROUPS = 2
SSD_STATE = 128
SSD_CONV = 4
SSD_CHUNK = 128
SSD_SUB = 2
MOBA_WIDTH = 1024
MOBA_BLOCK = 256
MOBA_TOPK = 3
FOX_WIDTH = 1536
FOX_HEADS = 24
S5_WIDTH = 512
S5_GROUP = 16
S5_GROUP_SHIFT = 4
S5_GROUPS = 32
S5_STATE = 64
S5_STATE_SHIFT = 6
S5_SETS = 4
S5_STEPS = 64
RMS_EPS = 1e-6
ATT_BLOCK = 256
ATT_BLOCK_SHIFT = 8
LOG2E = math.log2(math.e)
ATT_SCALE = LOG2E / math.sqrt(HEAD_DIM)
VT_ROWS = HEAD_DIM + 16
LANES = 128
NEG = -1e30
VMEM_LIMIT = 48 * 1024 * 1024

EV_ZA, EV_ZB, EV_XS, EV_Q, EV_K, EV_V, EV_BC, EV_N = 0, 1024, 2048, 3072, 4096, 5120, 6144, 6656
OD_ZC, OD_ZD, OD_Q, OD_K, OD_V, OD_U, OD_N = 0, 1536, 2048, 3584, 5120, 6656, 7168


def _nt(a, b):
    return lax.dot_general(a, b, (((1,), (1,)), ((), ())), preferred_element_type=F32)


def _dot(a, b):
    return jnp.dot(a, b, preferred_element_type=F32)


def _split3(x):
    hi = x.astype(BF16)
    r = x - hi.astype(F32)
    mid = r.astype(BF16)
    lo = (r - mid.astype(F32)).astype(BF16)
    return hi, mid, lo


def _dot_exact_rhs(x, m_bf16):
    hi, mid, lo = _split3(x)
    return _dot(hi, m_bf16) + _dot(mid, m_bf16) + _dot(lo, m_bf16)


def _silu(x):
    return x * jax.nn.sigmoid(x)


def _softplus(x):
    return jnp.maximum(x, 0.0) + jnp.log1p(jnp.exp(-jnp.abs(x)))


def _params(sem, limit=VMEM_LIMIT):
    return pltpu.CompilerParams(dimension_semantics=sem, vmem_limit_bytes=limit)


def _ada_kernel(c_ref, w_ref, b_ref, o_ref):
    cond = _silu(c_ref[...])
    hi, mid, lo = _split3(cond)
    w = w_ref[0]
    whi, wmid, wlo = _split3(w)
    acc = _dot(hi, whi) + _dot(hi, wmid) + _dot(mid, whi)
    acc = acc + _dot(hi, wlo) + _dot(mid, wmid) + _dot(lo, whi)
    o_ref[0] = acc + b_ref[0]


def _ada_mod(c, ada_w, ada_b):
    depth, d, d3 = ada_w.shape
    bsz = c.shape[0]
    nj = d3 // d
    return pl.pallas_call(
        _ada_kernel,
        grid=(depth, nj),
        in_specs=[
            pl.BlockSpec((bsz, d), lambda l, j: (0, 0)),
            pl.BlockSpec((1, d, d), lambda l, j: (l, 0, j)),
            pl.BlockSpec((1, 1, d), lambda l, j: (l, 0, j)),
        ],
        out_specs=pl.BlockSpec((1, bsz, d), lambda l, j: (l, 0, j)),
        out_shape=jax.ShapeDtypeStruct((depth, bsz, d3), F32),
        compiler_params=_params(("parallel", "parallel")),
        name="ada_mod",
    )(c, ada_w, ada_b.reshape(depth, 1, d3))


def _inproj_kernel(x_ref, sc_ref, sh_ref, g_ref, w_ref, ws_ref, o_ref, os_ref, h_scr):
    @pl.when(pl.program_id(1) == 0)
    def _():
        x = x_ref[...]
        ms = jnp.mean(x * x, axis=-1, keepdims=True)
        xn = x * lax.rsqrt(ms + RMS_EPS) * g_ref[...]
        h = xn * (1.0 + sc_ref[0]) + sh_ref[0]
        h_b = h.astype(BF16)
        h_scr[...] = h_b
        r = _dot(h_b, ws_ref[...])
        os_ref[...] = r[:, :LANES] + r[:, LANES:]

    o_ref[...] = _dot(h_scr[...], w_ref[...]).astype(BF16)


def _inproj(x2, scale, shift, g, w, ws, seq, tm=1024):
    t, d = x2.shape
    n = w.shape[1]
    tn = n // 2
    per = seq // tm
    return pl.pallas_call(
        _inproj_kernel,
        grid=(t // tm, n // tn),
        in_specs=[
            pl.BlockSpec((tm, d), lambda i, j: (i, 0)),
            pl.BlockSpec((1, 1, d), lambda i, j: (i // per, 0, 0)),
            pl.BlockSpec((1, 1, d), lambda i, j: (i // per, 0, 0)),
            pl.BlockSpec((1, d), lambda i, j: (0, 0)),
            pl.BlockSpec((d, tn), lambda i, j: (0, j)),
            pl.BlockSpec((d, 2 * LANES), lambda i, j: (0, 0)),
        ],
        out_specs=[
            pl.BlockSpec((tm, tn), lambda i, j: (i, j)),
            pl.BlockSpec((tm, LANES), lambda i, j: (i, 0)),
        ],
        out_shape=[
            jax.ShapeDtypeStruct((t, n), BF16),
            jax.ShapeDtypeStruct((t, LANES), F32),
        ],
        scratch_shapes=[pltpu.VMEM((tm, d), BF16)],
        compiler_params=_params(("parallel", "arbitrary")),
        name="in_proj",
    )(x2, scale, shift, g, w, ws)


def _outproj_kernel(a_ref, b_ref, wa_ref, wb_ref, x_ref, gate_ref, pg_ref, o_ref):
    y = _dot(a_ref[...], wa_ref[...]) + _dot(b_ref[...], wb_ref[...])
    ms = jnp.mean(y * y, axis=-1, keepdims=True)
    yn = y * lax.rsqrt(ms + RMS_EPS) * pg_ref[...]
    o_ref[...] = x_ref[...] + gate_ref[0] * yn


def _outproj(a, b, wa, wb, x2, gate, pg, seq, tm=1024):
    t, d = x2.shape
    ka, kb = a.shape[1], b.shape[1]
    per = seq // tm
    return pl.pallas_call(
        _outproj_kernel,
        grid=(t // tm,),
        in_specs=[
            pl.BlockSpec((tm, ka), lambda i: (i, 0)),
            pl.BlockSpec((tm, kb), lambda i: (i, 0)),
            pl.BlockSpec((ka, d), lambda i: (0, 0)),
            pl.BlockSpec((kb, d), lambda i: (0, 0)),
            pl.BlockSpec((tm, d), lambda i: (i, 0)),
            pl.BlockSpec((1, 1, d), lambda i: (i // per, 0, 0)),
            pl.BlockSpec((1, d), lambda i: (0, 0)),
        ],
        out_specs=pl.BlockSpec((tm, d), lambda i: (i, 0)),
        out_shape=jax.ShapeDtypeStruct((t, d), F32),
        compiler_params=_params(("parallel",)),
        name="out_proj",
    )(a, b, wa, wb, x2, gate, pg)


def _causal_conv_silu(raw, prev, shift, w, b):
    lc = raw.shape[0]
    ext = jnp.concatenate([prev, raw], axis=0)
    shifted = _dot(shift, ext)
    acc = b + w[SSD_CONV - 1:SSD_CONV, :] * raw.astype(F32)
    for j in range(SSD_CONV - 1):
        acc = acc + w[j:j + 1, :] * shifted[j * lc:(j + 1) * lc, :]
    return _silu(acc)


def _ssd_kernel(z_ref, xs_ref, bc_ref, xsp_ref, bcp_ref, dt_ref, cwx_ref, cwb_ref, cbx_ref, cbb_ref, dtb_ref,
                alog_ref, dexp_ref, ng_ref, e_ref, sh_ref, o_ref, state_scr):
    lc = SSD_CHUNK
    hw = SSD_HEADS // SSD_GROUPS * HEAD_DIM
    first = pl.program_id(1) == 0

    @pl.when(first)
    def _():
        state_scr[...] = jnp.zeros_like(state_scr)

    rows = lax.broadcasted_iota(jnp.int32, (lc, lc), 0)
    cols = lax.broadcasted_iota(jnp.int32, (lc, lc), 1)
    lower = rows >= cols
    state = [state_scr[:, g * hw:(g + 1) * hw] for g in range(SSD_GROUPS)]
    prev_x = jnp.where(first, jnp.zeros_like(xsp_ref[...]), xsp_ref[...])
    prev_b = jnp.where(first, jnp.zeros_like(bcp_ref[...]), bcp_ref[...])
    for j in range(xs_ref.shape[0] // lc):
        sub = slice(j * lc, (j + 1) * lc)
        raw_x = xs_ref[sub, :]
        raw_b = bc_ref[sub, :]
        y, state = _ssd_chunk(raw_x, raw_b, prev_x, prev_b, dt_ref[sub, :], z_ref[sub, :], state, lower,
                              cwx_ref, cwb_ref, cbx_ref, cbb_ref, dtb_ref, alog_ref, dexp_ref, e_ref, sh_ref)
        ms = jnp.mean(y * y, axis=-1, keepdims=True)
        o_ref[sub, :] = (y * lax.rsqrt(ms + RMS_EPS) * ng_ref[...]).astype(BF16)
        prev_x, prev_b = raw_x, raw_b
    for g in range(SSD_GROUPS):
        state_scr[:, g * hw:(g + 1) * hw] = state[g]


def _ssd_chunk(raw_x, raw_b, prev_x, prev_b, dt_raw, z, state, lower,
               cwx_ref, cwb_ref, cbx_ref, cbb_ref, dtb_ref, alog_ref, dexp_ref, e_ref, sh_ref):
    lc = SSD_CHUNK
    hw = SSD_HEADS // SSD_GROUPS * HEAD_DIM
    xs = _causal_conv_silu(raw_x, prev_x, sh_ref[...], cwx_ref[...], cbx_ref[...])
    bc = _causal_conv_silu(raw_b, prev_b, sh_ref[...], cwb_ref[...], cbb_ref[...])

    dt = _softplus(dt_raw + dtb_ref[...])
    adt = dt * (-jnp.exp(alog_ref[...]))
    a_cum = _dot_exact_rhs_lhs(lower.astype(BF16), adt)
    a_cum_t = a_cum.T
    a_last = a_cum[lc - 1:lc, :]
    stacked = jnp.concatenate([dt, jnp.exp(a_last - a_cum), jnp.exp(a_cum)], axis=0)
    expanded = _dot(stacked.astype(BF16), e_ref[...])
    dt_e = expanded[0:lc]
    dec_e = expanded[lc:2 * lc]
    ea_e = expanded[2 * lc:3 * lc]
    al_e = _dot_exact_rhs(jnp.broadcast_to(jnp.exp(a_last), (8, LANES)), e_ref[...])[0:1]

    xdt = xs * dt_e
    xdt_b = xdt.astype(BF16)
    xd_b = (xdt * dec_e).astype(BF16)
    lane = lax.broadcasted_iota(jnp.int32, (1, LANES), 1)
    pieces = []
    new_state = []
    for g in range(SSD_GROUPS):
        bg = bc[:, g * SSD_STATE:(g + 1) * SSD_STATE]
        cg = bc[:, (SSD_GROUPS + g) * SSD_STATE:(SSD_GROUPS + g + 1) * SSD_STATE]
        cg_b = cg.astype(BF16)
        cb = _nt(cg_b, bg.astype(BF16))
        st = state[g]
        y_off = _dot(cg_b, st.astype(BF16)) * ea_e[:, g * hw:(g + 1) * hw]
        new_st = _dot(bg.T.astype(BF16), xd_b[:, g * hw:(g + 1) * hw])
        new_state.append(al_e[:, g * hw:(g + 1) * hw] * st + new_st)
        for pair in range(SSD_HEADS // SSD_GROUPS // 2):
            acc = None
            c0 = g * hw + pair * LANES
            xp = xdt_b[:, c0:c0 + LANES]
            for half in range(2):
                h = g * (SSD_HEADS // SSD_GROUPS) + pair * 2 + half
                diff = a_cum[:, h:h + 1] - a_cum_t[h:h + 1, :]
                decay = jnp.exp(jnp.where(lower, diff, NEG))
                m = (cb * decay).astype(BF16)
                in_half = (lane >= half * HEAD_DIM) & (lane < (half + 1) * HEAD_DIM)
                part = _dot(m, jnp.where(in_half, xp, jnp.zeros_like(xp)))
                acc = part if acc is None else acc + part
            pieces.append(acc + y_off[:, pair * LANES:(pair + 1) * LANES])
    y = jnp.concatenate(pieces, axis=1) + dexp_ref[...] * xs
    return y * _silu(z.astype(F32)), new_state


def _dot_exact_rhs_lhs(m_bf16, x):
    hi, mid, lo = _split3(x)
    return _dot(m_bf16, hi) + _dot(m_bf16, mid) + _dot(m_bf16, lo)


def _ssd(proj, small, conv_w, conv_b, dt_bias, a_log, d_skip, norm_g, bsz, seq):
    lc = SSD_CHUNK
    nc = seq // lc
    t = bsz * seq
    pad = LANES - SSD_HEADS
    dtb = jnp.pad(dt_bias, (0, pad)).reshape(1, LANES)
    alog = jnp.pad(a_log, (0, pad)).reshape(1, LANES)
    dexp = jnp.repeat(d_skip, HEAD_DIM).reshape(1, SSD_WIDTH)
    expand = (jnp.arange(LANES)[:, None] == (jnp.arange(SSD_WIDTH) // HEAD_DIM)[None, :]).astype(BF16)
    nbc = 2 * SSD_GROUPS * SSD_STATE
    src = lc + jnp.arange(lc)[None, :] - (SSD_CONV - 1) + jnp.arange(SSD_CONV - 1)[:, None]
    shift = (src.reshape(-1, 1) == jnp.arange(2 * lc)[None, :]).astype(BF16)
    ns = nc // SSD_SUB
    rs = SSD_SUB * lc
    row = lambda b, c: b * ns + c
    prev = lambda b, c: b * nc + jnp.maximum(SSD_SUB * c - 1, 0)
    const = lambda b, c: (0, 0)
    return pl.pallas_call(
        _ssd_kernel,
        grid=(bsz, ns),
        in_specs=[
            pl.BlockSpec((rs, SSD_WIDTH), lambda b, c: (row(b, c), EV_ZA // SSD_WIDTH)),
            pl.BlockSpec((rs, SSD_WIDTH), lambda b, c: (row(b, c), EV_XS // SSD_WIDTH)),
            pl.BlockSpec((rs, nbc), lambda b, c: (row(b, c), EV_BC // nbc)),
            pl.BlockSpec((lc, SSD_WIDTH), lambda b, c: (prev(b, c), EV_XS // SSD_WIDTH)),
            pl.BlockSpec((lc, nbc), lambda b, c: (prev(b, c), EV_BC // nbc)),
            pl.BlockSpec((rs, LANES), lambda b, c: (row(b, c), 0)),
            pl.BlockSpec((SSD_CONV, SSD_WIDTH), const),
            pl.BlockSpec((SSD_CONV, nbc), const),
            pl.BlockSpec((1, SSD_WIDTH), const),
            pl.BlockSpec((1, nbc), const),
            pl.BlockSpec((1, LANES), const),
            pl.BlockSpec((1, LANES), const),
            pl.BlockSpec((1, SSD_WIDTH), const),
            pl.BlockSpec((1, SSD_WIDTH), const),
            pl.BlockSpec((LANES, SSD_WIDTH), const),
            pl.BlockSpec(((SSD_CONV - 1) * lc, 2 * lc), const),
        ],
        out_specs=pl.BlockSpec((rs, SSD_WIDTH), lambda b, c: (row(b, c), 0)),
        out_shape=jax.ShapeDtypeStruct((t, SSD_WIDTH), BF16),
        scratch_shapes=[pltpu.VMEM((SSD_STATE, SSD_WIDTH), F32)],
        compiler_params=_params(("parallel", "arbitrary")),
        name="ssd",
    )(proj, proj, proj, proj, proj, small,
      conv_w[:, :SSD_WIDTH], conv_w[:, SSD_WIDTH:], conv_b[:SSD_WIDTH].reshape(1, -1),
      conv_b[SSD_WIDTH:].reshape(1, -1), dtb, alog, dexp, norm_g.reshape(1, -1), expand, shift)


def _fold_rows(x, op):
    out = x[0:8, :]
    for i in range(1, x.shape[0] // 8):
        out = op(out, x[8 * i:8 * (i + 1), :])
    return out


def _attend_pair(qa_scr, ka_scr, vt_scr, s_scr, p_scr, ot_scr, bias_fn):
    blk = ATT_BLOCK
    nb = qa_scr.shape[1] // blk
    heads = (0, 1)
    causal = (lax.broadcasted_iota(jnp.int32, (blk, blk), 0)
              <= lax.broadcasted_iota(jnp.int32, (blk, blk), 1))

    def score_tile(half, qb, n, m8):
        s = _nt(ka_scr[half, n * blk:(n + 1) * blk, :], qa_scr[half, qb * blk:(qb + 1) * blk, :])
        bias = bias_fn(half, qb, n)
        if bias is not None:
            s = s + bias
        if n == qb:
            s = jnp.where(causal, s, NEG)
        s_scr[half, qb % 2, n] = s
        part = _fold_rows(s, jnp.maximum)
        return part if m8 is None else jnp.maximum(m8, part)

    def prob_tile(half, qb, n, m):
        p = jnp.exp2(s_scr[half, qb % 2, n] - m)
        p_scr[half, qb % 2, n * blk:(n + 1) * blk, :] = p.astype(BF16)

    m8 = [score_tile(half, 0, 0, None) for half in heads]
    for qb in range(nb):
        m = [jnp.max(m8[half], axis=0, keepdims=True) for half in heads]
        nxt = qb + 1
        m8 = [None, None]
        for n in range(nxt + 1):
            for half in heads:
                if nxt < nb:
                    m8[half] = score_tile(half, nxt, n, m8[half])
            for half in heads:
                if n <= qb:
                    prob_tile(half, qb, n, m[half])
        keys = (qb + 1) * blk
        for half in heads:
            acc = _dot(vt_scr[half, :, 0:keys], p_scr[half, qb % 2, 0:keys, :])
            out = acc[0:HEAD_DIM] / acc[HEAD_DIM:HEAD_DIM + 1]
            ot_scr[half * HEAD_DIM:(half + 1) * HEAD_DIM, qb * blk:(qb + 1) * blk] = out


def _store_vt(v_ref, vt_scr):
    ones = jnp.ones((VT_ROWS - HEAD_DIM, ATT_BLOCK), BF16)
    for n in range(v_ref.shape[0] // ATT_BLOCK):
        cols = slice(n * ATT_BLOCK, (n + 1) * ATT_BLOCK)
        vt = v_ref[cols, :].astype(F32).T.astype(BF16)
        for half in range(2):
            vt_scr[half, 0:HEAD_DIM, cols] = vt[half * HEAD_DIM:(half + 1) * HEAD_DIM, :]
            vt_scr[half, HEAD_DIM:VT_ROWS, cols] = ones


def _gated_output(ot_scr, z_ref, o_ref):
    for i in range(ot_scr.shape[1] // ATT_BLOCK):
        rows = slice(i * ATT_BLOCK, (i + 1) * ATT_BLOCK)
        o = ot_scr[:, rows].T
        o_ref[rows, :] = (o * _silu(z_ref[rows, :].astype(F32))).astype(BF16)


def _head_lanes(half):
    lane = lax.broadcasted_iota(jnp.int32, (1, LANES), 1)
    return (lane >= half * HEAD_DIM) & (lane < (half + 1) * HEAD_DIM)


def _moba_kernel(q_ref, k_ref, v_ref, z_ref, o_ref, vt_scr, kbar_scr, bias_scr, qa_scr, ka_scr, s_scr, p_scr,
                 ot_scr):
    blk = ATT_BLOCK
    seq = q_ref.shape[0]
    nb = seq // blk
    _store_vt(v_ref, vt_scr)
    for n in range(nb):
        kb = k_ref[n * blk:(n + 1) * blk, :].astype(F32)
        kbar_scr[n:n + 1, :] = jnp.mean(kb, axis=0, keepdims=True)
    n_idx = lax.broadcasted_iota(jnp.int32, (nb, seq), 0)
    q_blk = lax.broadcasted_iota(jnp.int32, (nb, seq), 1) >> ATT_BLOCK_SHIFT

    for half in range(2):
        in_half = _head_lanes(half)
        q = q_ref[...]
        qi = jnp.where(in_half, q, jnp.zeros_like(q))
        qa_scr[half] = qi
        ka_scr[half] = k_ref[...]
        kbar = jnp.where(in_half, kbar_scr[...], 0.0)
        kb_hi = kbar.astype(BF16)
        kb_lo = (kbar - kb_hi.astype(F32)).astype(BF16)
        gate = _nt(kb_hi, qi) + _nt(kb_lo, qi)
        rank = jnp.zeros((nb, seq), F32)
        for mth in range(nb):
            gm = gate[mth:mth + 1, :]
            beats = (gm > gate) | ((gm == gate) & (mth < n_idx))
            rank = rank + jnp.where(beats & (mth < q_blk), 1.0, 0.0)
        chosen = (rank < float(MOBA_TOPK)) & (n_idx < q_blk)
        bias_scr[half] = jnp.where(chosen, 0.0, NEG)

    def bias_fn(half, qb, n):
        if n == qb or qb <= MOBA_TOPK:
            return None
        return bias_scr[half, n:n + 1, qb * blk:(qb + 1) * blk]

    _attend_pair(qa_scr, ka_scr, vt_scr, s_scr, p_scr, ot_scr, bias_fn)
    _gated_output(ot_scr, z_ref, o_ref)


def _moba(proj, bsz, seq):
    blk = ATT_BLOCK
    nb = seq // blk
    pairs = MOBA_WIDTH // LANES
    t = bsz * seq
    return pl.pallas_call(
        _moba_kernel,
        grid=(bsz, pairs),
        in_specs=[
            pl.BlockSpec((seq, LANES), lambda b, p: (b, EV_Q // LANES + p)),
            pl.BlockSpec((seq, LANES), lambda b, p: (b, EV_K // LANES + p)),
            pl.BlockSpec((seq, LANES), lambda b, p: (b, EV_V // LANES + p)),
            pl.BlockSpec((seq, LANES), lambda b, p: (b, EV_ZB // LANES + p)),
        ],
        out_specs=pl.BlockSpec((seq, LANES), lambda b, p: (b, p)),
        out_shape=jax.ShapeDtypeStruct((t, MOBA_WIDTH), BF16),
        scratch_shapes=[
            pltpu.VMEM((2, VT_ROWS, seq), BF16),
            pltpu.VMEM((nb, LANES), F32),
            pltpu.VMEM((2, nb, seq), F32),
            pltpu.VMEM((2, seq, LANES), BF16),
            pltpu.VMEM((2, seq, LANES), BF16),
            pltpu.VMEM((2, 2, nb, blk, blk), F32),
            pltpu.VMEM((2, 2, seq, blk), BF16),
            pltpu.VMEM((LANES, seq), F32),
        ],
        compiler_params=_params(("parallel", "parallel")),
        name="moba",
    )(proj, proj, proj, proj)


def _fgate_kernel(f_ref, fb_ref, o_ref):
    blk = ATT_BLOCK
    rows = lax.broadcasted_iota(jnp.int32, (blk, blk), 0)
    cols = lax.broadcasted_iota(jnp.int32, (blk, blk), 1)
    lower = (rows >= cols).astype(BF16)
    carry = jnp.zeros((1, LANES), F32)
    for i in range(f_ref.shape[0] // blk):
        nlf = _softplus(-(f_ref[i * blk:(i + 1) * blk, :] + fb_ref[...]))
        csum = _dot_exact_rhs_lhs(lower, nlf) + carry
        o_ref[i * blk:(i + 1) * blk, :] = csum
        carry = csum[blk - 1:blk, :]


def _fgate(small, fgate_b, bsz, seq):
    fb = jnp.pad(fgate_b, (0, LANES - FOX_HEADS)).reshape(1, LANES)
    return pl.pallas_call(
        _fgate_kernel,
        grid=(bsz,),
        in_specs=[
            pl.BlockSpec((seq, LANES), lambda b: (b, 0)),
            pl.BlockSpec((1, LANES), lambda b: (0, 0)),
        ],
        out_specs=pl.BlockSpec((seq, LANES), lambda b: (b, 0)),
        out_shape=jax.ShapeDtypeStruct((bsz * seq, LANES), F32),
        compiler_params=_params(("parallel",)),
        name="fox_gate",
    )(small, fb)


def _fox_kernel(q_ref, k_ref, v_ref, z_ref, nf_ref, o_ref, vt_scr, fb_scr, qa_scr, ka_scr, s_scr, p_scr, ot_scr):
    pair = pl.program_id(1)
    nb = q_ref.shape[0] // ATT_BLOCK
    blk = ATT_BLOCK
    _store_vt(v_ref, vt_scr)
    lane = lax.broadcasted_iota(jnp.int32, (1, LANES), 1)
    for half in range(2):
        own_lane = lane == pair * 2 + half
        for n in range(nb):
            col = jnp.sum(jnp.where(own_lane, nf_ref[n * blk:(n + 1) * blk, :], 0.0), axis=1, keepdims=True)
            fb_scr[half, n] = jnp.broadcast_to(col * LOG2E, (blk, LANES))
        q = q_ref[...]
        qa_scr[half] = jnp.where(_head_lanes(half), q, jnp.zeros_like(q))
        ka_scr[half] = k_ref[...]

    def bias_fn(half, qb, n):
        fb = fb_scr[half, n]
        return jnp.concatenate([fb, fb], axis=1)

    _attend_pair(qa_scr, ka_scr, vt_scr, s_scr, p_scr, ot_scr, bias_fn)
    _gated_output(ot_scr, z_ref, o_ref)


def _fox(proj, negf, bsz, seq):
    blk = ATT_BLOCK
    nb = seq // blk
    pairs = FOX_WIDTH // LANES
    t = bsz * seq
    return pl.pallas_call(
        _fox_kernel,
        grid=(bsz, pairs),
        in_specs=[
            pl.BlockSpec((seq, LANES), lambda b, p: (b, OD_Q // LANES + p)),
            pl.BlockSpec((seq, LANES), lambda b, p: (b, OD_K // LANES + p)),
            pl.BlockSpec((seq, LANES), lambda b, p: (b, OD_V // LANES + p)),
            pl.BlockSpec((seq, LANES), lambda b, p: (b, OD_ZC // LANES + p)),
            pl.BlockSpec((seq, LANES), lambda b, p: (b, 0)),
        ],
        out_specs=pl.BlockSpec((seq, LANES), lambda b, p: (b, p)),
        out_shape=jax.ShapeDtypeStruct((t, FOX_WIDTH), BF16),
        scratch_shapes=[
            pltpu.VMEM((2, VT_ROWS, seq), BF16),
            pltpu.VMEM((2, nb, blk, LANES), F32),
            pltpu.VMEM((2, seq, LANES), BF16),
            pltpu.VMEM((2, seq, LANES), BF16),
            pltpu.VMEM((2, 2, nb, blk, blk), F32),
            pltpu.VMEM((2, 2, seq, blk), BF16),
            pltpu.VMEM((LANES, seq), F32),
        ],
        compiler_params=_params(("parallel", "parallel")),
        name="fox",
    )(proj, proj, proj, proj, negf)


def _s5scan_kernel(u_ref, perm_ref, permt_ref, lr_ref, li_ref, ldt_ref, bwr_ref, bwi_ref, cwr_ref, cwi_ref, d_ref,
                   y_ref, bre_scr, bim_scr, cre_scr, cim_scr, ar_scr, ai_scr, xr_scr, xi_scr, zr_scr, zi_scr,
                   *, bsz):
    nset, cw, sw = bre_scr.shape

    @pl.when(pl.program_id(0) == 0)
    def _():
        chan_grp = lax.broadcasted_iota(jnp.int32, (cw, sw), 0) >> S5_GROUP_SHIFT
        state_grp = lax.broadcasted_iota(jnp.int32, (cw, sw), 1) >> S5_STATE_SHIFT
        same_b = chan_grp == state_grp
        same_c = ((lax.broadcasted_iota(jnp.int32, (sw, cw), 0) >> S5_STATE_SHIFT)
                  == (lax.broadcasted_iota(jnp.int32, (sw, cw), 1) >> S5_GROUP_SHIFT))
        for s in range(nset):
            lr = lr_ref[s]
            li = li_ref[s]
            dt = jnp.exp(ldt_ref[s])
            mag = jnp.exp(lr * dt)
            ar = mag * jnp.cos(li * dt)
            ai = mag * jnp.sin(li * dt)
            den = lr * lr + li * li
            qr = ((ar - 1.0) * lr + ai * li) / den
            qi = (ai * lr - (ar - 1.0) * li) / den
            ar_scr[s] = jnp.broadcast_to(ar, (bsz, sw))
            ai_scr[s] = jnp.broadcast_to(ai, (bsz, sw))
            bwr = bwr_ref[s]
            bwi = bwi_ref[s]
            bre_scr[s] = jnp.where(same_b, qr * bwr - qi * bwi, 0.0).astype(BF16)
            bim_scr[s] = jnp.where(same_b, qr * bwi + qi * bwr, 0.0).astype(BF16)
            cre_scr[s] = jnp.where(same_c, cwr_ref[s], 0.0).astype(BF16)
            cim_scr[s] = jnp.where(same_c, cwi_ref[s], 0.0).astype(BF16)
        xr_scr[...] = jnp.zeros_like(xr_scr)
        xi_scr[...] = jnp.zeros_like(xi_scr)

    steps = u_ref.shape[1]
    width = u_ref.shape[2]
    u_tb = _dot(perm_ref[...], u_ref[...].reshape(bsz * steps, width))
    u_b = u_tb.astype(BF16)
    half_rows = (steps // 2) * bsz

    def project(half, s):
        rows = slice(half * half_rows, (half + 1) * half_rows)
        us = u_b[rows, s * cw:(s + 1) * cw]
        zr_scr[s, rows, :] = _dot(us, bre_scr[s])
        zi_scr[s, rows, :] = _dot(us, bim_scr[s])

    for s in range(nset):
        project(0, s)

    def step(t, carry):
        r0 = t * bsz
        out = []
        for s in range(nset):
            xr, xi = carry[2 * s], carry[2 * s + 1]
            ar = ar_scr[s]
            ai = ai_scr[s]
            nr = ar * xr - ai * xi + zr_scr[s, r0:r0 + bsz, :]
            ni = ar * xi + ai * xr + zi_scr[s, r0:r0 + bsz, :]
            zr_scr[s, r0:r0 + bsz, :] = nr
            zi_scr[s, r0:r0 + bsz, :] = ni
            out += [nr, ni]
        return out

    def readout(half, s):
        rows = slice(half * half_rows, (half + 1) * half_rows)
        return _dot(zr_scr[s, rows, :].astype(BF16), cre_scr[s]) - _dot(zi_scr[s, rows, :].astype(BF16), cim_scr[s])

    every = steps // 2 // nset
    carry = []
    for s in range(nset):
        carry += [xr_scr[s], xi_scr[s]]
    for t in range(steps // 2):
        carry = step(t, carry)
        if t % every == 0:
            project(1, t // every)
    first_half = []
    for t in range(steps // 2, steps):
        carry = step(t, carry)
        k = t - steps // 2
        if k % every == 0:
            first_half.append(readout(0, k // every))
    for s in range(nset):
        xr_scr[s] = carry[2 * s]
        xi_scr[s] = carry[2 * s + 1]
    second_half = [readout(1, s) for s in range(nset)]
    xc = jnp.concatenate([jnp.concatenate(first_half, axis=1), jnp.concatenate(second_half, axis=1)], axis=0)
    y = (xc + d_ref[...] * u_tb).astype(BF16)
    y_bt = _dot(permt_ref[...], y).astype(BF16)
    y_ref[...] = y_bt.reshape(bsz, steps, width)


def _s5scan(proj3, lam_re, lam_im, log_dt, b_re, b_im, c_re, c_im, d_skip):
    bsz, seq, _ = proj3.shape
    width = S5_WIDTH
    nset = S5_SETS
    cw = width // nset
    gs = S5_GROUPS // nset
    sw = gs * S5_STATE
    blk = S5_STEPS * bsz
    tb = jnp.arange(blk)
    perm = ((tb % bsz) * S5_STEPS + tb // bsz)[:, None] == jnp.arange(blk)[None, :]
    vec = lambda a: a.reshape(nset, 1, sw)
    bw = lambda b: jnp.tile(jnp.swapaxes(b, 1, 2).reshape(nset, cw, S5_STATE), (1, 1, gs))
    cw_t = lambda c: jnp.tile(
        jnp.swapaxes(c, 1, 2).reshape(nset, gs, S5_STATE, S5_GROUP).transpose(0, 2, 1, 3).reshape(nset, S5_STATE, cw),
        (1, gs, 1))
    full3 = lambda a, b, c: pl.BlockSpec((a, b, c), lambda i: (0, 0, 0))
    return pl.pallas_call(
        functools.partial(_s5scan_kernel, bsz=bsz),
        grid=(seq // S5_STEPS,),
        in_specs=[
            pl.BlockSpec((bsz, S5_STEPS, width), lambda i: (0, i, OD_U // width)),
            pl.BlockSpec((blk, blk), lambda i: (0, 0)),
            pl.BlockSpec((blk, blk), lambda i: (0, 0)),
            full3(nset, 1, sw), full3(nset, 1, sw), full3(nset, 1, sw),
            full3(nset, cw, sw), full3(nset, cw, sw),
            full3(nset, sw, cw), full3(nset, sw, cw),
            pl.BlockSpec((1, width), lambda i: (0, 0)),
        ],
        out_specs=pl.BlockSpec((bsz, S5_STEPS, width), lambda i: (0, i, 0)),
        out_shape=jax.ShapeDtypeStruct((bsz, seq, width), BF16),
        scratch_shapes=[
            pltpu.VMEM((nset, cw, sw), BF16), pltpu.VMEM((nset, cw, sw), BF16),
            pltpu.VMEM((nset, sw, cw), BF16), pltpu.VMEM((nset, sw, cw), BF16),
            pltpu.VMEM((nset, bsz, sw), F32), pltpu.VMEM((nset, bsz, sw), F32),
            pltpu.VMEM((nset, bsz, sw), F32), pltpu.VMEM((nset, bsz, sw), F32),
            pltpu.VMEM((nset, blk, sw), F32), pltpu.VMEM((nset, blk, sw), F32),
        ],
        compiler_params=_params(("arbitrary",)),
        name="s5_scan",
    )(proj3, perm.astype(BF16), perm.T.astype(BF16), vec(lam_re), vec(lam_im), vec(jnp.repeat(log_dt, S5_STATE)),
      bw(b_re), bw(b_im), cw_t(c_re), cw_t(c_im), d_skip.reshape(1, width))


def _s5post_kernel(y_ref, z_ref, gw_ref, gb_ref, o_ref):
    y = y_ref[...].astype(F32)
    y = 0.5 * y * (1.0 + jnp.tanh(math.sqrt(2.0 / math.pi) * (y + 0.044715 * (y * y * y))))
    y = y * jax.nn.sigmoid(_dot(y.astype(BF16), gw_ref[...]) + gb_ref[...])
    o_ref[...] = (y * _silu(z_ref[...].astype(F32))).astype(BF16)


def _s5post(y_pre, proj, glu_w, glu_b, tm=1024):
    t, w = y_pre.shape
    return pl.pallas_call(
        _s5post_kernel,
        grid=(t // tm,),
        in_specs=[
            pl.BlockSpec((tm, w), lambda i: (i, 0)),
            pl.BlockSpec((tm, w), lambda i: (i, OD_ZD // S5_WIDTH)),
            pl.BlockSpec((w, w), lambda i: (0, 0)),
            pl.BlockSpec((1, w), lambda i: (0, 0)),
        ],
        out_specs=pl.BlockSpec((tm, w), lambda i: (i, 0)),
        out_shape=jax.ShapeDtypeStruct((t, w), BF16),
        compiler_params=_params(("parallel",)),
        name="s5_post",
    )(y_pre, proj, glu_w.astype(BF16), glu_b.reshape(1, w))


def _small_weights(cols):
    padded = jnp.pad(cols, ((0, 0), (0, LANES - cols.shape[1])))
    hi = padded.astype(BF16)
    lo = (padded - hi.astype(F32)).astype(BF16)
    return jnp.concatenate([hi, lo], axis=1)


def _even_layer(x2, mod, pre_g, post_g, in_w, conv_w, conv_b, dt_bias, a_log, d_skip, norm_g, out_w, bsz, seq):
    d = D_MODEL
    shift, scale, gate = (mod[:, i * d:(i + 1) * d].reshape(bsz, 1, d) for i in range(3))
    o_xbc = 2 * SSD_WIDTH
    o_dt = o_xbc + SSD_WIDTH + 2 * SSD_GROUPS * SSD_STATE
    o_q = o_dt + SSD_HEADS
    w = jnp.concatenate([
        in_w[:, :o_xbc],
        in_w[:, o_xbc:o_xbc + SSD_WIDTH],
        in_w[:, o_q:o_q + MOBA_WIDTH] * ATT_SCALE,
        in_w[:, o_q + MOBA_WIDTH:],
        in_w[:, o_xbc + SSD_WIDTH:o_dt],
    ], axis=1).astype(BF16)
    ws = _small_weights(in_w[:, o_dt:o_q])
    proj, small = _inproj(x2, scale, shift, pre_g.reshape(1, d), w, ws, seq)
    y_a = _ssd(proj, small, conv_w, conv_b, dt_bias, a_log, d_skip, norm_g, bsz, seq)
    y_b = _moba(proj, bsz, seq)
    ow = out_w.astype(BF16)
    return _outproj(y_a, y_b, ow[:SSD_WIDTH], ow[SSD_WIDTH:], x2, gate, post_g.reshape(1, d), seq)


def _odd_layer(x2, mod, pre_g, post_g, in_w, fgate_b, lam_re, lam_im, log_dt, b_re, b_im, c_re, c_im,
               d_skip, glu_w, glu_b, out_w, bsz, seq):
    d = D_MODEL
    shift, scale, gate = (mod[:, i * d:(i + 1) * d].reshape(bsz, 1, d) for i in range(3))
    o_f = D_MIX + 3 * FOX_WIDTH
    o_u = o_f + FOX_HEADS
    w = jnp.concatenate([
        in_w[:, :D_MIX],
        in_w[:, D_MIX:D_MIX + FOX_WIDTH] * ATT_SCALE,
        in_w[:, D_MIX + FOX_WIDTH:o_f],
        in_w[:, o_u:],
    ], axis=1).astype(BF16)
    ws = _small_weights(in_w[:, o_f:o_u])
    proj, small = _inproj(x2, scale, shift, pre_g.reshape(1, d), w, ws, seq)

    negf = _fgate(small, fgate_b, bsz, seq)
    y_c = _fox(proj, negf, bsz, seq)

    y_pre = _s5scan(proj.reshape(bsz, seq, OD_N), lam_re, lam_im, log_dt, b_re, b_im, c_re, c_im, d_skip)
    y_d = _s5post(y_pre.reshape(bsz * seq, S5_WIDTH), proj, glu_w, glu_b)

    ow = out_w.astype(BF16)
    return _outproj(y_c, y_d, ow[:FOX_WIDTH], ow[FOX_WIDTH:], x2, gate, post_g.reshape(1, d), seq)


def kernel(x, c, ada_w, ada_b, pre_g, post_g, even_in_w, even_conv_w, even_conv_b, even_dt_bias, even_a_log,
           even_d_skip, even_norm_g, even_out_w, odd_in_w, odd_fgate_b, odd_lam_re, odd_lam_im, odd_log_dt,
           odd_b_re, odd_b_im, odd_c_re, odd_c_im, odd_d_skip, odd_glu_w, odd_glu_b, odd_out_w):
    bsz, seq, d = x.shape
    depth = ada_w.shape[0]
    mod = _ada_mod(c, ada_w, ada_b)
    x2 = x.reshape(bsz * seq, d)
    for layer in range(depth):
        i = layer // 2
        if layer % 2 == 0:
            x2 = _even_layer(x2, mod[layer], pre_g[layer], post_g[layer], even_in_w[i], even_conv_w[i],
                             even_conv_b[i], even_dt_bias[i], even_a_log[i], even_d_skip[i], even_norm_g[i],
                             even_out_w[i], bsz, seq)
        else:
            x2 = _odd_layer(x2, mod[layer], pre_g[layer], post_g[layer], odd_in_w[i], odd_fgate_b[i],
                            odd_lam_re[i], odd_lam_im[i], odd_log_dt[i], odd_b_re[i], odd_b_im[i], odd_c_re[i],
                            odd_c_im[i], odd_d_skip[i], odd_glu_w[i], odd_glu_b[i], odd_out_w[i], bsz, seq)
    return x2.reshape(bsz, seq, d)
```

```python
import functools
import math

import jax
import jax.numpy as jnp
from jax import lax
from jax.experimental import pallas as pl
from jax.experimental.pallas import tpu as pltpu

F32 = jnp.float32
BF16 = jnp.bfloat16

D_MODEL = 1024
HEAD_DIM = 64
D_MIX = 2 * D_MODEL
SSD_WIDTH = 1024
SSD_HEADS = 16
SSD_GROUPS = 2
SSD_STATE = 128
SSD_CONV = 4
SSD_CHUNK = 128
SSD_SUB = 2
MOBA_WIDTH = 1024
MOBA_BLOCK = 256
MOBA_TOPK = 3
FOX_WIDTH = 1536
FOX_HEADS = 24
S5_WIDTH = 512
S5_GROUP = 16
S5_GROUP_SHIFT = 4
S5_GROUPS = 32
S5_STATE = 64
S5_STATE_SHIFT = 6
S5_SETS = 4
S5_STEPS = 64
RMS_EPS = 1e-6
ATT_BLOCK = 256
ATT_BLOCK_SHIFT = 8
LOG2E = math.log2(math.e)
ATT_SCALE = LOG2E / math.sqrt(HEAD_DIM)
VT_ROWS = HEAD_DIM + 16
LANES = 128
NEG = -1e30
VMEM_LIMIT = 48 * 1024 * 1024

EV_ZA, EV_ZB, EV_XS, EV_Q, EV_K, EV_V, EV_BC, EV_N = 0, 1024, 2048, 3072, 4096, 5120, 6144, 6656
OD_ZC, OD_ZD, OD_Q, OD_K, OD_V, OD_U, OD_N = 0, 1536, 2048, 3584, 5120, 6656, 7168


def _nt(a, b):
    return lax.dot_general(a, b, (((1,), (1,)), ((), ())), preferred_element_type=F32)


def _dot(a, b):
    return jnp.dot(a, b, preferred_element_type=F32)


def _split3(x):
    hi = x.astype(BF16)
    r = x - hi.astype(F32)
    mid = r.astype(BF16)
    lo = (r - mid.astype(F32)).astype(BF16)
    return hi, mid, lo


def _dot_exact_rhs(x, m_bf16):
    hi, mid, lo = _split3(x)
    return _dot(hi, m_bf16) + _dot(mid, m_bf16) + _dot(lo, m_bf16)


def _silu(x):
    return x * jax.nn.sigmoid(x)


def _softplus(x):
    return jnp.maximum(x, 0.0) + jnp.log1p(jnp.exp(-jnp.abs(x)))


def _params(sem, limit=VMEM_LIMIT):
    return pltpu.CompilerParams(dimension_semantics=sem, vmem_limit_bytes=limit)


def _ada_kernel(c_ref, w_ref, b_ref, o_ref):
    cond = _silu(c_ref[...])
    hi, mid, lo = _split3(cond)
    w = w_ref[0]
    whi, wmid, wlo = _split3(w)
    acc = _dot(hi, whi) + _dot(hi, wmid) + _dot(mid, whi)
    acc = acc + _dot(hi, wlo) + _dot(mid, wmid) + _dot(lo, whi)
    o_ref[0] = acc + b_ref[0]


def _ada_mod(c, ada_w, ada_b):
    depth, d, d3 = ada_w.shape
    bsz = c.shape[0]
    nj = d3 // d
    return pl.pallas_call(
        _ada_kernel,
        grid=(depth, nj),
        in_specs=[
            pl.BlockSpec((bsz, d), lambda l, j: (0, 0)),
            pl.BlockSpec((1, d, d), lambda l, j: (l, 0, j)),
            pl.BlockSpec((1, 1, d), lambda l, j: (l, 0, j)),
        ],
        out_specs=pl.BlockSpec((1, bsz, d), lambda l, j: (l, 0, j)),
        out_shape=jax.ShapeDtypeStruct((depth, bsz, d3), F32),
        compiler_params=_params(("parallel", "parallel")),
        name="ada_mod",
    )(c, ada_w, ada_b.reshape(depth, 1, d3))


def _inproj_kernel(x_ref, sc_ref, sh_ref, g_ref, w_ref, ws_ref, o_ref, os_ref, h_scr):
    @pl.when(pl.program_id(1) == 0)
    def _():
        x = x_ref[...]
        ms = jnp.mean(x * x, axis=-1, keepdims=True)
        xn = x * lax.rsqrt(ms + RMS_EPS) * g_ref[...]
        h = xn * (1.0 + sc_ref[0]) + sh_ref[0]
        h_b = h.astype(BF16)
        h_scr[...] = h_b
        r = _nt(h_b, ws_ref[...])
        os_ref[...] = r[:, :LANES] + r[:, LANES:]

    o_ref[...] = _nt(h_scr[...], w_ref[...]).astype(BF16)


def _inproj(x2, scale, shift, g, w, ws, seq, tm=1024):
    t, d = x2.shape
    n = w.shape[0]
    tn = n // 2
    per = seq // tm
    return pl.pallas_call(
        _inproj_kernel,
        grid=(t // tm, n // tn),
        in_specs=[
            pl.BlockSpec((tm, d), lambda i, j: (i, 0)),
            pl.BlockSpec((1, 1, d), lambda i, j: (i // per, 0, 0)),
            pl.BlockSpec((1, 1, d), lambda i, j: (i // per, 0, 0)),
            pl.BlockSpec((1, d), lambda i, j: (0, 0)),
            pl.BlockSpec((tn, d), lambda i, j: (j, 0)),
            pl.BlockSpec((2 * LANES, d), lambda i, j: (0, 0)),
        ],
        out_specs=[
            pl.BlockSpec((tm, tn), lambda i, j: (i, j)),
            pl.BlockSpec((tm, LANES), lambda i, j: (i, 0)),
        ],
        out_shape=[
            jax.ShapeDtypeStruct((t, n), BF16),
            jax.ShapeDtypeStruct((t, LANES), F32),
        ],
        scratch_shapes=[pltpu.VMEM((tm, d), BF16)],
        compiler_params=_params(("parallel", "arbitrary")),
        name="in_proj",
    )(x2, scale, shift, g, w, ws)


def _outproj_kernel(a_ref, b_ref, wa_ref, wb_ref, x_ref, gate_ref, pg_ref, o_ref):
    y = _dot(a_ref[...], wa_ref[...]) + _dot(b_ref[...], wb_ref[...])
    ms = jnp.mean(y * y, axis=-1, keepdims=True)
    yn = y * lax.rsqrt(ms + RMS_EPS) * pg_ref[...]
    o_ref[...] = x_ref[...] + gate_ref[0] * yn


def _outproj(a, b, wa, wb, x2, gate, pg, seq, tm=1024):
    t, d = x2.shape
    ka, kb = a.shape[1], b.shape[1]
    per = seq // tm
    return pl.pallas_call(
        _outproj_kernel,
        grid=(t // tm,),
        in_specs=[
            pl.BlockSpec((tm, ka), lambda i: (i, 0)),
            pl.BlockSpec((tm, kb), lambda i: (i, 0)),
            pl.BlockSpec((ka, d), lambda i: (0, 0)),
            pl.BlockSpec((kb, d), lambda i: (0, 0)),
            pl.BlockSpec((tm, d), lambda i: (i, 0)),
            pl.BlockSpec((1, 1, d), lambda i: (i // per, 0, 0)),
            pl.BlockSpec((1, d), lambda i: (0, 0)),
        ],
        out_specs=pl.BlockSpec((tm, d), lambda i: (i, 0)),
        out_shape=jax.ShapeDtypeStruct((t, d), F32),
        compiler_params=_params(("parallel",)),
        name="out_proj",
    )(a, b, wa, wb, x2, gate, pg)


def _causal_conv_silu(raw, prev, shift, w, b):
    lc = raw.shape[0]
    ext = jnp.concatenate([prev, raw], axis=0)
    shifted = _dot(shift, ext)
    acc = b + w[SSD_CONV - 1:SSD_CONV, :] * raw.astype(F32)
    for j in range(SSD_CONV - 1):
        acc = acc + w[j:j + 1, :] * shifted[j * lc:(j + 1) * lc, :]
    return _silu(acc)


def _ssd_kernel(z_ref, xs_ref, bc_ref, xsp_ref, bcp_ref, dt_ref, cwx_ref, cwb_ref, cbx_ref, cbb_ref, dtb_ref,
                alog_ref, dexp_ref, ng_ref, e_ref, sh_ref, o_ref, state_scr):
    lc = SSD_CHUNK
    hw = SSD_HEADS // SSD_GROUPS * HEAD_DIM
    first = pl.program_id(1) == 0

    @pl.when(first)
    def _():
        state_scr[...] = jnp.zeros_like(state_scr)

    rows = lax.broadcasted_iota(jnp.int32, (lc, lc), 0)
    cols = lax.broadcasted_iota(jnp.int32, (lc, lc), 1)
    lower = rows >= cols
    state = [state_scr[:, g * hw:(g + 1) * hw] for g in range(SSD_GROUPS)]
    prev_x = jnp.where(first, jnp.zeros_like(xsp_ref[...]), xsp_ref[...])
    prev_b = jnp.where(first, jnp.zeros_like(bcp_ref[...]), bcp_ref[...])
    for j in range(xs_ref.shape[0] // lc):
        sub = slice(j * lc, (j + 1) * lc)
        raw_x = xs_ref[sub, :]
        raw_b = bc_ref[sub, :]
        y, state = _ssd_chunk(raw_x, raw_b, prev_x, prev_b, dt_ref[sub, :], z_ref[sub, :], state, lower,
                              cwx_ref, cwb_ref, cbx_ref, cbb_ref, dtb_ref, alog_ref, dexp_ref, e_ref, sh_ref)
        ms = jnp.mean(y * y, axis=-1, keepdims=True)
        o_ref[sub, :] = (y * lax.rsqrt(ms + RMS_EPS) * ng_ref[...]).astype(BF16)
        prev_x, prev_b = raw_x, raw_b
    for g in range(SSD_GROUPS):
        state_scr[:, g * hw:(g + 1) * hw] = state[g]


def _ssd_chunk(raw_x, raw_b, prev_x, prev_b, dt_raw, z, state, lower,
               cwx_ref, cwb_ref, cbx_ref, cbb_ref, dtb_ref, alog_ref, dexp_ref, e_ref, sh_ref):
    lc = SSD_CHUNK
    hw = SSD_HEADS // SSD_GROUPS * HEAD_DIM
    xs = _causal_conv_silu(raw_x, prev_x, sh_ref[...], cwx_ref[...], cbx_ref[...])
    bc = _causal_conv_silu(raw_b, prev_b, sh_ref[...], cwb_ref[...], cbb_ref[...])

    dt = _softplus(dt_raw + dtb_ref[...])
    adt = dt * (-jnp.exp(alog_ref[...]))
    a_cum = _dot_exact_rhs_lhs(lower.astype(BF16), adt)
    a_cum_t = a_cum.T
    a_last = a_cum[lc - 1:lc, :]
    stacked = jnp.concatenate([dt, jnp.exp(a_last - a_cum), jnp.exp(a_cum)], axis=0)
    expanded = _dot(stacked.astype(BF16), e_ref[...])
    dt_e = expanded[0:lc]
    dec_e = expanded[lc:2 * lc]
    ea_e = expanded[2 * lc:3 * lc]
    al_e = _dot_exact_rhs(jnp.broadcast_to(jnp.exp(a_last), (8, LANES)), e_ref[...])[0:1]

    xdt = xs * dt_e
    xdt_b = xdt.astype(BF16)
    xd_b = (xdt * dec_e).astype(BF16)
    lane = lax.broadcasted_iota(jnp.int32, (1, LANES), 1)
    pieces = []
    new_state = []
    for g in range(SSD_GROUPS):
        bg = bc[:, g * SSD_STATE:(g + 1) * SSD_STATE]
        cg = bc[:, (SSD_GROUPS + g) * SSD_STATE:(SSD_GROUPS + g + 1) * SSD_STATE]
        cg_b = cg.astype(BF16)
        cb = _nt(cg_b, bg.astype(BF16))
        st = state[g]
        y_off = _dot(cg_b, st.astype(BF16)) * ea_e[:, g * hw:(g + 1) * hw]
        new_st = _dot(bg.T.astype(BF16), xd_b[:, g * hw:(g + 1) * hw])
        new_state.append(al_e[:, g * hw:(g + 1) * hw] * st + new_st)
        for pair in range(SSD_HEADS // SSD_GROUPS // 2):
            acc = None
            c0 = g * hw + pair * LANES
            xp = xdt_b[:, c0:c0 + LANES]
            for half in range(2):
                h = g * (SSD_HEADS // SSD_GROUPS) + pair * 2 + half
                diff = a_cum[:, h:h + 1] - a_cum_t[h:h + 1, :]
                decay = jnp.exp(jnp.where(lower, diff, NEG))
                m = (cb * decay).astype(BF16)
                in_half = (lane >= half * HEAD_DIM) & (lane < (half + 1) * HEAD_DIM)
                part = _dot(m, jnp.where(in_half, xp, jnp.zeros_like(xp)))
                acc = part if acc is None else acc + part
            pieces.append(acc + y_off[:, pair * LANES:(pair + 1) * LANES])
    y = jnp.concatenate(pieces, axis=1) + dexp_ref[...] * xs
    return y * _silu(z.astype(F32)), new_state


def _dot_exact_rhs_lhs(m_bf16, x):
    hi, mid, lo = _split3(x)
    return _dot(m_bf16, hi) + _dot(m_bf16, mid) + _dot(m_bf16, lo)


def _ssd(proj, small, conv_w, conv_b, dt_bias, a_log, d_skip, norm_g, bsz, seq):
    lc = SSD_CHUNK
    nc = seq // lc
    t = bsz * seq
    pad = LANES - SSD_HEADS
    dtb = jnp.pad(dt_bias, (0, pad)).reshape(1, LANES)
    alog = jnp.pad(a_log, (0, pad)).reshape(1, LANES)
    dexp = jnp.repeat(d_skip, HEAD_DIM).reshape(1, SSD_WIDTH)
    expand = (jnp.arange(LANES)[:, None] == (jnp.arange(SSD_WIDTH) // HEAD_DIM)[None, :]).astype(BF16)
    nbc = 2 * SSD_GROUPS * SSD_STATE
    src = lc + jnp.arange(lc)[None, :] - (SSD_CONV - 1) + jnp.arange(SSD_CONV - 1)[:, None]
    shift = (src.reshape(-1, 1) == jnp.arange(2 * lc)[None, :]).astype(BF16)
    ns = nc // SSD_SUB
    rs = SSD_SUB * lc
    row = lambda b, c: b * ns + c
    prev = lambda b, c: b * nc + jnp.maximum(SSD_SUB * c - 1, 0)
    const = lambda b, c: (0, 0)
    return pl.pallas_call(
        _ssd_kernel,
        grid=(bsz, ns),
        in_specs=[
            pl.BlockSpec((rs, SSD_WIDTH), lambda b, c: (row(b, c), EV_ZA // SSD_WIDTH)),
            pl.BlockSpec((rs, SSD_WIDTH), lambda b, c: (row(b, c), EV_XS // SSD_WIDTH)),
            pl.BlockSpec((rs, nbc), lambda b, c: (row(b, c), EV_BC // nbc)),
            pl.BlockSpec((lc, SSD_WIDTH), lambda b, c: (prev(b, c), EV_XS // SSD_WIDTH)),
            pl.BlockSpec((lc, nbc), lambda b, c: (prev(b, c), EV_BC // nbc)),
            pl.BlockSpec((rs, LANES), lambda b, c: (row(b, c), 0)),
            pl.BlockSpec((SSD_CONV, SSD_WIDTH), const),
            pl.BlockSpec((SSD_CONV, nbc), const),
            pl.BlockSpec((1, SSD_WIDTH), const),
            pl.BlockSpec((1, nbc), const),
            pl.BlockSpec((1, LANES), const),
            pl.BlockSpec((1, LANES), const),
            pl.BlockSpec((1, SSD_WIDTH), const),
            pl.BlockSpec((1, SSD_WIDTH), const),
            pl.BlockSpec((LANES, SSD_WIDTH), const),
            pl.BlockSpec(((SSD_CONV - 1) * lc, 2 * lc), const),
        ],
        out_specs=pl.BlockSpec((rs, SSD_WIDTH), lambda b, c: (row(b, c), 0)),
        out_shape=jax.ShapeDtypeStruct((t, SSD_WIDTH), BF16),
        scratch_shapes=[pltpu.VMEM((SSD_STATE, SSD_WIDTH), F32)],
        compiler_params=_params(("parallel", "arbitrary")),
        name="ssd",
    )(proj, proj, proj, proj, proj, small,
      conv_w[:, :SSD_WIDTH], conv_w[:, SSD_WIDTH:], conv_b[:SSD_WIDTH].reshape(1, -1),
      conv_b[SSD_WIDTH:].reshape(1, -1), dtb, alog, dexp, norm_g.reshape(1, -1), expand, shift)


def _fold_rows(x, op):
    out = x[0:8, :]
    for i in range(1, x.shape[0] // 8):
        out = op(out, x[8 * i:8 * (i + 1), :])
    return out


def _attend_pair(qa_scr, ka_scr, vt_scr, s_scr, p_scr, ot_scr, bias_fn):
    blk = ATT_BLOCK
    nb = qa_scr.shape[1] // blk
    heads = (0, 1)
    causal = (lax.broadcasted_iota(jnp.int32, (blk, blk), 0)
              <= lax.broadcasted_iota(jnp.int32, (blk, blk), 1))

    def score_tile(half, qb, n, m8):
        s = _nt(ka_scr[half, n * blk:(n + 1) * blk, :], qa_scr[half, qb * blk:(qb + 1) * blk, :])
        bias = bias_fn(half, qb, n)
        if bias is not None:
            s = s + bias
        if n == qb:
            s = jnp.where(causal, s, NEG)
        s_scr[half, qb % 2, n] = s
        part = _fold_rows(s, jnp.maximum)
        return part if m8 is None else jnp.maximum(m8, part)

    def prob_tile(half, qb, n, m):
        p = jnp.exp2(s_scr[half, qb % 2, n] - m)
        p_scr[half, qb % 2, n * blk:(n + 1) * blk, :] = p.astype(BF16)

    m8 = [score_tile(half, 0, 0, None) for half in heads]
    for qb in range(nb):
        m = [jnp.max(m8[half], axis=0, keepdims=True) for half in heads]
        nxt = qb + 1
        m8 = [None, None]
        for n in range(nxt + 1):
            for half in heads:
                if nxt < nb:
                    m8[half] = score_tile(half, nxt, n, m8[half])
            for half in heads:
                if n <= qb:
                    prob_tile(half, qb, n, m[half])
        keys = (qb + 1) * blk
        for half in heads:
            acc = _dot(vt_scr[half, :, 0:keys], p_scr[half, qb % 2, 0:keys, :])
            out = acc[0:HEAD_DIM] / acc[HEAD_DIM:HEAD_DIM + 1]
            ot_scr[half * HEAD_DIM:(half + 1) * HEAD_DIM, qb * blk:(qb + 1) * blk] = out


def _store_vt(v_ref, vt_scr):
    ones = jnp.ones((VT_ROWS - HEAD_DIM, ATT_BLOCK), BF16)
    for n in range(v_ref.shape[0] // ATT_BLOCK):
        cols = slice(n * ATT_BLOCK, (n + 1) * ATT_BLOCK)
        vt = v_ref[cols, :].astype(F32).T.astype(BF16)
        for half in range(2):
            vt_scr[half, 0:HEAD_DIM, cols] = vt[half * HEAD_DIM:(half + 1) * HEAD_DIM, :]
            vt_scr[half, HEAD_DIM:VT_ROWS, cols] = ones


def _gated_output(ot_scr, z_ref, o_ref):
    for i in range(ot_scr.shape[1] // ATT_BLOCK):
        rows = slice(i * ATT_BLOCK, (i + 1) * ATT_BLOCK)
        o = ot_scr[:, rows].T
        o_ref[rows, :] = (o * _silu(z_ref[rows, :].astype(F32))).astype(BF16)


def _head_lanes(half):
    lane = lax.broadcasted_iota(jnp.int32, (1, LANES), 1)
    return (lane >= half * HEAD_DIM) & (lane < (half + 1) * HEAD_DIM)


def _moba_kernel(q_ref, k_ref, v_ref, z_ref, o_ref, vt_scr, kbar_scr, bias_scr, qa_scr, ka_scr, s_scr, p_scr,
                 ot_scr):
    blk = ATT_BLOCK
    seq = q_ref.shape[0]
    nb = seq // blk
    _store_vt(v_ref, vt_scr)
    for n in range(nb):
        kb = k_ref[n * blk:(n + 1) * blk, :].astype(F32)
        kbar_scr[n:n + 1, :] = jnp.mean(kb, axis=0, keepdims=True)
    n_idx = lax.broadcasted_iota(jnp.int32, (nb, seq), 0)
    q_blk = lax.broadcasted_iota(jnp.int32, (nb, seq), 1) >> ATT_BLOCK_SHIFT

    for half in range(2):
        in_half = _head_lanes(half)
        q = q_ref[...]
        qi = jnp.where(in_half, q, jnp.zeros_like(q))
        qa_scr[half] = qi
        ka_scr[half] = k_ref[...]
        kbar = jnp.where(in_half, kbar_scr[...], 0.0)
        kb_hi = kbar.astype(BF16)
        kb_lo = (kbar - kb_hi.astype(F32)).astype(BF16)
        gate = _nt(kb_hi, qi) + _nt(kb_lo, qi)
        rank = jnp.zeros((nb, seq), F32)
        for mth in range(nb):
            gm = gate[mth:mth + 1, :]
            beats = (gm > gate) | ((gm == gate) & (mth < n_idx))
            rank = rank + jnp.where(beats & (mth < q_blk), 1.0, 0.0)
        chosen = (rank < float(MOBA_TOPK)) & (n_idx < q_blk)
        bias_scr[half] = jnp.where(chosen, 0.0, NEG)

    def bias_fn(half, qb, n):
        if n == qb or qb <= MOBA_TOPK:
            return None
        return bias_scr[half, n:n + 1, qb * blk:(qb + 1) * blk]

    _attend_pair(qa_scr, ka_scr, vt_scr, s_scr, p_scr, ot_scr, bias_fn)
    _gated_output(ot_scr, z_ref, o_ref)


def _moba(proj, bsz, seq):
    blk = ATT_BLOCK
    nb = seq // blk
    pairs = MOBA_WIDTH // LANES
    t = bsz * seq
    return pl.pallas_call(
        _moba_kernel,
        grid=(bsz, pairs),
        in_specs=[
            pl.BlockSpec((seq, LANES), lambda b, p: (b, EV_Q // LANES + p)),
            pl.BlockSpec((seq, LANES), lambda b, p: (b, EV_K // LANES + p)),
            pl.BlockSpec((seq, LANES), lambda b, p: (b, EV_V // LANES + p)),
            pl.BlockSpec((seq, LANES), lambda b, p: (b, EV_ZB // LANES + p)),
        ],
        out_specs=pl.BlockSpec((seq, LANES), lambda b, p: (b, p)),
        out_shape=jax.ShapeDtypeStruct((t, MOBA_WIDTH), BF16),
        scratch_shapes=[
            pltpu.VMEM((2, VT_ROWS, seq), BF16),
            pltpu.VMEM((nb, LANES), F32),
            pltpu.VMEM((2, nb, seq), F32),
            pltpu.VMEM((2, seq, LANES), BF16),
            pltpu.VMEM((2, seq, LANES), BF16),
            pltpu.VMEM((2, 2, nb, blk, blk), F32),
            pltpu.VMEM((2, 2, seq, blk), BF16),
            pltpu.VMEM((LANES, seq), F32),
        ],
        compiler_params=_params(("parallel", "parallel")),
        name="moba",
    )(proj, proj, proj, proj)


def _fgate_kernel(f_ref, fb_ref, o_ref):
    blk = ATT_BLOCK
    rows = lax.broadcasted_iota(jnp.int32, (blk, blk), 0)
    cols = lax.broadcasted_iota(jnp.int32, (blk, blk), 1)
    lower = (rows >= cols).astype(BF16)
    carry = jnp.zeros((1, LANES), F32)
    for i in range(f_ref.shape[0] // blk):
        nlf = _softplus(-(f_ref[i * blk:(i + 1) * blk, :] + fb_ref[...]))
        csum = _dot_exact_rhs_lhs(lower, nlf) + carry
        o_ref[i * blk:(i + 1) * blk, :] = csum
        carry = csum[blk - 1:blk, :]


def _fgate(small, fgate_b, bsz, seq):
    fb = jnp.pad(fgate_b, (0, LANES - FOX_HEADS)).reshape(1, LANES)
    return pl.pallas_call(
        _fgate_kernel,
        grid=(bsz,),
        in_specs=[
            pl.BlockSpec((seq, LANES), lambda b: (b, 0)),
            pl.BlockSpec((1, LANES), lambda b: (0, 0)),
        ],
        out_specs=pl.BlockSpec((seq, LANES), lambda b: (b, 0)),
        out_shape=jax.ShapeDtypeStruct((bsz * seq, LANES), F32),
        compiler_params=_params(("parallel",)),
        name="fox_gate",
    )(small, fb)


def _fox_kernel(q_ref, k_ref, v_ref, z_ref, nf_ref, o_ref, vt_scr, fb_scr, qa_scr, ka_scr, s_scr, p_scr, ot_scr):
    pair = pl.program_id(1)
    nb = q_ref.shape[0] // ATT_BLOCK
    blk = ATT_BLOCK
    _store_vt(v_ref, vt_scr)
    lane = lax.broadcasted_iota(jnp.int32, (1, LANES), 1)
    for half in range(2):
        own_lane = lane == pair * 2 + half
        for n in range(nb):
            col = jnp.sum(jnp.where(own_lane, nf_ref[n * blk:(n + 1) * blk, :], 0.0), axis=1, keepdims=True)
            fb_scr[half, n] = jnp.broadcast_to(col * LOG2E, (blk, LANES))
        q = q_ref[...]
        qa_scr[half] = jnp.where(_head_lanes(half), q, jnp.zeros_like(q))
        ka_scr[half] = k_ref[...]

    def bias_fn(half, qb, n):
        fb = fb_scr[half, n]
        return jnp.concatenate([fb, fb], axis=1)

    _attend_pair(qa_scr, ka_scr, vt_scr, s_scr, p_scr, ot_scr, bias_fn)
    _gated_output(ot_scr, z_ref, o_ref)


def _fox(proj, negf, bsz, seq):
    blk = ATT_BLOCK
    nb = seq // blk
    pairs = FOX_WIDTH // LANES
    t = bsz * seq
    return pl.pallas_call(
        _fox_kernel,
        grid=(bsz, pairs),
        in_specs=[
            pl.BlockSpec((seq, LANES), lambda b, p: (b, OD_Q // LANES + p)),
            pl.BlockSpec((seq, LANES), lambda b, p: (b, OD_K // LANES + p)),
            pl.BlockSpec((seq, LANES), lambda b, p: (b, OD_V // LANES + p)),
            pl.BlockSpec((seq, LANES), lambda b, p: (b, OD_ZC // LANES + p)),
            pl.BlockSpec((seq, LANES), lambda b, p: (b, 0)),
        ],
        out_specs=pl.BlockSpec((seq, LANES), lambda b, p: (b, p)),
        out_shape=jax.ShapeDtypeStruct((t, FOX_WIDTH), BF16),
        scratch_shapes=[
            pltpu.VMEM((2, VT_ROWS, seq), BF16),
            pltpu.VMEM((2, nb, blk, LANES), F32),
            pltpu.VMEM((2, seq, LANES), BF16),
            pltpu.VMEM((2, seq, LANES), BF16),
            pltpu.VMEM((2, 2, nb, blk, blk), F32),
            pltpu.VMEM((2, 2, seq, blk), BF16),
            pltpu.VMEM((LANES, seq), F32),
        ],
        compiler_params=_params(("parallel", "parallel")),
        name="fox",
    )(proj, proj, proj, proj, negf)


def _s5scan_kernel(u_ref, perm_ref, permt_ref, lr_ref, li_ref, ldt_ref, bwr_ref, bwi_ref, cwr_ref, cwi_ref, d_ref,
                   y_ref, bre_scr, bim_scr, cre_scr, cim_scr, ar_scr, ai_scr, xr_scr, xi_scr, zr_scr, zi_scr,
                   *, bsz):
    nset, cw, sw = bre_scr.shape

    @pl.when(pl.program_id(0) == 0)
    def _():
        chan_grp = lax.broadcasted_iota(jnp.int32, (cw, sw), 0) >> S5_GROUP_SHIFT
        state_grp = lax.broadcasted_iota(jnp.int32, (cw, sw), 1) >> S5_STATE_SHIFT
        same_b = chan_grp == state_grp
        same_c = ((lax.broadcasted_iota(jnp.int32, (sw, cw), 0) >> S5_STATE_SHIFT)
                  == (lax.broadcasted_iota(jnp.int32, (sw, cw), 1) >> S5_GROUP_SHIFT))
        for s in range(nset):
            lr = lr_ref[s]
            li = li_ref[s]
            dt = jnp.exp(ldt_ref[s])
            mag = jnp.exp(lr * dt)
            ar = mag * jnp.cos(li * dt)
            ai = mag * jnp.sin(li * dt)
            den = lr * lr + li * li
            qr = ((ar - 1.0) * lr + ai * li) / den
            qi = (ai * lr - (ar - 1.0) * li) / den
            ar_scr[s] = jnp.broadcast_to(ar, (bsz, sw))
            ai_scr[s] = jnp.broadcast_to(ai, (bsz, sw))
            bwr = bwr_ref[s]
            bwi = bwi_ref[s]
            bre_scr[s] = jnp.where(same_b, qr * bwr - qi * bwi, 0.0).astype(BF16)
            bim_scr[s] = jnp.where(same_b, qr * bwi + qi * bwr, 0.0).astype(BF16)
            cre_scr[s] = jnp.where(same_c, cwr_ref[s], 0.0).astype(BF16)
            cim_scr[s] = jnp.where(same_c, cwi_ref[s], 0.0).astype(BF16)
        xr_scr[...] = jnp.zeros_like(xr_scr)
        xi_scr[...] = jnp.zeros_like(xi_scr)

    steps = u_ref.shape[1]
    width = u_ref.shape[2]
    u_tb = _dot(perm_ref[...], u_ref[...].reshape(bsz * steps, width))
    u_b = u_tb.astype(BF16)
    half_rows = (steps // 2) * bsz

    def project(half, s):
        rows = slice(half * half_rows, (half + 1) * half_rows)
        us = u_b[rows, s * cw:(s + 1) * cw]
        zr_scr[s, rows, :] = _dot(us, bre_scr[s])
        zi_scr[s, rows, :] = _dot(us, bim_scr[s])

    for s in range(nset):
        project(0, s)

    def step(t, carry):
        r0 = t * bsz
        out = []
        for s in range(nset):
            xr, xi = carry[2 * s], carry[2 * s + 1]
            ar = ar_scr[s]
            ai = ai_scr[s]
            nr = ar * xr - ai * xi + zr_scr[s, r0:r0 + bsz, :]
            ni = ar * xi + ai * xr + zi_scr[s, r0:r0 + bsz, :]
            zr_scr[s, r0:r0 + bsz, :] = nr
            zi_scr[s, r0:r0 + bsz, :] = ni
            out += [nr, ni]
        return out

    def readout(half, s):
        rows = slice(half * half_rows, (half + 1) * half_rows)
        return _dot(zr_scr[s, rows, :].astype(BF16), cre_scr[s]) - _dot(zi_scr[s, rows, :].astype(BF16), cim_scr[s])

    every = steps // 2 // nset
    carry = []
    for s in range(nset):
        carry += [xr_scr[s], xi_scr[s]]
    for t in range(steps // 2):
        carry = step(t, carry)
        if t % every == 0:
            project(1, t // every)
    first_half = []
    for t in range(steps // 2, steps):
        carry = step(t, carry)
        k = t - steps // 2
        if k % every == 0:
            first_half.append(readout(0, k // every))
    for s in range(nset):
        xr_scr[s] = carry[2 * s]
        xi_scr[s] = carry[2 * s + 1]
    second_half = [readout(1, s) for s in range(nset)]
    xc = jnp.concatenate([jnp.concatenate(first_half, axis=1), jnp.concatenate(second_half, axis=1)], axis=0)
    y = (xc + d_ref[...] * u_tb).astype(BF16)
    y_bt = _dot(permt_ref[...], y).astype(BF16)
    y_ref[...] = y_bt.reshape(bsz, steps, width)


def _s5scan(proj3, lam_re, lam_im, log_dt, b_re, b_im, c_re, c_im, d_skip):
    bsz, seq, _ = proj3.shape
    width = S5_WIDTH
    nset = S5_SETS
    cw = width // nset
    gs = S5_GROUPS // nset
    sw = gs * S5_STATE
    blk = S5_STEPS * bsz
    tb = jnp.arange(blk)
    perm = ((tb % bsz) * S5_STEPS + tb // bsz)[:, None] == jnp.arange(blk)[None, :]
    vec = lambda a: a.reshape(nset, 1, sw)
    bw = lambda b: jnp.tile(jnp.swapaxes(b, 1, 2).reshape(nset, cw, S5_STATE), (1, 1, gs))
    cw_t = lambda c: jnp.tile(
        jnp.swapaxes(c, 1, 2).reshape(nset, gs, S5_STATE, S5_GROUP).transpose(0, 2, 1, 3).reshape(nset, S5_STATE, cw),
        (1, gs, 1))
    full3 = lambda a, b, c: pl.BlockSpec((a, b, c), lambda i: (0, 0, 0))
    return pl.pallas_call(
        functools.partial(_s5scan_kernel, bsz=bsz),
        grid=(seq // S5_STEPS,),
        in_specs=[
            pl.BlockSpec((bsz, S5_STEPS, width), lambda i: (0, i, OD_U // width)),
            pl.BlockSpec((blk, blk), lambda i: (0, 0)),
            pl.BlockSpec((blk, blk), lambda i: (0, 0)),
            full3(nset, 1, sw), full3(nset, 1, sw), full3(nset, 1, sw),
            full3(nset, cw, sw), full3(nset, cw, sw),
            full3(nset, sw, cw), full3(nset, sw, cw),
            pl.BlockSpec((1, width), lambda i: (0, 0)),
        ],
        out_specs=pl.BlockSpec((bsz, S5_STEPS, width), lambda i: (0, i, 0)),
        out_shape=jax.ShapeDtypeStruct((bsz, seq, width), BF16),
        scratch_shapes=[
            pltpu.VMEM((nset, cw, sw), BF16), pltpu.VMEM((nset, cw, sw), BF16),
            pltpu.VMEM((nset, sw, cw), BF16), pltpu.VMEM((nset, sw, cw), BF16),
            pltpu.VMEM((nset, bsz, sw), F32), pltpu.VMEM((nset, bsz, sw), F32),
            pltpu.VMEM((nset, bsz, sw), F32), pltpu.VMEM((nset, bsz, sw), F32),
            pltpu.VMEM((nset, blk, sw), F32), pltpu.VMEM((nset, blk, sw), F32),
        ],
        compiler_params=_params(("arbitrary",)),
        name="s5_scan",
    )(proj3, perm.astype(BF16), perm.T.astype(BF16), vec(lam_re), vec(lam_im), vec(jnp.repeat(log_dt, S5_STATE)),
      bw(b_re), bw(b_im), cw_t(c_re), cw_t(c_im), d_skip.reshape(1, width))


def _s5post_kernel(y_ref, z_ref, gw_ref, gb_ref, o_ref):
    y = y_ref[...].astype(F32)
    y = 0.5 * y * (1.0 + jnp.tanh(math.sqrt(2.0 / math.pi) * (y + 0.044715 * (y * y * y))))
    y = y * jax.nn.sigmoid(_dot(y.astype(BF16), gw_ref[...]) + gb_ref[...])
    o_ref[...] = (y * _silu(z_ref[...].astype(F32))).astype(BF16)


def _s5post(y_pre, proj, glu_w, glu_b, tm=1024):
    t, w = y_pre.shape
    return pl.pallas_call(
        _s5post_kernel,
        grid=(t // tm,),
        in_specs=[
            pl.BlockSpec((tm, w), lambda i: (i, 0)),
            pl.BlockSpec((tm, w), lambda i: (i, OD_ZD // S5_WIDTH)),
            pl.BlockSpec((w, w), lambda i: (0, 0)),
            pl.BlockSpec((1, w), lambda i: (0, 0)),
        ],
        out_specs=pl.BlockSpec((tm, w), lambda i: (i, 0)),
        out_shape=jax.ShapeDtypeStruct((t, w), BF16),
        compiler_params=_params(("parallel",)),
        name="s5_post",
    )(y_pre, proj, glu_w.astype(BF16), glu_b.reshape(1, w))


def _pack_weights(in_w, pieces, small):
    wt = in_w.T
    main = jnp.concatenate([wt[a:b] if scale == 1.0 else wt[a:b] * scale for a, b, scale in pieces], axis=0)
    a, b = small
    rows = jnp.pad(wt[a:b], ((0, LANES - (b - a)), (0, 0)))
    hi = rows.astype(BF16)
    lo = (rows - hi.astype(F32)).astype(BF16)
    return main.astype(BF16), jnp.concatenate([hi, lo], axis=0)


def _even_layer(x2, mod, pre_g, post_g, in_w, conv_w, conv_b, dt_bias, a_log, d_skip, norm_g, out_w, bsz, seq):
    d = D_MODEL
    shift, scale, gate = (mod[:, i * d:(i + 1) * d].reshape(bsz, 1, d) for i in range(3))
    o_xbc = 2 * SSD_WIDTH
    o_dt = o_xbc + SSD_WIDTH + 2 * SSD_GROUPS * SSD_STATE
    o_q = o_dt + SSD_HEADS
    w, ws = _pack_weights(in_w, (
        (0, o_xbc + SSD_WIDTH, 1.0),
        (o_q, o_q + MOBA_WIDTH, ATT_SCALE),
        (o_q + MOBA_WIDTH, in_w.shape[1], 1.0),
        (o_xbc + SSD_WIDTH, o_dt, 1.0),
    ), (o_dt, o_q))
    proj, small = _inproj(x2, scale, shift, pre_g.reshape(1, d), w, ws, seq)
    y_a = _ssd(proj, small, conv_w, conv_b, dt_bias, a_log, d_skip, norm_g, bsz, seq)
    y_b = _moba(proj, bsz, seq)
    ow = out_w.astype(BF16)
    return _outproj(y_a, y_b, ow[:SSD_WIDTH], ow[SSD_WIDTH:], x2, gate, post_g.reshape(1, d), seq)


def _odd_layer(x2, mod, pre_g, post_g, in_w, fgate_b, lam_re, lam_im, log_dt, b_re, b_im, c_re, c_im,
               d_skip, glu_w, glu_b, out_w, bsz, seq):
    d = D_MODEL
    shift, scale, gate = (mod[:, i * d:(i + 1) * d].reshape(bsz, 1, d) for i in range(3))
    o_f = D_MIX + 3 * FOX_WIDTH
    o_u = o_f + FOX_HEADS
    w, ws = _pack_weights(in_w, (
        (0, D_MIX, 1.0),
        (D_MIX, D_MIX + FOX_WIDTH, ATT_SCALE),
        (D_MIX + FOX_WIDTH, o_f, 1.0),
        (o_u, in_w.shape[1], 1.0),
    ), (o_f, o_u))
    proj, small = _inproj(x2, scale, shift, pre_g.reshape(1, d), w, ws, seq)

    negf = _fgate(small, fgate_b, bsz, seq)
    y_c = _fox(proj, negf, bsz, seq)

    y_pre = _s5scan(proj.reshape(bsz, seq, OD_N), lam_re, lam_im, log_dt, b_re, b_im, c_re, c_im, d_skip)
    y_d = _s5post(y_pre.reshape(bsz * seq, S5_WIDTH), proj, glu_w, glu_b)

    ow = out_w.astype(BF16)
    return _outproj(y_c, y_d, ow[:FOX_WIDTH], ow[FOX_WIDTH:], x2, gate, post_g.reshape(1, d), seq)


def kernel(x, c, ada_w, ada_b, pre_g, post_g, even_in_w, even_conv_w, even_conv_b, even_dt_bias, even_a_log,
           even_d_skip, even_norm_g, even_out_w, odd_in_w, odd_fgate_b, odd_lam_re, odd_lam_im, odd_log_dt,
           odd_b_re, odd_b_im, odd_c_re, odd_c_im, odd_d_skip, odd_glu_w, odd_glu_b, odd_out_w):
    bsz, seq, d = x.shape
    depth = ada_w.shape[0]
    mod = _ada_mod(c, ada_w, ada_b)
    x2 = x.reshape(bsz * seq, d)
    for layer in range(depth):
        i = layer // 2
        if layer % 2 == 0:
            x2 = _even_layer(x2, mod[layer], pre_g[layer], post_g[layer], even_in_w[i], even_conv_w[i],
                             even_conv_b[i], even_dt_bias[i], even_a_log[i], even_d_skip[i], even_norm_g[i],
                             even_out_w[i], bsz, seq)
        else:
            x2 = _odd_layer(x2, mod[layer], pre_g[layer], post_g[layer], odd_in_w[i], odd_fgate_b[i],
                            odd_lam_re[i], odd_lam_im[i], odd_log_dt[i], odd_b_re[i], odd_b_im[i], odd_c_re[i],
                            odd_c_im[i], odd_d_skip[i], odd_glu_w[i], odd_glu_b[i], odd_out_w[i], bsz, seq)
    return x2.reshape(bsz, seq, d)
```

```python
import functools
import math

import jax
import jax.numpy as jnp
from jax import lax
from jax.experimental import pallas as pl
from jax.experimental.pallas import tpu as pltpu

F32 = jnp.float32
BF16 = jnp.bfloat16

D_MODEL = 1024
HEAD_DIM = 64
D_MIX = 2 * D_MODEL
SSD_WIDTH = 1024
SSD_HEADS = 16
SSD_GROUPS = 2
SSD_STATE = 128
SSD_CONV = 4
SSD_CHUNK = 128
SSD_SUB = 2
MOBA_WIDTH = 1024
MOBA_BLOCK = 256
MOBA_TOPK = 3
FOX_WIDTH = 1536
FOX_HEADS = 24
S5_WIDTH = 512
S5_GROUP = 16
S5_GROUP_SHIFT = 4
S5_GROUPS = 32
S5_STATE = 64
S5_STATE_SHIFT = 6
S5_SETS = 4
S5_STEPS = 64
RMS_EPS = 1e-6
ATT_BLOCK = 256
ATT_BLOCK_SHIFT = 8
LOG2E = math.log2(math.e)
ATT_SCALE = LOG2E / math.sqrt(HEAD_DIM)
VT_ROWS = HEAD_DIM + 16
LANES = 128
NEG = -1e30
VMEM_LIMIT = 48 * 1024 * 1024
PACK_ROWS = 512

EV_ZA, EV_ZB, EV_XS, EV_Q, EV_K, EV_V, EV_BC, EV_N = 0, 1024, 2048, 3072, 4096, 5120, 6144, 6656
OD_ZC, OD_ZD, OD_Q, OD_K, OD_V, OD_U, OD_N = 0, 1536, 2048, 3584, 5120, 6656, 7168


def _nt(a, b):
    return lax.dot_general(a, b, (((1,), (1,)), ((), ())), preferred_element_type=F32)


def _dot(a, b):
    return jnp.dot(a, b, preferred_element_type=F32)


def _split3(x):
    hi = x.astype(BF16)
    r = x - hi.astype(F32)
    mid = r.astype(BF16)
    lo = (r - mid.astype(F32)).astype(BF16)
    return hi, mid, lo


def _dot_exact_rhs(x, m_bf16):
    hi, mid, lo = _split3(x)
    return _dot(hi, m_bf16) + _dot(mid, m_bf16) + _dot(lo, m_bf16)


def _silu(x):
    return x * jax.nn.sigmoid(x)


def _softplus(x):
    return jnp.maximum(x, 0.0) + jnp.log1p(jnp.exp(-jnp.abs(x)))


def _params(sem, limit=VMEM_LIMIT):
    return pltpu.CompilerParams(dimension_semantics=sem, vmem_limit_bytes=limit)


def _ada_kernel(c_ref, w_ref, b_ref, o_ref):
    cond = _silu(c_ref[...])
    hi, mid, lo = _split3(cond)
    w = w_ref[0]
    whi, wmid, wlo = _split3(w)
    acc = _dot(hi, whi) + _dot(hi, wmid) + _dot(mid, whi)
    acc = acc + _dot(hi, wlo) + _dot(mid, wmid) + _dot(lo, whi)
    o_ref[0] = acc + b_ref[0]


def _ada_mod(c, ada_w, ada_b):
    depth, d, d3 = ada_w.shape
    bsz = c.shape[0]
    nj = d3 // d
    return pl.pallas_call(
        _ada_kernel,
        grid=(depth, nj),
        in_specs=[
            pl.BlockSpec((bsz, d), lambda l, j: (0, 0)),
            pl.BlockSpec((1, d, d), lambda l, j: (l, 0, j)),
            pl.BlockSpec((1, 1, d), lambda l, j: (l, 0, j)),
        ],
        out_specs=pl.BlockSpec((1, bsz, d), lambda l, j: (l, 0, j)),
        out_shape=jax.ShapeDtypeStruct((depth, bsz, d3), F32),
        compiler_params=_params(("parallel", "parallel")),
        name="ada_mod",
    )(c, ada_w, ada_b.reshape(depth, 1, d3))


def _inproj_kernel(x_ref, sc_ref, sh_ref, g_ref, w_ref, ws_ref, o_ref, os_ref, h_scr):
    @pl.when(pl.program_id(1) == 0)
    def _():
        x = x_ref[...]
        ms = jnp.mean(x * x, axis=-1, keepdims=True)
        xn = x * lax.rsqrt(ms + RMS_EPS) * g_ref[...]
        h = xn * (1.0 + sc_ref[0]) + sh_ref[0]
        h_b = h.astype(BF16)
        h_scr[...] = h_b
        r = _nt(h_b, ws_ref[...])
        os_ref[...] = r[:, :LANES] + r[:, LANES:]

    o_ref[...] = _nt(h_scr[...], w_ref[...]).astype(BF16)


def _inproj(x2, scale, shift, g, w, ws, seq, tm=1024):
    t, d = x2.shape
    n = w.shape[0]
    tn = n // 2
    per = seq // tm
    return pl.pallas_call(
        _inproj_kernel,
        grid=(t // tm, n // tn),
        in_specs=[
            pl.BlockSpec((tm, d), lambda i, j: (i, 0)),
            pl.BlockSpec((1, 1, d), lambda i, j: (i // per, 0, 0)),
            pl.BlockSpec((1, 1, d), lambda i, j: (i // per, 0, 0)),
            pl.BlockSpec((1, d), lambda i, j: (0, 0)),
            pl.BlockSpec((tn, d), lambda i, j: (j, 0)),
            pl.BlockSpec((2 * LANES, d), lambda i, j: (0, 0)),
        ],
        out_specs=[
            pl.BlockSpec((tm, tn), lambda i, j: (i, j)),
            pl.BlockSpec((tm, LANES), lambda i, j: (i, 0)),
        ],
        out_shape=[
            jax.ShapeDtypeStruct((t, n), BF16),
            jax.ShapeDtypeStruct((t, LANES), F32),
        ],
        scratch_shapes=[pltpu.VMEM((tm, d), BF16)],
        compiler_params=_params(("parallel", "arbitrary")),
        name="in_proj",
    )(x2, scale, shift, g, w, ws)


def _outproj_kernel(a_ref, b_ref, wa_ref, wb_ref, x_ref, gate_ref, pg_ref, o_ref):
    y = _dot(a_ref[...], wa_ref[...]) + _dot(b_ref[...], wb_ref[...])
    ms = jnp.mean(y * y, axis=-1, keepdims=True)
    yn = y * lax.rsqrt(ms + RMS_EPS) * pg_ref[...]
    o_ref[...] = x_ref[...] + gate_ref[0] * yn


def _outproj(a, b, wa, wb, x2, gate, pg, seq, tm=1024):
    t, d = x2.shape
    ka, kb = a.shape[1], b.shape[1]
    per = seq // tm
    return pl.pallas_call(
        _outproj_kernel,
        grid=(t // tm,),
        in_specs=[
            pl.BlockSpec((tm, ka), lambda i: (i, 0)),
            pl.BlockSpec((tm, kb), lambda i: (i, 0)),
            pl.BlockSpec((ka, d), lambda i: (0, 0)),
            pl.BlockSpec((kb, d), lambda i: (0, 0)),
            pl.BlockSpec((tm, d), lambda i: (i, 0)),
            pl.BlockSpec((1, 1, d), lambda i: (i // per, 0, 0)),
            pl.BlockSpec((1, d), lambda i: (0, 0)),
        ],
        out_specs=pl.BlockSpec((tm, d), lambda i: (i, 0)),
        out_shape=jax.ShapeDtypeStruct((t, d), F32),
        compiler_params=_params(("parallel",)),
        name="out_proj",
    )(a, b, wa, wb, x2, gate, pg)


def _causal_conv_silu(raw, prev, shift, w, b):
    lc = raw.shape[0]
    ext = jnp.concatenate([prev, raw], axis=0)
    shifted = _dot(shift, ext)
    acc = b + w[SSD_CONV - 1:SSD_CONV, :] * raw.astype(F32)
    for j in range(SSD_CONV - 1):
        acc = acc + w[j:j + 1, :] * shifted[j * lc:(j + 1) * lc, :]
    return _silu(acc)


def _ssd_kernel(z_ref, xs_ref, bc_ref, xsp_ref, bcp_ref, dt_ref, cwx_ref, cwb_ref, cbx_ref, cbb_ref, dtb_ref,
                alog_ref, dexp_ref, ng_ref, e_ref, sh_ref, o_ref, state_scr):
    lc = SSD_CHUNK
    hw = SSD_HEADS // SSD_GROUPS * HEAD_DIM
    first = pl.program_id(1) == 0

    @pl.when(first)
    def _():
        state_scr[...] = jnp.zeros_like(state_scr)

    rows = lax.broadcasted_iota(jnp.int32, (lc, lc), 0)
    cols = lax.broadcasted_iota(jnp.int32, (lc, lc), 1)
    lower = rows >= cols
    state = [state_scr[:, g * hw:(g + 1) * hw] for g in range(SSD_GROUPS)]
    prev_x = jnp.where(first, jnp.zeros_like(xsp_ref[...]), xsp_ref[...])
    prev_b = jnp.where(first, jnp.zeros_like(bcp_ref[...]), bcp_ref[...])
    for j in range(xs_ref.shape[0] // lc):
        sub = slice(j * lc, (j + 1) * lc)
        raw_x = xs_ref[sub, :]
        raw_b = bc_ref[sub, :]
        y, state = _ssd_chunk(raw_x, raw_b, prev_x, prev_b, dt_ref[sub, :], z_ref[sub, :], state, lower,
                              cwx_ref, cwb_ref, cbx_ref, cbb_ref, dtb_ref, alog_ref, dexp_ref, e_ref, sh_ref)
        ms = jnp.mean(y * y, axis=-1, keepdims=True)
        o_ref[sub, :] = (y * lax.rsqrt(ms + RMS_EPS) * ng_ref[...]).astype(BF16)
        prev_x, prev_b = raw_x, raw_b
    for g in range(SSD_GROUPS):
        state_scr[:, g * hw:(g + 1) * hw] = state[g]


def _ssd_chunk(raw_x, raw_b, prev_x, prev_b, dt_raw, z, state, lower,
               cwx_ref, cwb_ref, cbx_ref, cbb_ref, dtb_ref, alog_ref, dexp_ref, e_ref, sh_ref):
    lc = SSD_CHUNK
    hw = SSD_HEADS // SSD_GROUPS * HEAD_DIM
    xs = _causal_conv_silu(raw_x, prev_x, sh_ref[...], cwx_ref[...], cbx_ref[...])
    bc = _causal_conv_silu(raw_b, prev_b, sh_ref[...], cwb_ref[...], cbb_ref[...])

    dt = _softplus(dt_raw + dtb_ref[...])
    adt = dt * (-jnp.exp(alog_ref[...]))
    a_cum = _dot_exact_rhs_lhs(lower.astype(BF16), adt)
    a_cum_t = a_cum.T
    a_last = a_cum[lc - 1:lc, :]
    stacked = jnp.concatenate([dt, jnp.exp(a_last - a_cum), jnp.exp(a_cum)], axis=0)
    expanded = _dot(stacked.astype(BF16), e_ref[...])
    dt_e = expanded[0:lc]
    dec_e = expanded[lc:2 * lc]
    ea_e = expanded[2 * lc:3 * lc]
    al_e = _dot_exact_rhs(jnp.broadcast_to(jnp.exp(a_last), (8, LANES)), e_ref[...])[0:1]

    xdt = xs * dt_e
    xdt_b = xdt.astype(BF16)
    xd_b = (xdt * dec_e).astype(BF16)
    lane = lax.broadcasted_iota(jnp.int32, (1, LANES), 1)
    pieces = []
    new_state = []
    for g in range(SSD_GROUPS):
        bg = bc[:, g * SSD_STATE:(g + 1) * SSD_STATE]
        cg = bc[:, (SSD_GROUPS + g) * SSD_STATE:(SSD_GROUPS + g + 1) * SSD_STATE]
        cg_b = cg.astype(BF16)
        cb = _nt(cg_b, bg.astype(BF16))
        st = state[g]
        y_off = _dot(cg_b, st.astype(BF16)) * ea_e[:, g * hw:(g + 1) * hw]
        new_st = _dot(bg.T.astype(BF16), xd_b[:, g * hw:(g + 1) * hw])
        new_state.append(al_e[:, g * hw:(g + 1) * hw] * st + new_st)
        for pair in range(SSD_HEADS // SSD_GROUPS // 2):
            acc = None
            c0 = g * hw + pair * LANES
            xp = xdt_b[:, c0:c0 + LANES]
            for half in range(2):
                h = g * (SSD_HEADS // SSD_GROUPS) + pair * 2 + half
                diff = a_cum[:, h:h + 1] - a_cum_t[h:h + 1, :]
                decay = jnp.exp(jnp.where(lower, diff, NEG))
                m = (cb * decay).astype(BF16)
                in_half = (lane >= half * HEAD_DIM) & (lane < (half + 1) * HEAD_DIM)
                part = _dot(m, jnp.where(in_half, xp, jnp.zeros_like(xp)))
                acc = part if acc is None else acc + part
            pieces.append(acc + y_off[:, pair * LANES:(pair + 1) * LANES])
    y = jnp.concatenate(pieces, axis=1) + dexp_ref[...] * xs
    return y * _silu(z.astype(F32)), new_state


def _dot_exact_rhs_lhs(m_bf16, x):
    hi, mid, lo = _split3(x)
    return _dot(m_bf16, hi) + _dot(m_bf16, mid) + _dot(m_bf16, lo)


def _ssd(proj, small, conv_w, conv_b, dt_bias, a_log, d_skip, norm_g, bsz, seq):
    lc = SSD_CHUNK
    nc = seq // lc
    t = bsz * seq
    pad = LANES - SSD_HEADS
    dtb = jnp.pad(dt_bias, (0, pad)).reshape(1, LANES)
    alog = jnp.pad(a_log, (0, pad)).reshape(1, LANES)
    dexp = jnp.repeat(d_skip, HEAD_DIM).reshape(1, SSD_WIDTH)
    expand = (jnp.arange(LANES)[:, None] == (jnp.arange(SSD_WIDTH) // HEAD_DIM)[None, :]).astype(BF16)
    nbc = 2 * SSD_GROUPS * SSD_STATE
    src = lc + jnp.arange(lc)[None, :] - (SSD_CONV - 1) + jnp.arange(SSD_CONV - 1)[:, None]
    shift = (src.reshape(-1, 1) == jnp.arange(2 * lc)[None, :]).astype(BF16)
    ns = nc // SSD_SUB
    rs = SSD_SUB * lc
    row = lambda b, c: b * ns + c
    prev = lambda b, c: b * nc + jnp.maximum(SSD_SUB * c - 1, 0)
    const = lambda b, c: (0, 0)
    return pl.pallas_call(
        _ssd_kernel,
        grid=(bsz, ns),
        in_specs=[
            pl.BlockSpec((rs, SSD_WIDTH), lambda b, c: (row(b, c), EV_ZA // SSD_WIDTH)),
            pl.BlockSpec((rs, SSD_WIDTH), lambda b, c: (row(b, c), EV_XS // SSD_WIDTH)),
            pl.BlockSpec((rs, nbc), lambda b, c: (row(b, c), EV_BC // nbc)),
            pl.BlockSpec((lc, SSD_WIDTH), lambda b, c: (prev(b, c), EV_XS // SSD_WIDTH)),
            pl.BlockSpec((lc, nbc), lambda b, c: (prev(b, c), EV_BC // nbc)),
            pl.BlockSpec((rs, LANES), lambda b, c: (row(b, c), 0)),
            pl.BlockSpec((SSD_CONV, SSD_WIDTH), const),
            pl.BlockSpec((SSD_CONV, nbc), const),
            pl.BlockSpec((1, SSD_WIDTH), const),
            pl.BlockSpec((1, nbc), const),
            pl.BlockSpec((1, LANES), const),
            pl.BlockSpec((1, LANES), const),
            pl.BlockSpec((1, SSD_WIDTH), const),
            pl.BlockSpec((1, SSD_WIDTH), const),
            pl.BlockSpec((LANES, SSD_WIDTH), const),
            pl.BlockSpec(((SSD_CONV - 1) * lc, 2 * lc), const),
        ],
        out_specs=pl.BlockSpec((rs, SSD_WIDTH), lambda b, c: (row(b, c), 0)),
        out_shape=jax.ShapeDtypeStruct((t, SSD_WIDTH), BF16),
        scratch_shapes=[pltpu.VMEM((SSD_STATE, SSD_WIDTH), F32)],
        compiler_params=_params(("parallel", "arbitrary")),
        name="ssd",
    )(proj, proj, proj, proj, proj, small,
      conv_w[:, :SSD_WIDTH], conv_w[:, SSD_WIDTH:], conv_b[:SSD_WIDTH].reshape(1, -1),
      conv_b[SSD_WIDTH:].reshape(1, -1), dtb, alog, dexp, norm_g.reshape(1, -1), expand, shift)


def _fold_rows(x, op):
    out = x[0:8, :]
    for i in range(1, x.shape[0] // 8):
        out = op(out, x[8 * i:8 * (i + 1), :])
    return out


def _attend_pair(qa_scr, ka_scr, vt_scr, s_scr, p_scr, ot_scr, bias_fn):
    blk = ATT_BLOCK
    nb = qa_scr.shape[1] // blk
    heads = (0, 1)
    causal = (lax.broadcasted_iota(jnp.int32, (blk, blk), 0)
              <= lax.broadcasted_iota(jnp.int32, (blk, blk), 1))

    def score_tile(half, qb, n, m8):
        s = _nt(ka_scr[half, n * blk:(n + 1) * blk, :], qa_scr[half, qb * blk:(qb + 1) * blk, :])
        bias = bias_fn(half, qb, n)
        if bias is not None:
            s = s + bias
        if n == qb:
            s = jnp.where(causal, s, NEG)
        s_scr[half, qb % 2, n] = s
        part = _fold_rows(s, jnp.maximum)
        return part if m8 is None else jnp.maximum(m8, part)

    def prob_tile(half, qb, n, m):
        p = jnp.exp2(s_scr[half, qb % 2, n] - m)
        p_scr[half, qb % 2, n * blk:(n + 1) * blk, :] = p.astype(BF16)

    m8 = [score_tile(half, 0, 0, None) for half in heads]
    for qb in range(nb):
        m = [jnp.max(m8[half], axis=0, keepdims=True) for half in heads]
        nxt = qb + 1
        m8 = [None, None]
        for n in range(nxt + 1):
            for half in heads:
                if n <= qb:
                    prob_tile(half, qb, n, m[half])
            for half in heads:
                if nxt < nb:
                    m8[half] = score_tile(half, nxt, n, m8[half])
        keys = (qb + 1) * blk
        for half in heads:
            acc = _dot(vt_scr[half, :, 0:keys], p_scr[half, qb % 2, 0:keys, :])
            out = acc[0:HEAD_DIM] / acc[HEAD_DIM:HEAD_DIM + 1]
            ot_scr[half * HEAD_DIM:(half + 1) * HEAD_DIM, qb * blk:(qb + 1) * blk] = out


def _store_vt(v_ref, vt_scr):
    ones = jnp.ones((VT_ROWS - HEAD_DIM, ATT_BLOCK), BF16)
    for n in range(v_ref.shape[0] // ATT_BLOCK):
        cols = slice(n * ATT_BLOCK, (n + 1) * ATT_BLOCK)
        vt = v_ref[cols, :].astype(F32).T.astype(BF16)
        for half in range(2):
            vt_scr[half, 0:HEAD_DIM, cols] = vt[half * HEAD_DIM:(half + 1) * HEAD_DIM, :]
            vt_scr[half, HEAD_DIM:VT_ROWS, cols] = ones


def _gated_output(ot_scr, z_ref, o_ref):
    for i in range(ot_scr.shape[1] // ATT_BLOCK):
        rows = slice(i * ATT_BLOCK, (i + 1) * ATT_BLOCK)
        o = ot_scr[:, rows].T
        o_ref[rows, :] = (o * _silu(z_ref[rows, :].astype(F32))).astype(BF16)


def _head_lanes(half):
    lane = lax.broadcasted_iota(jnp.int32, (1, LANES), 1)
    return (lane >= half * HEAD_DIM) & (lane < (half + 1) * HEAD_DIM)


def _moba_kernel(q_ref, k_ref, v_ref, z_ref, o_ref, vt_scr, kbar_scr, bias_scr, qa_scr, ka_scr, s_scr, p_scr,
                 ot_scr):
    blk = ATT_BLOCK
    seq = q_ref.shape[0]
    nb = seq // blk
    _store_vt(v_ref, vt_scr)
    for n in range(nb):
        kb = k_ref[n * blk:(n + 1) * blk, :].astype(F32)
        kbar_scr[n:n + 1, :] = jnp.mean(kb, axis=0, keepdims=True)
    n_idx = lax.broadcasted_iota(jnp.int32, (nb, seq), 0)
    q_blk = lax.broadcasted_iota(jnp.int32, (nb, seq), 1) >> ATT_BLOCK_SHIFT

    for half in range(2):
        in_half = _head_lanes(half)
        q = q_ref[...]
        qi = jnp.where(in_half, q, jnp.zeros_like(q))
        qa_scr[half] = qi
        ka_scr[half] = k_ref[...]
        kbar = jnp.where(in_half, kbar_scr[...], 0.0)
        kb_hi = kbar.astype(BF16)
        kb_lo = (kbar - kb_hi.astype(F32)).astype(BF16)
        gate = _nt(kb_hi, qi) + _nt(kb_lo, qi)
        rank = jnp.zeros((nb, seq), F32)
        for mth in range(nb):
            gm = gate[mth:mth + 1, :]
            beats = (gm > gate) | ((gm == gate) & (mth < n_idx))
            rank = rank + jnp.where(beats & (mth < q_blk), 1.0, 0.0)
        chosen = (rank < float(MOBA_TOPK)) & (n_idx < q_blk)
        bias_scr[half] = jnp.where(chosen, 0.0, NEG)

    def bias_fn(half, qb, n):
        if n == qb or qb <= MOBA_TOPK:
            return None
        return bias_scr[half, n:n + 1, qb * blk:(qb + 1) * blk]

    _attend_pair(qa_scr, ka_scr, vt_scr, s_scr, p_scr, ot_scr, bias_fn)
    _gated_output(ot_scr, z_ref, o_ref)


def _moba(proj, bsz, seq):
    blk = ATT_BLOCK
    nb = seq // blk
    pairs = MOBA_WIDTH // LANES
    t = bsz * seq
    return pl.pallas_call(
        _moba_kernel,
        grid=(bsz, pairs),
        in_specs=[
            pl.BlockSpec((seq, LANES), lambda b, p: (b, EV_Q // LANES + p)),
            pl.BlockSpec((seq, LANES), lambda b, p: (b, EV_K // LANES + p)),
            pl.BlockSpec((seq, LANES), lambda b, p: (b, EV_V // LANES + p)),
            pl.BlockSpec((seq, LANES), lambda b, p: (b, EV_ZB // LANES + p)),
        ],
        out_specs=pl.BlockSpec((seq, LANES), lambda b, p: (b, p)),
        out_shape=jax.ShapeDtypeStruct((t, MOBA_WIDTH), BF16),
        scratch_shapes=[
            pltpu.VMEM((2, VT_ROWS, seq), BF16),
            pltpu.VMEM((nb, LANES), F32),
            pltpu.VMEM((2, nb, seq), F32),
            pltpu.VMEM((2, seq, LANES), BF16),
            pltpu.VMEM((2, seq, LANES), BF16),
            pltpu.VMEM((2, 2, nb, blk, blk), F32),
            pltpu.VMEM((2, 2, seq, blk), BF16),
            pltpu.VMEM((LANES, seq), F32),
        ],
        compiler_params=_params(("parallel", "parallel")),
        name="moba",
    )(proj, proj, proj, proj)


def _fgate_kernel(f_ref, fb_ref, o_ref):
    blk = ATT_BLOCK
    rows = lax.broadcasted_iota(jnp.int32, (blk, blk), 0)
    cols = lax.broadcasted_iota(jnp.int32, (blk, blk), 1)
    lower = (rows >= cols).astype(BF16)
    carry = jnp.zeros((1, LANES), F32)
    for i in range(f_ref.shape[0] // blk):
        nlf = _softplus(-(f_ref[i * blk:(i + 1) * blk, :] + fb_ref[...]))
        csum = _dot_exact_rhs_lhs(lower, nlf) + carry
        o_ref[i * blk:(i + 1) * blk, :] = csum
        carry = csum[blk - 1:blk, :]


def _fgate(small, fgate_b, bsz, seq):
    fb = jnp.pad(fgate_b, (0, LANES - FOX_HEADS)).reshape(1, LANES)
    return pl.pallas_call(
        _fgate_kernel,
        grid=(bsz,),
        in_specs=[
            pl.BlockSpec((seq, LANES), lambda b: (b, 0)),
            pl.BlockSpec((1, LANES), lambda b: (0, 0)),
        ],
        out_specs=pl.BlockSpec((seq, LANES), lambda b: (b, 0)),
        out_shape=jax.ShapeDtypeStruct((bsz * seq, LANES), F32),
        compiler_params=_params(("parallel",)),
        name="fox_gate",
    )(small, fb)


def _fox_kernel(q_ref, k_ref, v_ref, z_ref, nf_ref, o_ref, vt_scr, fb_scr, qa_scr, ka_scr, s_scr, p_scr, ot_scr):
    pair = pl.program_id(1)
    nb = q_ref.shape[0] // ATT_BLOCK
    blk = ATT_BLOCK
    _store_vt(v_ref, vt_scr)
    lane = lax.broadcasted_iota(jnp.int32, (1, LANES), 1)
    for half in range(2):
        own_lane = lane == pair * 2 + half
        for n in range(nb):
            col = jnp.sum(jnp.where(own_lane, nf_ref[n * blk:(n + 1) * blk, :], 0.0), axis=1, keepdims=True)
            fb_scr[half, n] = jnp.broadcast_to(col * LOG2E, (blk, LANES))
        q = q_ref[...]
        qa_scr[half] = jnp.where(_head_lanes(half), q, jnp.zeros_like(q))
        ka_scr[half] = k_ref[...]

    def bias_fn(half, qb, n):
        fb = fb_scr[half, n]
        return jnp.concatenate([fb, fb], axis=1)

    _attend_pair(qa_scr, ka_scr, vt_scr, s_scr, p_scr, ot_scr, bias_fn)
    _gated_output(ot_scr, z_ref, o_ref)


def _fox(proj, negf, bsz, seq):
    blk = ATT_BLOCK
    nb = seq // blk
    pairs = FOX_WIDTH // LANES
    t = bsz * seq
    return pl.pallas_call(
        _fox_kernel,
        grid=(bsz, pairs),
        in_specs=[
            pl.BlockSpec((seq, LANES), lambda b, p: (b, OD_Q // LANES + p)),
            pl.BlockSpec((seq, LANES), lambda b, p: (b, OD_K // LANES + p)),
            pl.BlockSpec((seq, LANES), lambda b, p: (b, OD_V // LANES + p)),
            pl.BlockSpec((seq, LANES), lambda b, p: (b, OD_ZC // LANES + p)),
            pl.BlockSpec((seq, LANES), lambda b, p: (b, 0)),
        ],
        out_specs=pl.BlockSpec((seq, LANES), lambda b, p: (b, p)),
        out_shape=jax.ShapeDtypeStruct((t, FOX_WIDTH), BF16),
        scratch_shapes=[
            pltpu.VMEM((2, VT_ROWS, seq), BF16),
            pltpu.VMEM((2, nb, blk, LANES), F32),
            pltpu.VMEM((2, seq, LANES), BF16),
            pltpu.VMEM((2, seq, LANES), BF16),
            pltpu.VMEM((2, 2, nb, blk, blk), F32),
            pltpu.VMEM((2, 2, seq, blk), BF16),
            pltpu.VMEM((LANES, seq), F32),
        ],
        compiler_params=_params(("parallel", "parallel")),
        name="fox",
    )(proj, proj, proj, proj, negf)


def _s5scan_kernel(u_ref, z_ref, perm_ref, permt_ref, lr_ref, li_ref, ldt_ref, bwr_ref, bwi_ref, cwr_ref, cwi_ref,
                   d_ref, gw_ref, gb_ref, y_ref, bre_scr, bim_scr, cre_scr, cim_scr, ar_scr, ai_scr, xr_scr, xi_scr, zr_scr, zi_scr,
                   *, bsz):
    nset, cw, sw = bre_scr.shape

    @pl.when(pl.program_id(0) == 0)
    def _():
        chan_grp = lax.broadcasted_iota(jnp.int32, (cw, sw), 0) >> S5_GROUP_SHIFT
        state_grp = lax.broadcasted_iota(jnp.int32, (cw, sw), 1) >> S5_STATE_SHIFT
        same_b = chan_grp == state_grp
        same_c = ((lax.broadcasted_iota(jnp.int32, (sw, cw), 0) >> S5_STATE_SHIFT)
                  == (lax.broadcasted_iota(jnp.int32, (sw, cw), 1) >> S5_GROUP_SHIFT))
        for s in range(nset):
            lr = lr_ref[s]
            li = li_ref[s]
            dt = jnp.exp(ldt_ref[s])
            mag = jnp.exp(lr * dt)
            ar = mag * jnp.cos(li * dt)
            ai = mag * jnp.sin(li * dt)
            den = lr * lr + li * li
            qr = ((ar - 1.0) * lr + ai * li) / den
            qi = (ai * lr - (ar - 1.0) * li) / den
            ar_scr[s] = jnp.broadcast_to(ar, (bsz, sw))
            ai_scr[s] = jnp.broadcast_to(ai, (bsz, sw))
            bwr = bwr_ref[s]
            bwi = bwi_ref[s]
            bre_scr[s] = jnp.where(same_b, qr * bwr - qi * bwi, 0.0).astype(BF16)
            bim_scr[s] = jnp.where(same_b, qr * bwi + qi * bwr, 0.0).astype(BF16)
            cre_scr[s] = jnp.where(same_c, cwr_ref[s], 0.0).astype(BF16)
            cim_scr[s] = jnp.where(same_c, cwi_ref[s], 0.0).astype(BF16)
        xr_scr[...] = jnp.zeros_like(xr_scr)
        xi_scr[...] = jnp.zeros_like(xi_scr)

    steps = u_ref.shape[1]
    width = u_ref.shape[2]
    u_tb = _dot(perm_ref[...], u_ref[...].reshape(bsz * steps, width))
    u_b = u_tb.astype(BF16)
    half_rows = (steps // 2) * bsz

    def project(half, s):
        rows = slice(half * half_rows, (half + 1) * half_rows)
        us = u_b[rows, s * cw:(s + 1) * cw]
        zr_scr[s, rows, :] = _dot(us, bre_scr[s])
        zi_scr[s, rows, :] = _dot(us, bim_scr[s])

    for s in range(nset):
        project(0, s)

    def step(t, carry):
        r0 = t * bsz
        out = []
        for s in range(nset):
            xr, xi = carry[2 * s], carry[2 * s + 1]
            ar = ar_scr[s]
            ai = ai_scr[s]
            nr = ar * xr - ai * xi + zr_scr[s, r0:r0 + bsz, :]
            ni = ar * xi + ai * xr + zi_scr[s, r0:r0 + bsz, :]
            zr_scr[s, r0:r0 + bsz, :] = nr
            zi_scr[s, r0:r0 + bsz, :] = ni
            out += [nr, ni]
        return out

    def readout(half, s):
        rows = slice(half * half_rows, (half + 1) * half_rows)
        return _dot(zr_scr[s, rows, :].astype(BF16), cre_scr[s]) - _dot(zi_scr[s, rows, :].astype(BF16), cim_scr[s])

    every = steps // 2 // nset
    carry = []
    for s in range(nset):
        carry += [xr_scr[s], xi_scr[s]]
    for t in range(steps // 2):
        carry = step(t, carry)
        if t % every == 0:
            project(1, t // every)
    first_half = []
    for t in range(steps // 2, steps):
        carry = step(t, carry)
        k = t - steps // 2
        if k % every == 0:
            first_half.append(readout(0, k // every))
    for s in range(nset):
        xr_scr[s] = carry[2 * s]
        xi_scr[s] = carry[2 * s + 1]
    second_half = [readout(1, s) for s in range(nset)]
    xc = jnp.concatenate([jnp.concatenate(first_half, axis=1), jnp.concatenate(second_half, axis=1)], axis=0)
    y = (xc + d_ref[...] * u_tb).astype(BF16)
    y = _dot(permt_ref[...], y)
    y = 0.5 * y * (1.0 + jnp.tanh(math.sqrt(2.0 / math.pi) * (y + 0.044715 * (y * y * y))))
    y = y * jax.nn.sigmoid(_dot(y.astype(BF16), gw_ref[...]) + gb_ref[...])
    z = z_ref[...].reshape(bsz * steps, width).astype(F32)
    y_ref[...] = (y * _silu(z)).astype(BF16).reshape(bsz, steps, width)


def _s5scan(proj3, lam_re, lam_im, log_dt, b_re, b_im, c_re, c_im, d_skip, glu_w, glu_b):
    bsz, seq, _ = proj3.shape
    width = S5_WIDTH
    nset = S5_SETS
    cw = width // nset
    gs = S5_GROUPS // nset
    sw = gs * S5_STATE
    blk = S5_STEPS * bsz
    tb = jnp.arange(blk)
    perm = ((tb % bsz) * S5_STEPS + tb // bsz)[:, None] == jnp.arange(blk)[None, :]
    vec = lambda a: a.reshape(nset, 1, sw)
    bw = lambda b: jnp.tile(jnp.swapaxes(b, 1, 2).reshape(nset, cw, S5_STATE), (1, 1, gs))
    cw_t = lambda c: jnp.tile(
        jnp.swapaxes(c, 1, 2).reshape(nset, gs, S5_STATE, S5_GROUP).transpose(0, 2, 1, 3).reshape(nset, S5_STATE, cw),
        (1, gs, 1))
    full3 = lambda a, b, c: pl.BlockSpec((a, b, c), lambda i: (0, 0, 0))
    return pl.pallas_call(
        functools.partial(_s5scan_kernel, bsz=bsz),
        grid=(seq // S5_STEPS,),
        in_specs=[
            pl.BlockSpec((bsz, S5_STEPS, width), lambda i: (0, i, OD_U // width)),
            pl.BlockSpec((bsz, S5_STEPS, width), lambda i: (0, i, OD_ZD // width)),
            pl.BlockSpec((blk, blk), lambda i: (0, 0)),
            pl.BlockSpec((blk, blk), lambda i: (0, 0)),
            full3(nset, 1, sw), full3(nset, 1, sw), full3(nset, 1, sw),
            full3(nset, cw, sw), full3(nset, cw, sw),
            full3(nset, sw, cw), full3(nset, sw, cw),
            pl.BlockSpec((1, width), lambda i: (0, 0)),
            pl.BlockSpec((width, width), lambda i: (0, 0)),
            pl.BlockSpec((1, width), lambda i: (0, 0)),
        ],
        out_specs=pl.BlockSpec((bsz, S5_STEPS, width), lambda i: (0, i, 0)),
        out_shape=jax.ShapeDtypeStruct((bsz, seq, width), BF16),
        scratch_shapes=[
            pltpu.VMEM((nset, cw, sw), BF16), pltpu.VMEM((nset, cw, sw), BF16),
            pltpu.VMEM((nset, sw, cw), BF16), pltpu.VMEM((nset, sw, cw), BF16),
            pltpu.VMEM((nset, bsz, sw), F32), pltpu.VMEM((nset, bsz, sw), F32),
            pltpu.VMEM((nset, bsz, sw), F32), pltpu.VMEM((nset, bsz, sw), F32),
            pltpu.VMEM((nset, blk, sw), F32), pltpu.VMEM((nset, blk, sw), F32),
        ],
        compiler_params=_params(("arbitrary",)),
        name="s5_scan",
    )(proj3, proj3, perm.astype(BF16), perm.T.astype(BF16), vec(lam_re), vec(lam_im),
      vec(jnp.repeat(log_dt, S5_STATE)), bw(b_re), bw(b_im), cw_t(c_re), cw_t(c_im), d_skip.reshape(1, width),
      glu_w.astype(BF16), glu_b.reshape(1, width))


def _pack_weights(in_w, pieces, small):
    wt = in_w.T
    d = wt.shape[1]
    starts, scales = [], []
    for a, b, scale in pieces:
        starts += list(range(a, b, PACK_ROWS))
        scales += [scale] * ((b - a) // PACK_ROWS)

    def pick(c, table):
        out = table[-1]
        for i in range(len(table) - 2, -1, -1):
            out = jnp.where(c == i, table[i], out)
        return out

    def pack_kernel(w_ref, o_ref):
        scale = pick(pl.program_id(0), [jnp.float32(s) for s in scales])
        o_ref[...] = (w_ref[...] * scale).astype(BF16)

    main = pl.pallas_call(
        pack_kernel,
        grid=(len(starts),),
        in_specs=[pl.BlockSpec((pl.Element(PACK_ROWS), pl.Element(d)),
                               lambda c: (pl.multiple_of(pick(c, [s // 8 for s in starts]) * 8, 8), 0))],
        out_specs=pl.BlockSpec((PACK_ROWS, d), lambda c: (c, 0)),
        out_shape=jax.ShapeDtypeStruct((len(starts) * PACK_ROWS, d), BF16),
        compiler_params=_params(("parallel",)),
        name="pack_w",
    )(wt)
    a, b = small
    rows = jnp.pad(wt[a:b], ((0, LANES - (b - a)), (0, 0)))
    hi = rows.astype(BF16)
    lo = (rows - hi.astype(F32)).astype(BF16)
    return main, jnp.concatenate([hi, lo], axis=0)


def _even_layer(x2, mod, pre_g, post_g, in_w, conv_w, conv_b, dt_bias, a_log, d_skip, norm_g, out_w, bsz, seq):
    d = D_MODEL
    shift, scale, gate = (mod[:, i * d:(i + 1) * d].reshape(bsz, 1, d) for i in range(3))
    o_xbc = 2 * SSD_WIDTH
    o_dt = o_xbc + SSD_WIDTH + 2 * SSD_GROUPS * SSD_STATE
    o_q = o_dt + SSD_HEADS
    w, ws = _pack_weights(in_w, (
        (0, o_xbc + SSD_WIDTH, 1.0),
        (o_q, o_q + MOBA_WIDTH, ATT_SCALE),
        (o_q + MOBA_WIDTH, in_w.shape[1], 1.0),
        (o_xbc + SSD_WIDTH, o_dt, 1.0),
    ), (o_dt, o_q))
    proj, small = _inproj(x2, scale, shift, pre_g.reshape(1, d), w, ws, seq)
    y_a = _ssd(proj, small, conv_w, conv_b, dt_bias, a_log, d_skip, norm_g, bsz, seq)
    y_b = _moba(proj, bsz, seq)
    ow = out_w.astype(BF16)
    return _outproj(y_a, y_b, ow[:SSD_WIDTH], ow[SSD_WIDTH:], x2, gate, post_g.reshape(1, d), seq)


def _odd_layer(x2, mod, pre_g, post_g, in_w, fgate_b, lam_re, lam_im, log_dt, b_re, b_im, c_re, c_im,
               d_skip, glu_w, glu_b, out_w, bsz, seq):
    d = D_MODEL
    shift, scale, gate = (mod[:, i * d:(i + 1) * d].reshape(bsz, 1, d) for i in range(3))
    o_f = D_MIX + 3 * FOX_WIDTH
    o_u = o_f + FOX_HEADS
    w, ws = _pack_weights(in_w, (
        (0, D_MIX, 1.0),
        (D_MIX, D_MIX + FOX_WIDTH, ATT_SCALE),
        (D_MIX + FOX_WIDTH, o_f, 1.0),
        (o_u, in_w.shape[1], 1.0),
    ), (o_f, o_u))
    proj, small = _inproj(x2, scale, shift, pre_g.reshape(1, d), w, ws, seq)

    negf = _fgate(small, fgate_b, bsz, seq)
    y_c = _fox(proj, negf, bsz, seq)

    y_d = _s5scan(proj.reshape(bsz, seq, OD_N), lam_re, lam_im, log_dt, b_re, b_im, c_re, c_im, d_skip,
                  glu_w, glu_b).reshape(bsz * seq, S5_WIDTH)

    ow = out_w.astype(BF16)
    return _outproj(y_c, y_d, ow[:FOX_WIDTH], ow[FOX_WIDTH:], x2, gate, post_g.reshape(1, d), seq)


def kernel(x, c, ada_w, ada_b, pre_g, post_g, even_in_w, even_conv_w, even_conv_b, even_dt_bias, even_a_log,
           even_d_skip, even_norm_g, even_out_w, odd_in_w, odd_fgate_b, odd_lam_re, odd_lam_im, odd_log_dt,
           odd_b_re, odd_b_im, odd_c_re, odd_c_im, odd_d_skip, odd_glu_w, odd_glu_b, odd_out_w):
    bsz, seq, d = x.shape
    depth = ada_w.shape[0]
    mod = _ada_mod(c, ada_w, ada_b)
    x2 = x.reshape(bsz * seq, d)
    for layer in range(depth):
        i = layer // 2
        if layer % 2 == 0:
            x2 = _even_layer(x2, mod[layer], pre_g[layer], post_g[layer], even_in_w[i], even_conv_w[i],
                             even_conv_b[i], even_dt_bias[i], even_a_log[i], even_d_skip[i], even_norm_g[i],
                             even_out_w[i], bsz, seq)
        else:
            x2 = _odd_layer(x2, mod[layer], pre_g[layer], post_g[layer], odd_in_w[i], odd_fgate_b[i],
                            odd_lam_re[i], odd_lam_im[i], odd_log_dt[i], odd_b_re[i], odd_b_im[i], odd_c_re[i],
                            odd_c_im[i], odd_d_skip[i], odd_glu_w[i], odd_glu_b[i], odd_out_w[i], bsz, seq)
    return x2.reshape(bsz, seq, d)
```

```python
import functools
import math

import jax
import jax.numpy as jnp
from jax import lax
from jax.experimental import pallas as pl
from jax.experimental.pallas import tpu as pltpu

F32 = jnp.float32
BF16 = jnp.bfloat16

D_MODEL = 1024
HEAD_DIM = 64
D_MIX = 2 * D_MODEL
SSD_WIDTH = 1024
SSD_HEADS = 16
SSD_GROUPS = 2
SSD_STATE = 128
SSD_CONV = 4
SSD_CHUNK = 128
SSD_SUB = 4
MOBA_WIDTH = 1024
MOBA_BLOCK = 256
MOBA_TOPK = 3
FOX_WIDTH = 1536
FOX_HEADS = 24
S5_WIDTH = 512
S5_GROUP = 16
S5_GROUP_SHIFT = 4
S5_GROUPS = 32
S5_STATE = 64
S5_STATE_SHIFT = 6
S5_SETS = 4
S5_STEPS = 64
RMS_EPS = 1e-6
ATT_BLOCK = 256
ATT_BLOCK_SHIFT = 8
LOG2E = math.log2(math.e)
ATT_SCALE = LOG2E / math.sqrt(HEAD_DIM)
VT_ROWS = HEAD_DIM + 16
LANES = 128
NEG = -1e30
VMEM_LIMIT = 48 * 1024 * 1024
PACK_ROWS = 512

EV_ZA, EV_ZB, EV_XS, EV_Q, EV_K, EV_V, EV_BC, EV_N = 0, 1024, 2048, 3072, 4096, 5120, 6144, 6656
OD_ZC, OD_ZD, OD_Q, OD_K, OD_V, OD_U, OD_N = 0, 1536, 2048, 3584, 5120, 6656, 7168


def _nt(a, b):
    return lax.dot_general(a, b, (((1,), (1,)), ((), ())), preferred_element_type=F32)


def _dot(a, b):
    return jnp.dot(a, b, preferred_element_type=F32)


def _split3(x):
    hi = x.astype(BF16)
    r = x - hi.astype(F32)
    mid = r.astype(BF16)
    lo = (r - mid.astype(F32)).astype(BF16)
    return hi, mid, lo


def _dot_exact_rhs(x, m_bf16):
    hi, mid, lo = _split3(x)
    return _dot(hi, m_bf16) + _dot(mid, m_bf16) + _dot(lo, m_bf16)


def _sigmoid(x):
    return 0.5 + 0.5 * jnp.tanh(0.5 * x)


def _silu(x):
    h = 0.5 * x
    return h + h * jnp.tanh(h)


def _softplus(x):
    return jnp.maximum(x, 0.0) + jnp.log1p(jnp.exp(-jnp.abs(x)))


def _params(sem, limit=VMEM_LIMIT):
    return pltpu.CompilerParams(dimension_semantics=sem, vmem_limit_bytes=limit)


def _ada_kernel(c_ref, w_ref, b_ref, o_ref):
    cond = _silu(c_ref[...])
    hi, mid, lo = _split3(cond)
    w = w_ref[0]
    whi, wmid, wlo = _split3(w)
    acc = _dot(hi, whi) + _dot(hi, wmid) + _dot(mid, whi)
    acc = acc + _dot(hi, wlo) + _dot(mid, wmid) + _dot(lo, whi)
    o_ref[0] = acc + b_ref[0]


def _ada_mod(c, ada_w, ada_b):
    depth, d, d3 = ada_w.shape
    bsz = c.shape[0]
    nj = d3 // d
    return pl.pallas_call(
        _ada_kernel,
        grid=(depth, nj),
        in_specs=[
            pl.BlockSpec((bsz, d), lambda l, j: (0, 0)),
            pl.BlockSpec((1, d, d), lambda l, j: (l, 0, j)),
            pl.BlockSpec((1, 1, d), lambda l, j: (l, 0, j)),
        ],
        out_specs=pl.BlockSpec((1, bsz, d), lambda l, j: (l, 0, j)),
        out_shape=jax.ShapeDtypeStruct((depth, bsz, d3), F32),
        compiler_params=_params(("parallel", "parallel")),
        name="ada_mod",
    )(c, ada_w, ada_b.reshape(depth, 1, d3))


def _inproj_kernel(x_ref, sc_ref, sh_ref, g_ref, w_ref, ws_ref, o_ref, os_ref, h_scr):
    @pl.when(pl.program_id(1) == 0)
    def _():
        x = x_ref[...]
        ms = jnp.mean(x * x, axis=-1, keepdims=True)
        xn = x * lax.rsqrt(ms + RMS_EPS) * g_ref[...]
        h = xn * (1.0 + sc_ref[0]) + sh_ref[0]
        h_b = h.astype(BF16)
        h_scr[...] = h_b
        r = _nt(h_b, ws_ref[...])
        os_ref[...] = r[:, :LANES] + r[:, LANES:]

    o_ref[...] = _nt(h_scr[...], w_ref[...]).astype(BF16)


def _inproj(x2, scale, shift, g, w, ws, seq, tm=1024):
    t, d = x2.shape
    n = w.shape[0]
    tn = n // 2
    per = seq // tm
    return pl.pallas_call(
        _inproj_kernel,
        grid=(t // tm, n // tn),
        in_specs=[
            pl.BlockSpec((tm, d), lambda i, j: (i, 0)),
            pl.BlockSpec((1, 1, d), lambda i, j: (i // per, 0, 0)),
            pl.BlockSpec((1, 1, d), lambda i, j: (i // per, 0, 0)),
            pl.BlockSpec((1, d), lambda i, j: (0, 0)),
            pl.BlockSpec((tn, d), lambda i, j: (j, 0)),
            pl.BlockSpec((2 * LANES, d), lambda i, j: (0, 0)),
        ],
        out_specs=[
            pl.BlockSpec((tm, tn), lambda i, j: (i, j)),
            pl.BlockSpec((tm, LANES), lambda i, j: (i, 0)),
        ],
        out_shape=[
            jax.ShapeDtypeStruct((t, n), BF16),
            jax.ShapeDtypeStruct((t, LANES), F32),
        ],
        scratch_shapes=[pltpu.VMEM((tm, d), BF16)],
        compiler_params=_params(("parallel", "arbitrary")),
        name="in_proj",
    )(x2, scale, shift, g, w, ws)


def _outproj_kernel(a_ref, b_ref, wa_ref, wb_ref, x_ref, gate_ref, pg_ref, o_ref):
    y = _dot(a_ref[...], wa_ref[...]) + _dot(b_ref[...], wb_ref[...])
    ms = jnp.mean(y * y, axis=-1, keepdims=True)
    yn = y * lax.rsqrt(ms + RMS_EPS) * pg_ref[...]
    o_ref[...] = x_ref[...] + gate_ref[0] * yn


def _outproj(a, b, wa, wb, x2, gate, pg, seq, tm=1024):
    t, d = x2.shape
    ka, kb = a.shape[1], b.shape[1]
    per = seq // tm
    return pl.pallas_call(
        _outproj_kernel,
        grid=(t // tm,),
        in_specs=[
            pl.BlockSpec((tm, ka), lambda i: (i, 0)),
            pl.BlockSpec((tm, kb), lambda i: (i, 0)),
            pl.BlockSpec((ka, d), lambda i: (0, 0)),
            pl.BlockSpec((kb, d), lambda i: (0, 0)),
            pl.BlockSpec((tm, d), lambda i: (i, 0)),
            pl.BlockSpec((1, 1, d), lambda i: (i // per, 0, 0)),
            pl.BlockSpec((1, d), lambda i: (0, 0)),
        ],
        out_specs=pl.BlockSpec((tm, d), lambda i: (i, 0)),
        out_shape=jax.ShapeDtypeStruct((t, d), F32),
        compiler_params=_params(("parallel",)),
        name="out_proj",
    )(a, b, wa, wb, x2, gate, pg)


def _causal_conv_silu(raw, prev, shift, w, b):
    lc = raw.shape[0]
    ext = jnp.concatenate([prev, raw], axis=0)
    shifted = _dot(shift, ext)
    acc = b + w[SSD_CONV - 1:SSD_CONV, :] * raw.astype(F32)
    for j in range(SSD_CONV - 1):
        acc = acc + w[j:j + 1, :] * shifted[j * lc:(j + 1) * lc, :]
    return _silu(acc)


def _ssd_kernel(z_ref, xs_ref, bc_ref, xsp_ref, bcp_ref, dt_ref, cwx_ref, cwb_ref, cbx_ref, cbb_ref, dtb_ref,
                alog_ref, dexp_ref, ng_ref, e_ref, sh_ref, o_ref, state_scr):
    lc = SSD_CHUNK
    hw = SSD_HEADS // SSD_GROUPS * HEAD_DIM
    first = pl.program_id(1) == 0

    @pl.when(first)
    def _():
        state_scr[...] = jnp.zeros_like(state_scr)

    rows = lax.broadcasted_iota(jnp.int32, (lc, lc), 0)
    cols = lax.broadcasted_iota(jnp.int32, (lc, lc), 1)
    lower = rows >= cols
    state = [state_scr[:, g * hw:(g + 1) * hw] for g in range(SSD_GROUPS)]
    prev_x = jnp.where(first, jnp.zeros_like(xsp_ref[...]), xsp_ref[...])
    prev_b = jnp.where(first, jnp.zeros_like(bcp_ref[...]), bcp_ref[...])
    for j in range(xs_ref.shape[0] // lc):
        sub = slice(j * lc, (j + 1) * lc)
        raw_x = xs_ref[sub, :]
        raw_b = bc_ref[sub, :]
        y, state = _ssd_chunk(raw_x, raw_b, prev_x, prev_b, dt_ref[sub, :], z_ref[sub, :], state, lower,
                              cwx_ref, cwb_ref, cbx_ref, cbb_ref, dtb_ref, alog_ref, dexp_ref, e_ref, sh_ref)
        ms = jnp.mean(y * y, axis=-1, keepdims=True)
        o_ref[sub, :] = (y * lax.rsqrt(ms + RMS_EPS) * ng_ref[...]).astype(BF16)
        prev_x, prev_b = raw_x, raw_b
    for g in range(SSD_GROUPS):
        state_scr[:, g * hw:(g + 1) * hw] = state[g]


def _ssd_chunk(raw_x, raw_b, prev_x, prev_b, dt_raw, z, state, lower,
               cwx_ref, cwb_ref, cbx_ref, cbb_ref, dtb_ref, alog_ref, dexp_ref, e_ref, sh_ref):
    lc = SSD_CHUNK
    hw = SSD_HEADS // SSD_GROUPS * HEAD_DIM
    xs = _causal_conv_silu(raw_x, prev_x, sh_ref[...], cwx_ref[...], cbx_ref[...])
    bc = _causal_conv_silu(raw_b, prev_b, sh_ref[...], cwb_ref[...], cbb_ref[...])

    dt = _softplus(dt_raw + dtb_ref[...])
    adt = dt * (-jnp.exp(alog_ref[...]))
    a_cum = _dot_exact_rhs_lhs(lower.astype(BF16), adt)
    a_cum_t = a_cum.T
    a_last = a_cum[lc - 1:lc, :]
    stacked = jnp.concatenate([dt, jnp.exp(a_last - a_cum), jnp.exp(a_cum)], axis=0)
    expanded = _dot(stacked.astype(BF16), e_ref[...])
    dt_e = expanded[0:lc]
    dec_e = expanded[lc:2 * lc]
    ea_e = expanded[2 * lc:3 * lc]
    al_e = _dot_exact_rhs(jnp.broadcast_to(jnp.exp(a_last), (8, LANES)), e_ref[...])[0:1]

    xdt = xs * dt_e
    xdt_b = xdt.astype(BF16)
    xd_b = (xdt * dec_e).astype(BF16)
    lane = lax.broadcasted_iota(jnp.int32, (1, LANES), 1)
    pieces = []
    new_state = []
    for g in range(SSD_GROUPS):
        bg = bc[:, g * SSD_STATE:(g + 1) * SSD_STATE]
        cg = bc[:, (SSD_GROUPS + g) * SSD_STATE:(SSD_GROUPS + g + 1) * SSD_STATE]
        cg_b = cg.astype(BF16)
        cb = _nt(cg_b, bg.astype(BF16))
        st = state[g]
        y_off = _dot(cg_b, st.astype(BF16)) * ea_e[:, g * hw:(g + 1) * hw]
        new_st = _dot(bg.T.astype(BF16), xd_b[:, g * hw:(g + 1) * hw])
        new_state.append(al_e[:, g * hw:(g + 1) * hw] * st + new_st)
        for pair in range(SSD_HEADS // SSD_GROUPS // 2):
            acc = None
            c0 = g * hw + pair * LANES
            xp = xdt_b[:, c0:c0 + LANES]
            for half in range(2):
                h = g * (SSD_HEADS // SSD_GROUPS) + pair * 2 + half
                diff = a_cum[:, h:h + 1] - a_cum_t[h:h + 1, :]
                decay = jnp.exp(jnp.where(lower, diff, NEG))
                m = (cb * decay).astype(BF16)
                in_half = (lane >= half * HEAD_DIM) & (lane < (half + 1) * HEAD_DIM)
                part = _dot(m, jnp.where(in_half, xp, jnp.zeros_like(xp)))
                acc = part if acc is None else acc + part
            pieces.append(acc + y_off[:, pair * LANES:(pair + 1) * LANES])
    y = jnp.concatenate(pieces, axis=1) + dexp_ref[...] * xs
    return y * _silu(z.astype(F32)), new_state


def _dot_exact_rhs_lhs(m_bf16, x):
    hi, mid, lo = _split3(x)
    return _dot(m_bf16, hi) + _dot(m_bf16, mid) + _dot(m_bf16, lo)


def _ssd(proj, small, conv_w, conv_b, dt_bias, a_log, d_skip, norm_g, bsz, seq):
    lc = SSD_CHUNK
    nc = seq // lc
    t = bsz * seq
    pad = LANES - SSD_HEADS
    dtb = jnp.pad(dt_bias, (0, pad)).reshape(1, LANES)
    alog = jnp.pad(a_log, (0, pad)).reshape(1, LANES)
    dexp = jnp.repeat(d_skip, HEAD_DIM).reshape(1, SSD_WIDTH)
    expand = (jnp.arange(LANES)[:, None] == (jnp.arange(SSD_WIDTH) // HEAD_DIM)[None, :]).astype(BF16)
    nbc = 2 * SSD_GROUPS * SSD_STATE
    src = lc + jnp.arange(lc)[None, :] - (SSD_CONV - 1) + jnp.arange(SSD_CONV - 1)[:, None]
    shift = (src.reshape(-1, 1) == jnp.arange(2 * lc)[None, :]).astype(BF16)
    ns = nc // SSD_SUB
    rs = SSD_SUB * lc
    row = lambda b, c: b * ns + c
    prev = lambda b, c: b * nc + jnp.maximum(SSD_SUB * c - 1, 0)
    const = lambda b, c: (0, 0)
    return pl.pallas_call(
        _ssd_kernel,
        grid=(bsz, ns),
        in_specs=[
            pl.BlockSpec((rs, SSD_WIDTH), lambda b, c: (row(b, c), EV_ZA // SSD_WIDTH)),
            pl.BlockSpec((rs, SSD_WIDTH), lambda b, c: (row(b, c), EV_XS // SSD_WIDTH)),
            pl.BlockSpec((rs, nbc), lambda b, c: (row(b, c), EV_BC // nbc)),
            pl.BlockSpec((lc, SSD_WIDTH), lambda b, c: (prev(b, c), EV_XS // SSD_WIDTH)),
            pl.BlockSpec((lc, nbc), lambda b, c: (prev(b, c), EV_BC // nbc)),
            pl.BlockSpec((rs, LANES), lambda b, c: (row(b, c), 0)),
            pl.BlockSpec((SSD_CONV, SSD_WIDTH), const),
            pl.BlockSpec((SSD_CONV, nbc), const),
            pl.BlockSpec((1, SSD_WIDTH), const),
            pl.BlockSpec((1, nbc), const),
            pl.BlockSpec((1, LANES), const),
            pl.BlockSpec((1, LANES), const),
            pl.BlockSpec((1, SSD_WIDTH), const),
            pl.BlockSpec((1, SSD_WIDTH), const),
            pl.BlockSpec((LANES, SSD_WIDTH), const),
            pl.BlockSpec(((SSD_CONV - 1) * lc, 2 * lc), const),
        ],
        out_specs=pl.BlockSpec((rs, SSD_WIDTH), lambda b, c: (row(b, c), 0)),
        out_shape=jax.ShapeDtypeStruct((t, SSD_WIDTH), BF16),
        scratch_shapes=[pltpu.VMEM((SSD_STATE, SSD_WIDTH), F32)],
        compiler_params=_params(("parallel", "arbitrary")),
        name="ssd",
    )(proj, proj, proj, proj, proj, small,
      conv_w[:, :SSD_WIDTH], conv_w[:, SSD_WIDTH:], conv_b[:SSD_WIDTH].reshape(1, -1),
      conv_b[SSD_WIDTH:].reshape(1, -1), dtb, alog, dexp, norm_g.reshape(1, -1), expand, shift)


def _fold_rows(x, op):
    out = x[0:8, :]
    for i in range(1, x.shape[0] // 8):
        out = op(out, x[8 * i:8 * (i + 1), :])
    return out


def _attend_pair(qa_scr, ka_scr, vt_scr, s_scr, p_scr, ot_scr, bias_fn):
    blk = ATT_BLOCK
    nb = qa_scr.shape[1] // blk
    heads = (0, 1)
    causal = (lax.broadcasted_iota(jnp.int32, (blk, blk), 0)
              <= lax.broadcasted_iota(jnp.int32, (blk, blk), 1))

    def score_tile(half, qb, n, m8):
        s = _nt(ka_scr[half, n * blk:(n + 1) * blk, :], qa_scr[half, qb * blk:(qb + 1) * blk, :])
        bias = bias_fn(half, qb, n)
        if bias is not None:
            s = s + bias
        if n == qb:
            s = jnp.where(causal, s, NEG)
        s_scr[half, qb % 2, n] = s
        part = _fold_rows(s, jnp.maximum)
        return part if m8 is None else jnp.maximum(m8, part)

    def prob_tile(half, qb, n, m):
        p = jnp.exp2(s_scr[half, qb % 2, n] - m)
        p_scr[half, qb % 2, n * blk:(n + 1) * blk, :] = p.astype(BF16)

    m8 = [score_tile(half, 0, 0, None) for half in heads]
    for qb in range(nb):
        m = [jnp.max(m8[half], axis=0, keepdims=True) for half in heads]
        nxt = qb + 1
        m8 = [None, None]
        for n in range(nxt + 1):
            for half in heads:
                if n <= qb:
                    prob_tile(half, qb, n, m[half])
            for half in heads:
                if nxt < nb:
                    m8[half] = score_tile(half, nxt, n, m8[half])
        keys = (qb + 1) * blk
        for half in heads:
            acc = _dot(vt_scr[half, :, 0:keys], p_scr[half, qb % 2, 0:keys, :])
            out = acc[0:HEAD_DIM] / acc[HEAD_DIM:HEAD_DIM + 1]
            ot_scr[half * HEAD_DIM:(half + 1) * HEAD_DIM, qb * blk:(qb + 1) * blk] = out


def _store_vt(v_ref, vt_scr):
    ones = jnp.ones((VT_ROWS - HEAD_DIM, ATT_BLOCK), BF16)
    for n in range(v_ref.shape[0] // ATT_BLOCK):
        cols = slice(n * ATT_BLOCK, (n + 1) * ATT_BLOCK)
        vt = v_ref[cols, :].astype(F32).T.astype(BF16)
        for half in range(2):
            vt_scr[half, 0:HEAD_DIM, cols] = vt[half * HEAD_DIM:(half + 1) * HEAD_DIM, :]
            vt_scr[half, HEAD_DIM:VT_ROWS, cols] = ones


def _gated_output(ot_scr, z_ref, o_ref):
    for i in range(ot_scr.shape[1] // ATT_BLOCK):
        rows = slice(i * ATT_BLOCK, (i + 1) * ATT_BLOCK)
        o = ot_scr[:, rows].T
        o_ref[rows, :] = (o * _silu(z_ref[rows, :].astype(F32))).astype(BF16)


def _head_lanes(half):
    lane = lax.broadcasted_iota(jnp.int32, (1, LANES), 1)
    return (lane >= half * HEAD_DIM) & (lane < (half + 1) * HEAD_DIM)


def _moba_kernel(q_ref, k_ref, v_ref, z_ref, o_ref, vt_scr, kbar_scr, bias_scr, qa_scr, ka_scr, s_scr, p_scr,
                 ot_scr):
    blk = ATT_BLOCK
    seq = q_ref.shape[0]
    nb = seq // blk
    _store_vt(v_ref, vt_scr)
    for n in range(nb):
        kb = k_ref[n * blk:(n + 1) * blk, :].astype(F32)
        kbar_scr[n:n + 1, :] = jnp.mean(kb, axis=0, keepdims=True)
    n_idx = lax.broadcasted_iota(jnp.int32, (nb, seq), 0)
    q_blk = lax.broadcasted_iota(jnp.int32, (nb, seq), 1) >> ATT_BLOCK_SHIFT

    for half in range(2):
        in_half = _head_lanes(half)
        q = q_ref[...]
        qi = jnp.where(in_half, q, jnp.zeros_like(q))
        qa_scr[half] = qi
        ka_scr[half] = k_ref[...]
        kbar = jnp.where(in_half, kbar_scr[...], 0.0)
        kb_hi = kbar.astype(BF16)
        kb_lo = (kbar - kb_hi.astype(F32)).astype(BF16)
        gate = _nt(kb_hi, qi) + _nt(kb_lo, qi)
        rank = jnp.zeros((nb, seq), F32)
        for mth in range(nb):
            gm = gate[mth:mth + 1, :]
            beats = (gm > gate) | ((gm == gate) & (mth < n_idx))
            rank = rank + jnp.where(beats & (mth < q_blk), 1.0, 0.0)
        chosen = (rank < float(MOBA_TOPK)) & (n_idx < q_blk)
        bias_scr[half] = jnp.where(chosen, 0.0, NEG)

    def bias_fn(half, qb, n):
        if n == qb or qb <= MOBA_TOPK:
            return None
        return bias_scr[half, n:n + 1, qb * blk:(qb + 1) * blk]

    _attend_pair(qa_scr, ka_scr, vt_scr, s_scr, p_scr, ot_scr, bias_fn)
    _gated_output(ot_scr, z_ref, o_ref)


def _moba(proj, bsz, seq):
    blk = ATT_BLOCK
    nb = seq // blk
    pairs = MOBA_WIDTH // LANES
    t = bsz * seq
    return pl.pallas_call(
        _moba_kernel,
        grid=(bsz, pairs),
        in_specs=[
            pl.BlockSpec((seq, LANES), lambda b, p: (b, EV_Q // LANES + p)),
            pl.BlockSpec((seq, LANES), lambda b, p: (b, EV_K // LANES + p)),
            pl.BlockSpec((seq, LANES), lambda b, p: (b, EV_V // LANES + p)),
            pl.BlockSpec((seq, LANES), lambda b, p: (b, EV_ZB // LANES + p)),
        ],
        out_specs=pl.BlockSpec((seq, LANES), lambda b, p: (b, p)),
        out_shape=jax.ShapeDtypeStruct((t, MOBA_WIDTH), BF16),
        scratch_shapes=[
            pltpu.VMEM((2, VT_ROWS, seq), BF16),
            pltpu.VMEM((nb, LANES), F32),
            pltpu.VMEM((2, nb, seq), F32),
            pltpu.VMEM((2, seq, LANES), BF16),
            pltpu.VMEM((2, seq, LANES), BF16),
            pltpu.VMEM((2, 2, nb, blk, blk), F32),
            pltpu.VMEM((2, 2, seq, blk), BF16),
            pltpu.VMEM((LANES, seq), F32),
        ],
        compiler_params=_params(("parallel", "parallel")),
        name="moba",
    )(proj, proj, proj, proj)


def _fgate_kernel(f_ref, fb_ref, o_ref):
    blk = ATT_BLOCK
    rows = lax.broadcasted_iota(jnp.int32, (blk, blk), 0)
    cols = lax.broadcasted_iota(jnp.int32, (blk, blk), 1)
    lower = (rows >= cols).astype(BF16)
    carry = jnp.zeros((1, LANES), F32)
    for i in range(f_ref.shape[0] // blk):
        nlf = _softplus(-(f_ref[i * blk:(i + 1) * blk, :] + fb_ref[...]))
        csum = _dot_exact_rhs_lhs(lower, nlf) + carry
        o_ref[i * blk:(i + 1) * blk, :] = csum
        carry = csum[blk - 1:blk, :]


def _fgate(small, fgate_b, bsz, seq):
    fb = jnp.pad(fgate_b, (0, LANES - FOX_HEADS)).reshape(1, LANES)
    return pl.pallas_call(
        _fgate_kernel,
        grid=(bsz,),
        in_specs=[
            pl.BlockSpec((seq, LANES), lambda b: (b, 0)),
            pl.BlockSpec((1, LANES), lambda b: (0, 0)),
        ],
        out_specs=pl.BlockSpec((seq, LANES), lambda b: (b, 0)),
        out_shape=jax.ShapeDtypeStruct((bsz * seq, LANES), F32),
        compiler_params=_params(("parallel",)),
        name="fox_gate",
    )(small, fb)


def _fox_kernel(q_ref, k_ref, v_ref, z_ref, nf_ref, o_ref, vt_scr, fb_scr, qa_scr, ka_scr, s_scr, p_scr, ot_scr):
    pair = pl.program_id(1)
    nb = q_ref.shape[0] // ATT_BLOCK
    blk = ATT_BLOCK
    _store_vt(v_ref, vt_scr)
    lane = lax.broadcasted_iota(jnp.int32, (1, LANES), 1)
    for half in range(2):
        own_lane = lane == pair * 2 + half
        for n in range(nb):
            col = jnp.sum(jnp.where(own_lane, nf_ref[n * blk:(n + 1) * blk, :], 0.0), axis=1, keepdims=True)
            fb_scr[half, n] = jnp.broadcast_to(col * LOG2E, (blk, LANES))
        q = q_ref[...]
        qa_scr[half] = jnp.where(_head_lanes(half), q, jnp.zeros_like(q))
        ka_scr[half] = k_ref[...]

    def bias_fn(half, qb, n):
        fb = fb_scr[half, n]
        return jnp.concatenate([fb, fb], axis=1)

    _attend_pair(qa_scr, ka_scr, vt_scr, s_scr, p_scr, ot_scr, bias_fn)
    _gated_output(ot_scr, z_ref, o_ref)


def _fox(proj, negf, bsz, seq):
    blk = ATT_BLOCK
    nb = seq // blk
    pairs = FOX_WIDTH // LANES
    t = bsz * seq
    return pl.pallas_call(
        _fox_kernel,
        grid=(bsz, pairs),
        in_specs=[
            pl.BlockSpec((seq, LANES), lambda b, p: (b, OD_Q // LANES + p)),
            pl.BlockSpec((seq, LANES), lambda b, p: (b, OD_K // LANES + p)),
            pl.BlockSpec((seq, LANES), lambda b, p: (b, OD_V // LANES + p)),
            pl.BlockSpec((seq, LANES), lambda b, p: (b, OD_ZC // LANES + p)),
            pl.BlockSpec((seq, LANES), lambda b, p: (b, 0)),
        ],
        out_specs=pl.BlockSpec((seq, LANES), lambda b, p: (b, p)),
        out_shape=jax.ShapeDtypeStruct((t, FOX_WIDTH), BF16),
        scratch_shapes=[
            pltpu.VMEM((2, VT_ROWS, seq), BF16),
            pltpu.VMEM((2, nb, blk, LANES), F32),
            pltpu.VMEM((2, seq, LANES), BF16),
            pltpu.VMEM((2, seq, LANES), BF16),
            pltpu.VMEM((2, 2, nb, blk, blk), F32),
            pltpu.VMEM((2, 2, seq, blk), BF16),
            pltpu.VMEM((LANES, seq), F32),
        ],
        compiler_params=_params(("parallel", "parallel")),
        name="fox",
    )(proj, proj, proj, proj, negf)


def _s5scan_kernel(u_ref, z_ref, perm_ref, permt_ref, lr_ref, li_ref, ldt_ref, bwr_ref, bwi_ref, cwr_ref, cwi_ref,
                   d_ref, gw_ref, gb_ref, y_ref, bre_scr, bim_scr, cre_scr, cim_scr, ar_scr, ai_scr, xr_scr, xi_scr, zr_scr, zi_scr,
                   *, bsz):
    nset, cw, sw = bre_scr.shape

    @pl.when(pl.program_id(0) == 0)
    def _():
        chan_grp = lax.broadcasted_iota(jnp.int32, (cw, sw), 0) >> S5_GROUP_SHIFT
        state_grp = lax.broadcasted_iota(jnp.int32, (cw, sw), 1) >> S5_STATE_SHIFT
        same_b = chan_grp == state_grp
        same_c = ((lax.broadcasted_iota(jnp.int32, (sw, cw), 0) >> S5_STATE_SHIFT)
                  == (lax.broadcasted_iota(jnp.int32, (sw, cw), 1) >> S5_GROUP_SHIFT))
        for s in range(nset):
            lr = lr_ref[s]
            li = li_ref[s]
            dt = jnp.exp(ldt_ref[s])
            mag = jnp.exp(lr * dt)
            ar = mag * jnp.cos(li * dt)
            ai = mag * jnp.sin(li * dt)
            den = lr * lr + li * li
            qr = ((ar - 1.0) * lr + ai * li) / den
            qi = (ai * lr - (ar - 1.0) * li) / den
            ar_scr[s] = jnp.broadcast_to(ar, (bsz, sw))
            ai_scr[s] = jnp.broadcast_to(ai, (bsz, sw))
            bwr = bwr_ref[s]
            bwi = bwi_ref[s]
            bre_scr[s] = jnp.where(same_b, qr * bwr - qi * bwi, 0.0).astype(BF16)
            bim_scr[s] = jnp.where(same_b, qr * bwi + qi * bwr, 0.0).astype(BF16)
            cre_scr[s] = jnp.where(same_c, cwr_ref[s], 0.0).astype(BF16)
            cim_scr[s] = jnp.where(same_c, cwi_ref[s], 0.0).astype(BF16)
        xr_scr[...] = jnp.zeros_like(xr_scr)
        xi_scr[...] = jnp.zeros_like(xi_scr)

    steps = u_ref.shape[1]
    width = u_ref.shape[2]
    u_tb = _dot(perm_ref[...], u_ref[...].reshape(bsz * steps, width))
    u_b = u_tb.astype(BF16)
    half_rows = (steps // 2) * bsz

    def project(half, s):
        rows = slice(half * half_rows, (half + 1) * half_rows)
        us = u_b[rows, s * cw:(s + 1) * cw]
        zr_scr[s, rows, :] = _dot(us, bre_scr[s])
        zi_scr[s, rows, :] = _dot(us, bim_scr[s])

    for s in range(nset):
        project(0, s)

    def step(t, carry):
        r0 = t * bsz
        out = []
        for s in range(nset):
            xr, xi = carry[2 * s], carry[2 * s + 1]
            ar = ar_scr[s]
            ai = ai_scr[s]
            nr = ar * xr - ai * xi + zr_scr[s, r0:r0 + bsz, :]
            ni = ar * xi + ai * xr + zi_scr[s, r0:r0 + bsz, :]
            zr_scr[s, r0:r0 + bsz, :] = nr
            zi_scr[s, r0:r0 + bsz, :] = ni
            out += [nr, ni]
        return out

    def readout(half, s):
        rows = slice(half * half_rows, (half + 1) * half_rows)
        return _dot(zr_scr[s, rows, :].astype(BF16), cre_scr[s]) - _dot(zi_scr[s, rows, :].astype(BF16), cim_scr[s])

    every = steps // 2 // nset
    carry = []
    for s in range(nset):
        carry += [xr_scr[s], xi_scr[s]]
    for t in range(steps // 2):
        carry = step(t, carry)
        if t % every == 0:
            project(1, t // every)
    first_half = []
    for t in range(steps // 2, steps):
        carry = step(t, carry)
        k = t - steps // 2
        if k % every == 0:
            first_half.append(readout(0, k // every))
    for s in range(nset):
        xr_scr[s] = carry[2 * s]
        xi_scr[s] = carry[2 * s + 1]
    second_half = [readout(1, s) for s in range(nset)]
    xc = jnp.concatenate([jnp.concatenate(first_half, axis=1), jnp.concatenate(second_half, axis=1)], axis=0)
    y = (xc + d_ref[...] * u_tb).astype(BF16)
    y = _dot(permt_ref[...], y)
    y = 0.5 * y * (1.0 + jnp.tanh(math.sqrt(2.0 / math.pi) * (y + 0.044715 * (y * y * y))))
    y = y * _sigmoid(_dot(y.astype(BF16), gw_ref[...]) + gb_ref[...])
    z = z_ref[...].reshape(bsz * steps, width).astype(F32)
    y_ref[...] = (y * _silu(z)).astype(BF16).reshape(bsz, steps, width)


def _s5scan(proj3, lam_re, lam_im, log_dt, b_re, b_im, c_re, c_im, d_skip, glu_w, glu_b):
    bsz, seq, _ = proj3.shape
    width = S5_WIDTH
    nset = S5_SETS
    cw = width // nset
    gs = S5_GROUPS // nset
    sw = gs * S5_STATE
    blk = S5_STEPS * bsz
    tb = jnp.arange(blk)
    perm = ((tb % bsz) * S5_STEPS + tb // bsz)[:, None] == jnp.arange(blk)[None, :]
    vec = lambda a: a.reshape(nset, 1, sw)
    bw = lambda b: jnp.tile(jnp.swapaxes(b, 1, 2).reshape(nset, cw, S5_STATE), (1, 1, gs))
    cw_t = lambda c: jnp.tile(
        jnp.swapaxes(c, 1, 2).reshape(nset, gs, S5_STATE, S5_GROUP).transpose(0, 2, 1, 3).reshape(nset, S5_STATE, cw),
        (1, gs, 1))
    full3 = lambda a, b, c: pl.BlockSpec((a, b, c), lambda i: (0, 0, 0))
    return pl.pallas_call(
        functools.partial(_s5scan_kernel, bsz=bsz),
        grid=(seq // S5_STEPS,),
        in_specs=[
            pl.BlockSpec((bsz, S5_STEPS, width), lambda i: (0, i, OD_U // width)),
            pl.BlockSpec((bsz, S5_STEPS, width), lambda i: (0, i, OD_ZD // width)),
            pl.BlockSpec((blk, blk), lambda i: (0, 0)),
            pl.BlockSpec((blk, blk), lambda i: (0, 0)),
            full3(nset, 1, sw), full3(nset, 1, sw), full3(nset, 1, sw),
            full3(nset, cw, sw), full3(nset, cw, sw),
            full3(nset, sw, cw), full3(nset, sw, cw),
            pl.BlockSpec((1, width), lambda i: (0, 0)),
            pl.BlockSpec((width, width), lambda i: (0, 0)),
            pl.BlockSpec((1, width), lambda i: (0, 0)),
        ],
        out_specs=pl.BlockSpec((bsz, S5_STEPS, width), lambda i: (0, i, 0)),
        out_shape=jax.ShapeDtypeStruct((bsz, seq, width), BF16),
        scratch_shapes=[
            pltpu.VMEM((nset, cw, sw), BF16), pltpu.VMEM((nset, cw, sw), BF16),
            pltpu.VMEM((nset, sw, cw), BF16), pltpu.VMEM((nset, sw, cw), BF16),
            pltpu.VMEM((nset, bsz, sw), F32), pltpu.VMEM((nset, bsz, sw), F32),
            pltpu.VMEM((nset, bsz, sw), F32), pltpu.VMEM((nset, bsz, sw), F32),
            pltpu.VMEM((nset, blk, sw), F32), pltpu.VMEM((nset, blk, sw), F32),
        ],
        compiler_params=_params(("arbitrary",)),
        name="s5_scan",
    )(proj3, proj3, perm.astype(BF16), perm.T.astype(BF16), vec(lam_re), vec(lam_im),
      vec(jnp.repeat(log_dt, S5_STATE)), bw(b_re), bw(b_im), cw_t(c_re), cw_t(c_im), d_skip.reshape(1, width),
      glu_w.astype(BF16), glu_b.reshape(1, width))


def _pack_weights(in_w, pieces, small):
    wt = in_w.T
    d = wt.shape[1]
    starts, scales = [], []
    for a, b, scale in pieces:
        starts += list(range(a, b, PACK_ROWS))
        scales += [scale] * ((b - a) // PACK_ROWS)

    def pick(c, table):
        out = table[-1]
        for i in range(len(table) - 2, -1, -1):
            out = jnp.where(c == i, table[i], out)
        return out

    def pack_kernel(w_ref, o_ref):
        scale = pick(pl.program_id(0), [jnp.float32(s) for s in scales])
        o_ref[...] = (w_ref[...] * scale).astype(BF16)

    main = pl.pallas_call(
        pack_kernel,
        grid=(len(starts),),
        in_specs=[pl.BlockSpec((pl.Element(PACK_ROWS), pl.Element(d)),
                               lambda c: (pl.multiple_of(pick(c, [s // 8 for s in starts]) * 8, 8), 0))],
        out_specs=pl.BlockSpec((PACK_ROWS, d), lambda c: (c, 0)),
        out_shape=jax.ShapeDtypeStruct((len(starts) * PACK_ROWS, d), BF16),
        compiler_params=_params(("parallel",)),
        name="pack_w",
    )(wt)
    a, b = small
    rows = jnp.pad(wt[a:b], ((0, LANES - (b - a)), (0, 0)))
    hi = rows.astype(BF16)
    lo = (rows - hi.astype(F32)).astype(BF16)
    return main, jnp.concatenate([hi, lo], axis=0)


def _even_layer(x2, mod, pre_g, post_g, in_w, conv_w, conv_b, dt_bias, a_log, d_skip, norm_g, out_w, bsz, seq):
    d = D_MODEL
    shift, scale, gate = (mod[:, i * d:(i + 1) * d].reshape(bsz, 1, d) for i in range(3))
    o_xbc = 2 * SSD_WIDTH
    o_dt = o_xbc + SSD_WIDTH + 2 * SSD_GROUPS * SSD_STATE
    o_q = o_dt + SSD_HEADS
    w, ws = _pack_weights(in_w, (
        (0, o_xbc + SSD_WIDTH, 1.0),
        (o_q, o_q + MOBA_WIDTH, ATT_SCALE),
        (o_q + MOBA_WIDTH, in_w.shape[1], 1.0),
        (o_xbc + SSD_WIDTH, o_dt, 1.0),
    ), (o_dt, o_q))
    proj, small = _inproj(x2, scale, shift, pre_g.reshape(1, d), w, ws, seq)
    y_a = _ssd(proj, small, conv_w, conv_b, dt_bias, a_log, d_skip, norm_g, bsz, seq)
    y_b = _moba(proj, bsz, seq)
    ow = out_w.astype(BF16)
    return _outproj(y_a, y_b, ow[:SSD_WIDTH], ow[SSD_WIDTH:], x2, gate, post_g.reshape(1, d), seq)


def _odd_layer(x2, mod, pre_g, post_g, in_w, fgate_b, lam_re, lam_im, log_dt, b_re, b_im, c_re, c_im,
               d_skip, glu_w, glu_b, out_w, bsz, seq):
    d = D_MODEL
    shift, scale, gate = (mod[:, i * d:(i + 1) * d].reshape(bsz, 1, d) for i in range(3))
    o_f = D_MIX + 3 * FOX_WIDTH
    o_u = o_f + FOX_HEADS
    w, ws = _pack_weights(in_w, (
        (0, D_MIX, 1.0),
        (D_MIX, D_MIX + FOX_WIDTH, ATT_SCALE),
        (D_MIX + FOX_WIDTH, o_f, 1.0),
        (o_u, in_w.shape[1], 1.0),
    ), (o_f, o_u))
    proj, small = _inproj(x2, scale, shift, pre_g.reshape(1, d), w, ws, seq)

    negf = _fgate(small, fgate_b, bsz, seq)
    y_c = _fox(proj, negf, bsz, seq)

    y_d = _s5scan(proj.reshape(bsz, seq, OD_N), lam_re, lam_im, log_dt, b_re, b_im, c_re, c_im, d_skip,
                  glu_w, glu_b).reshape(bsz * seq, S5_WIDTH)

    ow = out_w.astype(BF16)
    return _outproj(y_c, y_d, ow[:FOX_WIDTH], ow[FOX_WIDTH:], x2, gate, post_g.reshape(1, d), seq)


def kernel(x, c, ada_w, ada_b, pre_g, post_g, even_in_w, even_conv_w, even_conv_b, even_dt_bias, even_a_log,
           even_d_skip, even_norm_g, even_out_w, odd_in_w, odd_fgate_b, odd_lam_re, odd_lam_im, odd_log_dt,
           odd_b_re, odd_b_im, odd_c_re, odd_c_im, odd_d_skip, odd_glu_w, odd_glu_b, odd_out_w):
    bsz, seq, d = x.shape
    depth = ada_w.shape[0]
    mod = _ada_mod(c, ada_w, ada_b)
    x2 = x.reshape(bsz * seq, d)
    for layer in range(depth):
        i = layer // 2
        if layer % 2 == 0:
            x2 = _even_layer(x2, mod[layer], pre_g[layer], post_g[layer], even_in_w[i], even_conv_w[i],
                             even_conv_b[i], even_dt_bias[i], even_a_log[i], even_d_skip[i], even_norm_g[i],
                             even_out_w[i], bsz, seq)
        else:
            x2 = _odd_layer(x2, mod[layer], pre_g[layer], post_g[layer], odd_in_w[i], odd_fgate_b[i],
                            odd_lam_re[i], odd_lam_im[i], odd_log_dt[i], odd_b_re[i], odd_b_im[i], odd_c_re[i],
                            odd_c_im[i], odd_d_skip[i], odd_glu_w[i], odd_glu_b[i], odd_out_w[i], bsz, seq)
    return x2.reshape(bsz, seq, d)
```

```python
import functools
import math

import jax
import jax.numpy as jnp
from jax import lax
from jax.experimental import pallas as pl
from jax.experimental.pallas import tpu as pltpu

F32 = jnp.float32
BF16 = jnp.bfloat16

D_MODEL = 1024
HEAD_DIM = 64
D_MIX = 2 * D_MODEL
SSD_WIDTH = 1024
SSD_HEADS = 16
SSD_GROUPS = 2
SSD_STATE = 128
SSD_CONV = 4
SSD_CHUNK = 128
SSD_SUB = 4
MOBA_WIDTH = 1024
MOBA_BLOCK = 256
MOBA_TOPK = 3
FOX_WIDTH = 1536
FOX_HEADS = 24
S5_WIDTH = 512
S5_GROUP = 16
S5_GROUP_SHIFT = 4
S5_GROUPS = 32
S5_STATE = 64
S5_STATE_SHIFT = 6
S5_SETS = 4
S5_STEPS = 64
RMS_EPS = 1e-6
ATT_BLOCK = MOBA_BLOCK
ATT_BLOCK_SHIFT = 8
LOG2E = math.log2(math.e)
ATT_SCALE = LOG2E / math.sqrt(HEAD_DIM)
VT_ROWS = HEAD_DIM + 16
LANES = 128
NEG = -1e30
VMEM_LIMIT = 48 * 1024 * 1024
PACK_ROWS = 512

EV_ZA, EV_ZB, EV_XS, EV_Q, EV_K, EV_V, EV_BC = 0, 1024, 2048, 3072, 4096, 5120, 6144
OD_ZC, OD_ZD, OD_Q, OD_K, OD_V, OD_U, OD_N = 0, 1536, 2048, 3584, 5120, 6656, 7168


def _nt(a, b):
    return lax.dot_general(a, b, (((1,), (1,)), ((), ())), preferred_element_type=F32)


def _dot(a, b):
    return jnp.dot(a, b, preferred_element_type=F32)


def _split3(x):
    hi = x.astype(BF16)
    r = x - hi.astype(F32)
    mid = r.astype(BF16)
    lo = (r - mid.astype(F32)).astype(BF16)
    return hi, mid, lo


def _dot_exact_rhs(x, m_bf16):
    hi, mid, lo = _split3(x)
    return _dot(hi, m_bf16) + _dot(mid, m_bf16) + _dot(lo, m_bf16)


def _sigmoid(x):
    return 0.5 + 0.5 * jnp.tanh(0.5 * x)


def _silu(x):
    h = 0.5 * x
    return h + h * jnp.tanh(h)


def _softplus(x):
    return jnp.maximum(x, 0.0) + jnp.log1p(jnp.exp(-jnp.abs(x)))


def _params(sem, limit=VMEM_LIMIT):
    return pltpu.CompilerParams(dimension_semantics=sem, vmem_limit_bytes=limit)


def _ada_kernel(c_ref, w_ref, b_ref, o_ref):
    cond = _silu(c_ref[...])
    hi, mid, lo = _split3(cond)
    w = w_ref[0]
    whi, wmid, wlo = _split3(w)
    acc = _dot(hi, whi) + _dot(hi, wmid) + _dot(mid, whi)
    acc = acc + _dot(hi, wlo) + _dot(mid, wmid) + _dot(lo, whi)
    o_ref[0] = acc + b_ref[0]


def _ada_mod(c, ada_w, ada_b):
    depth, d, d3 = ada_w.shape
    bsz = c.shape[0]
    nj = d3 // d
    return pl.pallas_call(
        _ada_kernel,
        grid=(depth, nj),
        in_specs=[
            pl.BlockSpec((bsz, d), lambda l, j: (0, 0)),
            pl.BlockSpec((1, d, d), lambda l, j: (l, 0, j)),
            pl.BlockSpec((1, 1, d), lambda l, j: (l, 0, j)),
        ],
        out_specs=pl.BlockSpec((1, bsz, d), lambda l, j: (l, 0, j)),
        out_shape=jax.ShapeDtypeStruct((depth, bsz, d3), F32),
        compiler_params=_params(("parallel", "parallel")),
        name="ada_mod",
    )(c, ada_w, ada_b.reshape(depth, 1, d3))


def _inproj_kernel(x_ref, sc_ref, sh_ref, g_ref, w_ref, ws_ref, o_ref, os_ref, h_scr):
    @pl.when(pl.program_id(1) == 0)
    def _():
        x = x_ref[...]
        ms = jnp.mean(x * x, axis=-1, keepdims=True)
        xn = x * lax.rsqrt(ms + RMS_EPS) * g_ref[...]
        h = xn * (1.0 + sc_ref[0]) + sh_ref[0]
        h_b = h.astype(BF16)
        h_scr[...] = h_b
        r = _nt(h_b, ws_ref[...])
        os_ref[...] = r[:, :LANES] + r[:, LANES:]

    o_ref[...] = _nt(h_scr[...], w_ref[...]).astype(BF16)


def _inproj(x2, scale, shift, g, w, ws, seq, tm=1024):
    t, d = x2.shape
    n = w.shape[0]
    tn = n // 2
    per = seq // tm
    return pl.pallas_call(
        _inproj_kernel,
        grid=(t // tm, n // tn),
        in_specs=[
            pl.BlockSpec((tm, d), lambda i, j: (i, 0)),
            pl.BlockSpec((1, 1, d), lambda i, j: (i // per, 0, 0)),
            pl.BlockSpec((1, 1, d), lambda i, j: (i // per, 0, 0)),
            pl.BlockSpec((1, d), lambda i, j: (0, 0)),
            pl.BlockSpec((tn, d), lambda i, j: (j, 0)),
            pl.BlockSpec((2 * LANES, d), lambda i, j: (0, 0)),
        ],
        out_specs=[
            pl.BlockSpec((tm, tn), lambda i, j: (i, j)),
            pl.BlockSpec((tm, LANES), lambda i, j: (i, 0)),
        ],
        out_shape=[
            jax.ShapeDtypeStruct((t, n), BF16),
            jax.ShapeDtypeStruct((t, LANES), F32),
        ],
        scratch_shapes=[pltpu.VMEM((tm, d), BF16)],
        compiler_params=_params(("parallel", "arbitrary")),
        name="in_proj",
    )(x2, scale, shift, g, w, ws)


def _outproj_kernel(a_ref, b_ref, wa_ref, wb_ref, x_ref, gate_ref, pg_ref, o_ref):
    y = _dot(a_ref[...], wa_ref[...]) + _dot(b_ref[...], wb_ref[...])
    ms = jnp.mean(y * y, axis=-1, keepdims=True)
    yn = y * lax.rsqrt(ms + RMS_EPS) * pg_ref[...]
    o_ref[...] = x_ref[...] + gate_ref[0] * yn


def _outproj(a, b, wa, wb, x2, gate, pg, seq, tm=1024):
    t, d = x2.shape
    ka, kb = a.shape[1], b.shape[1]
    per = seq // tm
    return pl.pallas_call(
        _outproj_kernel,
        grid=(t // tm,),
        in_specs=[
            pl.BlockSpec((tm, ka), lambda i: (i, 0)),
            pl.BlockSpec((tm, kb), lambda i: (i, 0)),
            pl.BlockSpec((ka, d), lambda i: (0, 0)),
            pl.BlockSpec((kb, d), lambda i: (0, 0)),
            pl.BlockSpec((tm, d), lambda i: (i, 0)),
            pl.BlockSpec((1, 1, d), lambda i: (i // per, 0, 0)),
            pl.BlockSpec((1, d), lambda i: (0, 0)),
        ],
        out_specs=pl.BlockSpec((tm, d), lambda i: (i, 0)),
        out_shape=jax.ShapeDtypeStruct((t, d), F32),
        compiler_params=_params(("parallel",)),
        name="out_proj",
    )(a, b, wa, wb, x2, gate, pg)


def _causal_conv_silu(raw, prev, shift, w, b):
    lc = raw.shape[0]
    ext = jnp.concatenate([prev, raw], axis=0)
    shifted = _dot(shift, ext)
    acc = b + w[SSD_CONV - 1:SSD_CONV, :] * raw.astype(F32)
    for j in range(SSD_CONV - 1):
        acc = acc + w[j:j + 1, :] * shifted[j * lc:(j + 1) * lc, :]
    return _silu(acc)


def _ssd_kernel(z_ref, xs_ref, bc_ref, xsp_ref, bcp_ref, dt_ref, cwx_ref, cwb_ref, cbx_ref, cbb_ref, dtb_ref,
                alog_ref, dexp_ref, ng_ref, e_ref, sh_ref, o_ref, state_scr):
    lc = SSD_CHUNK
    hw = SSD_HEADS // SSD_GROUPS * HEAD_DIM
    first = pl.program_id(1) == 0

    @pl.when(first)
    def _():
        state_scr[...] = jnp.zeros_like(state_scr)

    rows = lax.broadcasted_iota(jnp.int32, (lc, lc), 0)
    cols = lax.broadcasted_iota(jnp.int32, (lc, lc), 1)
    lower = rows >= cols
    state = [state_scr[:, g * hw:(g + 1) * hw] for g in range(SSD_GROUPS)]
    prev_x = jnp.where(first, jnp.zeros_like(xsp_ref[...]), xsp_ref[...])
    prev_b = jnp.where(first, jnp.zeros_like(bcp_ref[...]), bcp_ref[...])
    for j in range(xs_ref.shape[0] // lc):
        sub = slice(j * lc, (j + 1) * lc)
        raw_x = xs_ref[sub, :]
        raw_b = bc_ref[sub, :]
        y, state = _ssd_chunk(raw_x, raw_b, prev_x, prev_b, dt_ref[sub, :], z_ref[sub, :], state, lower,
                              cwx_ref, cwb_ref, cbx_ref, cbb_ref, dtb_ref, alog_ref, dexp_ref, e_ref, sh_ref)
        ms = jnp.mean(y * y, axis=-1, keepdims=True)
        o_ref[sub, :] = (y * lax.rsqrt(ms + RMS_EPS) * ng_ref[...]).astype(BF16)
        prev_x, prev_b = raw_x, raw_b
    for g in range(SSD_GROUPS):
        state_scr[:, g * hw:(g + 1) * hw] = state[g]


def _ssd_chunk(raw_x, raw_b, prev_x, prev_b, dt_raw, z, state, lower,
               cwx_ref, cwb_ref, cbx_ref, cbb_ref, dtb_ref, alog_ref, dexp_ref, e_ref, sh_ref):
    lc = SSD_CHUNK
    hw = SSD_HEADS // SSD_GROUPS * HEAD_DIM
    xs = _causal_conv_silu(raw_x, prev_x, sh_ref[...], cwx_ref[...], cbx_ref[...])
    bc = _causal_conv_silu(raw_b, prev_b, sh_ref[...], cwb_ref[...], cbb_ref[...])

    dt = _softplus(dt_raw + dtb_ref[...])
    adt = dt * (-jnp.exp(alog_ref[...]))
    a_cum = _dot_exact_rhs_lhs(lower.astype(BF16), adt)
    a_cum_t = a_cum.T
    a_last = a_cum[lc - 1:lc, :]
    stacked = jnp.concatenate([dt, jnp.exp(a_last - a_cum), jnp.exp(a_cum)], axis=0)
    expanded = _dot(stacked.astype(BF16), e_ref[...])
    dt_e = expanded[0:lc]
    dec_e = expanded[lc:2 * lc]
    ea_e = expanded[2 * lc:3 * lc]
    al_e = _dot_exact_rhs(jnp.broadcast_to(jnp.exp(a_last), (8, LANES)), e_ref[...])[0:1]

    xdt = xs * dt_e
    xdt_b = xdt.astype(BF16)
    xd_b = (xdt * dec_e).astype(BF16)
    lane = lax.broadcasted_iota(jnp.int32, (1, LANES), 1)
    pieces = []
    new_state = []
    for g in range(SSD_GROUPS):
        bg = bc[:, g * SSD_STATE:(g + 1) * SSD_STATE]
        cg = bc[:, (SSD_GROUPS + g) * SSD_STATE:(SSD_GROUPS + g + 1) * SSD_STATE]
        cg_b = cg.astype(BF16)
        cb = _nt(cg_b, bg.astype(BF16))
        st = state[g]
        y_off = _dot(cg_b, st.astype(BF16)) * ea_e[:, g * hw:(g + 1) * hw]
        new_st = _dot(bg.T.astype(BF16), xd_b[:, g * hw:(g + 1) * hw])
        new_state.append(al_e[:, g * hw:(g + 1) * hw] * st + new_st)
        for pair in range(SSD_HEADS // SSD_GROUPS // 2):
            acc = None
            c0 = g * hw + pair * LANES
            xp = xdt_b[:, c0:c0 + LANES]
            for half in range(2):
                h = g * (SSD_HEADS // SSD_GROUPS) + pair * 2 + half
                diff = a_cum[:, h:h + 1] - a_cum_t[h:h + 1, :]
                decay = jnp.exp(jnp.where(lower, diff, NEG))
                m = (cb * decay).astype(BF16)
                in_half = (lane >= half * HEAD_DIM) & (lane < (half + 1) * HEAD_DIM)
                part = _dot(m, jnp.where(in_half, xp, jnp.zeros_like(xp)))
                acc = part if acc is None else acc + part
            pieces.append(acc + y_off[:, pair * LANES:(pair + 1) * LANES])
    y = jnp.concatenate(pieces, axis=1) + dexp_ref[...] * xs
    return y * _silu(z.astype(F32)), new_state


def _dot_exact_rhs_lhs(m_bf16, x):
    hi, mid, lo = _split3(x)
    return _dot(m_bf16, hi) + _dot(m_bf16, mid) + _dot(m_bf16, lo)


def _ssd(proj, small, conv_w, conv_b, dt_bias, a_log, d_skip, norm_g, bsz, seq):
    lc = SSD_CHUNK
    nc = seq // lc
    t = bsz * seq
    pad = LANES - SSD_HEADS
    dtb = jnp.pad(dt_bias, (0, pad)).reshape(1, LANES)
    alog = jnp.pad(a_log, (0, pad)).reshape(1, LANES)
    dexp = jnp.repeat(d_skip, HEAD_DIM).reshape(1, SSD_WIDTH)
    expand = (jnp.arange(LANES)[:, None] == (jnp.arange(SSD_WIDTH) // HEAD_DIM)[None, :]).astype(BF16)
    nbc = 2 * SSD_GROUPS * SSD_STATE
    src = lc + jnp.arange(lc)[None, :] - (SSD_CONV - 1) + jnp.arange(SSD_CONV - 1)[:, None]
    shift = (src.reshape(-1, 1) == jnp.arange(2 * lc)[None, :]).astype(BF16)
    ns = nc // SSD_SUB
    rs = SSD_SUB * lc
    row = lambda b, c: b * ns + c
    prev = lambda b, c: b * nc + jnp.maximum(SSD_SUB * c - 1, 0)
    const = lambda b, c: (0, 0)
    return pl.pallas_call(
        _ssd_kernel,
        grid=(bsz, ns),
        in_specs=[
            pl.BlockSpec((rs, SSD_WIDTH), lambda b, c: (row(b, c), EV_ZA // SSD_WIDTH)),
            pl.BlockSpec((rs, SSD_WIDTH), lambda b, c: (row(b, c), EV_XS // SSD_WIDTH)),
            pl.BlockSpec((rs, nbc), lambda b, c: (row(b, c), EV_BC // nbc)),
            pl.BlockSpec((lc, SSD_WIDTH), lambda b, c: (prev(b, c), EV_XS // SSD_WIDTH)),
            pl.BlockSpec((lc, nbc), lambda b, c: (prev(b, c), EV_BC // nbc)),
            pl.BlockSpec((rs, LANES), lambda b, c: (row(b, c), 0)),
            pl.BlockSpec((SSD_CONV, SSD_WIDTH), const),
            pl.BlockSpec((SSD_CONV, nbc), const),
            pl.BlockSpec((1, SSD_WIDTH), const),
            pl.BlockSpec((1, nbc), const),
            pl.BlockSpec((1, LANES), const),
            pl.BlockSpec((1, LANES), const),
            pl.BlockSpec((1, SSD_WIDTH), const),
            pl.BlockSpec((1, SSD_WIDTH), const),
            pl.BlockSpec((LANES, SSD_WIDTH), const),
            pl.BlockSpec(((SSD_CONV - 1) * lc, 2 * lc), const),
        ],
        out_specs=pl.BlockSpec((rs, SSD_WIDTH), lambda b, c: (row(b, c), 0)),
        out_shape=jax.ShapeDtypeStruct((t, SSD_WIDTH), BF16),
        scratch_shapes=[pltpu.VMEM((SSD_STATE, SSD_WIDTH), F32)],
        compiler_params=_params(("parallel", "arbitrary")),
        name="ssd",
    )(proj, proj, proj, proj, proj, small,
      conv_w[:, :SSD_WIDTH], conv_w[:, SSD_WIDTH:], conv_b[:SSD_WIDTH].reshape(1, -1),
      conv_b[SSD_WIDTH:].reshape(1, -1), dtb, alog, dexp, norm_g.reshape(1, -1), expand, shift)


def _fold_rows(x, op):
    out = x[0:8, :]
    for i in range(1, x.shape[0] // 8):
        out = op(out, x[8 * i:8 * (i + 1), :])
    return out


def _attend_pair(qa_scr, ka_scr, vt_scr, s_scr, p_scr, ot_scr, bias_fn):
    blk = ATT_BLOCK
    nb = qa_scr.shape[1] // blk
    heads = (0, 1)
    dyn0 = jnp.minimum(pl.program_id(1), 0)
    causal = (lax.broadcasted_iota(jnp.int32, (blk, blk), 0)
              <= lax.broadcasted_iota(jnp.int32, (blk, blk), 1))

    def score_tile(half, qb, n, m8):
        s = _nt(ka_scr[half, n * blk:(n + 1) * blk, :], qa_scr[half, qb * blk:(qb + 1) * blk, :])
        bias = bias_fn(half, qb, n)
        if bias is not None:
            s = s + bias
        if n == qb:
            s = jnp.where(causal, s, NEG)
        s_scr[half, qb % 2 + dyn0, n] = s
        part = _fold_rows(s, jnp.maximum)
        return part if m8 is None else jnp.maximum(m8, part)

    def prob_tile(half, qb, n, m):
        p = jnp.exp2(s_scr[half, qb % 2 + dyn0, n] - m)
        p_scr[half, qb % 2, n * blk:(n + 1) * blk, :] = p.astype(BF16)

    m8 = [score_tile(half, 0, 0, None) for half in heads]
    for qb in range(nb):
        m = [jnp.max(m8[half], axis=0, keepdims=True) for half in heads]
        nxt = qb + 1
        m8 = [None, None]
        for n in range(nxt + 1):
            for half in heads:
                if n <= qb:
                    prob_tile(half, qb, n, m[half])
            for half in heads:
                if nxt < nb:
                    m8[half] = score_tile(half, nxt, n, m8[half])
        keys = (qb + 1) * blk
        for half in heads:
            acc = _dot(vt_scr[half, :, 0:keys], p_scr[half, qb % 2, 0:keys, :])
            out = acc[0:HEAD_DIM] / acc[HEAD_DIM:HEAD_DIM + 1]
            ot_scr[half * HEAD_DIM:(half + 1) * HEAD_DIM, qb * blk:(qb + 1) * blk] = out


def _store_vt(v_ref, vt_scr):
    ones = jnp.ones((VT_ROWS - HEAD_DIM, ATT_BLOCK), BF16)
    for n in range(v_ref.shape[0] // ATT_BLOCK):
        cols = slice(n * ATT_BLOCK, (n + 1) * ATT_BLOCK)
        vt = v_ref[cols, :].astype(F32).T.astype(BF16)
        for half in range(2):
            vt_scr[half, 0:HEAD_DIM, cols] = vt[half * HEAD_DIM:(half + 1) * HEAD_DIM, :]
            vt_scr[half, HEAD_DIM:VT_ROWS, cols] = ones


def _gated_output(ot_scr, z_ref, o_ref):
    for i in range(ot_scr.shape[1] // ATT_BLOCK):
        rows = slice(i * ATT_BLOCK, (i + 1) * ATT_BLOCK)
        o = ot_scr[:, rows].T
        o_ref[rows, :] = (o * _silu(z_ref[rows, :].astype(F32))).astype(BF16)


def _head_lanes(half):
    lane = lax.broadcasted_iota(jnp.int32, (1, LANES), 1)
    return (lane >= half * HEAD_DIM) & (lane < (half + 1) * HEAD_DIM)


def _moba_kernel(q_ref, k_ref, v_ref, z_ref, o_ref, vt_scr, kbar_scr, bias_scr, qa_scr, ka_scr, s_scr, p_scr,
                 ot_scr):
    blk = ATT_BLOCK
    seq = q_ref.shape[0]
    nb = seq // blk
    _store_vt(v_ref, vt_scr)
    for n in range(nb):
        kb = k_ref[n * blk:(n + 1) * blk, :].astype(F32)
        kbar_scr[n:n + 1, :] = jnp.mean(kb, axis=0, keepdims=True)
    n_idx = lax.broadcasted_iota(jnp.int32, (nb, seq), 0)
    q_blk = lax.broadcasted_iota(jnp.int32, (nb, seq), 1) >> ATT_BLOCK_SHIFT

    for half in range(2):
        in_half = _head_lanes(half)
        q = q_ref[...]
        qi = jnp.where(in_half, q, jnp.zeros_like(q))
        qa_scr[half] = qi
        ka_scr[half] = k_ref[...]
        kbar = jnp.where(in_half, kbar_scr[...], 0.0)
        kb_hi = kbar.astype(BF16)
        kb_lo = (kbar - kb_hi.astype(F32)).astype(BF16)
        gate = _nt(kb_hi, qi) + _nt(kb_lo, qi)
        rank = jnp.zeros((nb, seq), F32)
        for mth in range(nb):
            gm = gate[mth:mth + 1, :]
            beats = (gm > gate) | ((gm == gate) & (mth < n_idx))
            rank = rank + jnp.where(beats & (mth < q_blk), 1.0, 0.0)
        chosen = (rank < float(MOBA_TOPK)) & (n_idx < q_blk)
        bias_scr[half] = jnp.where(chosen, 0.0, NEG)

    def bias_fn(half, qb, n):
        if n == qb or qb <= MOBA_TOPK:
            return None
        return bias_scr[half, n:n + 1, qb * blk:(qb + 1) * blk]

    _attend_pair(qa_scr, ka_scr, vt_scr, s_scr, p_scr, ot_scr, bias_fn)
    _gated_output(ot_scr, z_ref, o_ref)


def _moba(proj, bsz, seq):
    blk = ATT_BLOCK
    nb = seq // blk
    pairs = MOBA_WIDTH // LANES
    t = bsz * seq
    return pl.pallas_call(
        _moba_kernel,
        grid=(bsz, pairs),
        in_specs=[
            pl.BlockSpec((seq, LANES), lambda b, p: (b, EV_Q // LANES + p)),
            pl.BlockSpec((seq, LANES), lambda b, p: (b, EV_K // LANES + p)),
            pl.BlockSpec((seq, LANES), lambda b, p: (b, EV_V // LANES + p)),
            pl.BlockSpec((seq, LANES), lambda b, p: (b, EV_ZB // LANES + p)),
        ],
        out_specs=pl.BlockSpec((seq, LANES), lambda b, p: (b, p)),
        out_shape=jax.ShapeDtypeStruct((t, MOBA_WIDTH), BF16),
        scratch_shapes=[
            pltpu.VMEM((2, VT_ROWS, seq), BF16),
            pltpu.VMEM((nb, LANES), F32),
            pltpu.VMEM((2, nb, seq), F32),
            pltpu.VMEM((2, seq, LANES), BF16),
            pltpu.VMEM((2, seq, LANES), BF16),
            pltpu.VMEM((2, 2, nb, blk, blk), F32),
            pltpu.VMEM((2, 2, seq, blk), BF16),
            pltpu.VMEM((LANES, seq), F32),
        ],
        compiler_params=_params(("parallel", "parallel")),
        name="moba",
    )(proj, proj, proj, proj)


def _fgate_kernel(f_ref, fb_ref, o_ref):
    blk = ATT_BLOCK
    rows = lax.broadcasted_iota(jnp.int32, (blk, blk), 0)
    cols = lax.broadcasted_iota(jnp.int32, (blk, blk), 1)
    lower = (rows >= cols).astype(BF16)
    carry = jnp.zeros((1, LANES), F32)
    for i in range(f_ref.shape[0] // blk):
        nlf = _softplus(-(f_ref[i * blk:(i + 1) * blk, :] + fb_ref[...]))
        csum = _dot_exact_rhs_lhs(lower, nlf) + carry
        o_ref[i * blk:(i + 1) * blk, :] = csum
        carry = csum[blk - 1:blk, :]


def _fgate(small, fgate_b, bsz, seq):
    fb = jnp.pad(fgate_b, (0, LANES - FOX_HEADS)).reshape(1, LANES)
    return pl.pallas_call(
        _fgate_kernel,
        grid=(bsz,),
        in_specs=[
            pl.BlockSpec((seq, LANES), lambda b: (b, 0)),
            pl.BlockSpec((1, LANES), lambda b: (0, 0)),
        ],
        out_specs=pl.BlockSpec((seq, LANES), lambda b: (b, 0)),
        out_shape=jax.ShapeDtypeStruct((bsz * seq, LANES), F32),
        compiler_params=_params(("parallel",)),
        name="fox_gate",
    )(small, fb)


def _fox_kernel(q_ref, k_ref, v_ref, z_ref, nf_ref, o_ref, vt_scr, fb_scr, qa_scr, ka_scr, s_scr, p_scr, ot_scr):
    pair = pl.program_id(1)
    nb = q_ref.shape[0] // ATT_BLOCK
    blk = ATT_BLOCK
    _store_vt(v_ref, vt_scr)
    lane = lax.broadcasted_iota(jnp.int32, (1, LANES), 1)
    for half in range(2):
        own_lane = lane == pair * 2 + half
        for n in range(nb):
            col = jnp.sum(jnp.where(own_lane, nf_ref[n * blk:(n + 1) * blk, :], 0.0), axis=1, keepdims=True)
            fb_scr[half, n] = jnp.broadcast_to(col * LOG2E, (blk, LANES))
        q = q_ref[...]
        qa_scr[half] = jnp.where(_head_lanes(half), q, jnp.zeros_like(q))
        ka_scr[half] = k_ref[...]

    def bias_fn(half, qb, n):
        fb = fb_scr[half, n]
        return jnp.concatenate([fb, fb], axis=1)

    _attend_pair(qa_scr, ka_scr, vt_scr, s_scr, p_scr, ot_scr, bias_fn)
    _gated_output(ot_scr, z_ref, o_ref)


def _fox(proj, negf, bsz, seq):
    blk = ATT_BLOCK
    nb = seq // blk
    pairs = FOX_WIDTH // LANES
    t = bsz * seq
    return pl.pallas_call(
        _fox_kernel,
        grid=(bsz, pairs),
        in_specs=[
            pl.BlockSpec((seq, LANES), lambda b, p: (b, OD_Q // LANES + p)),
            pl.BlockSpec((seq, LANES), lambda b, p: (b, OD_K // LANES + p)),
            pl.BlockSpec((seq, LANES), lambda b, p: (b, OD_V // LANES + p)),
            pl.BlockSpec((seq, LANES), lambda b, p: (b, OD_ZC // LANES + p)),
            pl.BlockSpec((seq, LANES), lambda b, p: (b, 0)),
        ],
        out_specs=pl.BlockSpec((seq, LANES), lambda b, p: (b, p)),
        out_shape=jax.ShapeDtypeStruct((t, FOX_WIDTH), BF16),
        scratch_shapes=[
            pltpu.VMEM((2, VT_ROWS, seq), BF16),
            pltpu.VMEM((2, nb, blk, LANES), F32),
            pltpu.VMEM((2, seq, LANES), BF16),
            pltpu.VMEM((2, seq, LANES), BF16),
            pltpu.VMEM((2, 2, nb, blk, blk), F32),
            pltpu.VMEM((2, 2, seq, blk), BF16),
            pltpu.VMEM((LANES, seq), F32),
        ],
        compiler_params=_params(("parallel", "parallel")),
        name="fox",
    )(proj, proj, proj, proj, negf)


def _s5scan_kernel(u_ref, z_ref, perm_ref, permt_ref, lr_ref, li_ref, ldt_ref, bwr_ref, bwi_ref, cwr_ref, cwi_ref,
                   d_ref, gw_ref, gb_ref, y_ref, bre_scr, bim_scr, cre_scr, cim_scr, ar_scr, ai_scr, xr_scr, xi_scr, zr_scr, zi_scr,
                   *, bsz):
    nset, cw, sw = bre_scr.shape

    @pl.when(pl.program_id(0) == 0)
    def _():
        chan_grp = lax.broadcasted_iota(jnp.int32, (cw, sw), 0) >> S5_GROUP_SHIFT
        state_grp = lax.broadcasted_iota(jnp.int32, (cw, sw), 1) >> S5_STATE_SHIFT
        same_b = chan_grp == state_grp
        same_c = ((lax.broadcasted_iota(jnp.int32, (sw, cw), 0) >> S5_STATE_SHIFT)
                  == (lax.broadcasted_iota(jnp.int32, (sw, cw), 1) >> S5_GROUP_SHIFT))
        for s in range(nset):
            lr = lr_ref[s]
            li = li_ref[s]
            dt = jnp.exp(ldt_ref[s])
            mag = jnp.exp(lr * dt)
            ar = mag * jnp.cos(li * dt)
            ai = mag * jnp.sin(li * dt)
            den = lr * lr + li * li
            qr = ((ar - 1.0) * lr + ai * li) / den
            qi = (ai * lr - (ar - 1.0) * li) / den
            ar_scr[s] = jnp.broadcast_to(ar, (bsz, sw))
            ai_scr[s] = jnp.broadcast_to(ai, (bsz, sw))
            bwr = bwr_ref[s]
            bwi = bwi_ref[s]
            bre_scr[s] = jnp.where(same_b, qr * bwr - qi * bwi, 0.0).astype(BF16)
            bim_scr[s] = jnp.where(same_b, qr * bwi + qi * bwr, 0.0).astype(BF16)
            cre_scr[s] = jnp.where(same_c, cwr_ref[s], 0.0).astype(BF16)
            cim_scr[s] = jnp.where(same_c, cwi_ref[s], 0.0).astype(BF16)
        xr_scr[...] = jnp.zeros_like(xr_scr)
        xi_scr[...] = jnp.zeros_like(xi_scr)

    steps = u_ref.shape[1]
    width = u_ref.shape[2]
    u_tb = _dot(perm_ref[...], u_ref[...].reshape(bsz * steps, width))
    u_b = u_tb.astype(BF16)
    half_rows = (steps // 2) * bsz

    def project(half, s):
        rows = slice(half * half_rows, (half + 1) * half_rows)
        us = u_b[rows, s * cw:(s + 1) * cw]
        zr_scr[s, rows, :] = _dot(us, bre_scr[s])
        zi_scr[s, rows, :] = _dot(us, bim_scr[s])

    for s in range(nset):
        project(0, s)

    def step(t, carry):
        r0 = t * bsz
        out = []
        for s in range(nset):
            xr, xi = carry[2 * s], carry[2 * s + 1]
            ar = ar_scr[s]
            ai = ai_scr[s]
            nr = ar * xr - ai * xi + zr_scr[s, r0:r0 + bsz, :]
            ni = ar * xi + ai * xr + zi_scr[s, r0:r0 + bsz, :]
            zr_scr[s, r0:r0 + bsz, :] = nr
            zi_scr[s, r0:r0 + bsz, :] = ni
            out += [nr, ni]
        return out

    def readout(half, s):
        rows = slice(half * half_rows, (half + 1) * half_rows)
        return _dot(zr_scr[s, rows, :].astype(BF16), cre_scr[s]) - _dot(zi_scr[s, rows, :].astype(BF16), cim_scr[s])

    every = steps // 2 // nset
    carry = []
    for s in range(nset):
        carry += [xr_scr[s], xi_scr[s]]
    for t in range(steps // 2):
        carry = step(t, carry)
        if t % every == 0:
            project(1, t // every)
    first_half = []
    for t in range(steps // 2, steps):
        carry = step(t, carry)
        k = t - steps // 2
        if k % every == 0:
            first_half.append(readout(0, k // every))
    for s in range(nset):
        xr_scr[s] = carry[2 * s]
        xi_scr[s] = carry[2 * s + 1]
    second_half = [readout(1, s) for s in range(nset)]
    xc = jnp.concatenate([jnp.concatenate(first_half, axis=1), jnp.concatenate(second_half, axis=1)], axis=0)
    y = (xc + d_ref[...] * u_tb).astype(BF16)
    y = _dot(permt_ref[...], y)
    y = 0.5 * y * (1.0 + jnp.tanh(math.sqrt(2.0 / math.pi) * (y + 0.044715 * (y * y * y))))
    y = y * _sigmoid(_dot(y.astype(BF16), gw_ref[...]) + gb_ref[...])
    z = z_ref[...].reshape(bsz * steps, width).astype(F32)
    y_ref[...] = (y * _silu(z)).astype(BF16).reshape(bsz, steps, width)


def _s5scan(proj3, lam_re, lam_im, log_dt, b_re, b_im, c_re, c_im, d_skip, glu_w, glu_b):
    bsz, seq, _ = proj3.shape
    width = S5_WIDTH
    nset = S5_SETS
    cw = width // nset
    gs = S5_GROUPS // nset
    sw = gs * S5_STATE
    blk = S5_STEPS * bsz
    tb = jnp.arange(blk)
    perm = ((tb % bsz) * S5_STEPS + tb // bsz)[:, None] == jnp.arange(blk)[None, :]
    vec = lambda a: a.reshape(nset, 1, sw)
    bw = lambda b: jnp.tile(jnp.swapaxes(b, 1, 2).reshape(nset, cw, S5_STATE), (1, 1, gs))
    cw_t = lambda c: jnp.tile(
        jnp.swapaxes(c, 1, 2).reshape(nset, gs, S5_STATE, S5_GROUP).transpose(0, 2, 1, 3).reshape(nset, S5_STATE, cw),
        (1, gs, 1))
    full3 = lambda a, b, c: pl.BlockSpec((a, b, c), lambda i: (0, 0, 0))
    return pl.pallas_call(
        functools.partial(_s5scan_kernel, bsz=bsz),
        grid=(seq // S5_STEPS,),
        in_specs=[
            pl.BlockSpec((bsz, S5_STEPS, width), lambda i: (0, i, OD_U // width)),
            pl.BlockSpec((bsz, S5_STEPS, width), lambda i: (0, i, OD_ZD // width)),
            pl.BlockSpec((blk, blk), lambda i: (0, 0)),
            pl.BlockSpec((blk, blk), lambda i: (0, 0)),
            full3(nset, 1, sw), full3(nset, 1, sw), full3(nset, 1, sw),
            full3(nset, cw, sw), full3(nset, cw, sw),
            full3(nset, sw, cw), full3(nset, sw, cw),
            pl.BlockSpec((1, width), lambda i: (0, 0)),
            pl.BlockSpec((width, width), lambda i: (0, 0)),
            pl.BlockSpec((1, width), lambda i: (0, 0)),
        ],
        out_specs=pl.BlockSpec((bsz, S5_STEPS, width), lambda i: (0, i, 0)),
        out_shape=jax.ShapeDtypeStruct((bsz, seq, width), BF16),
        scratch_shapes=[
            pltpu.VMEM((nset, cw, sw), BF16), pltpu.VMEM((nset, cw, sw), BF16),
            pltpu.VMEM((nset, sw, cw), BF16), pltpu.VMEM((nset, sw, cw), BF16),
            pltpu.VMEM((nset, bsz, sw), F32), pltpu.VMEM((nset, bsz, sw), F32),
            pltpu.VMEM((nset, bsz, sw), F32), pltpu.VMEM((nset, bsz, sw), F32),
            pltpu.VMEM((nset, blk, sw), F32), pltpu.VMEM((nset, blk, sw), F32),
        ],
        compiler_params=_params(("arbitrary",)),
        name="s5_scan",
    )(proj3, proj3, perm.astype(BF16), perm.T.astype(BF16), vec(lam_re), vec(lam_im),
      vec(jnp.repeat(log_dt, S5_STATE)), bw(b_re), bw(b_im), cw_t(c_re), cw_t(c_im), d_skip.reshape(1, width),
      glu_w.astype(BF16), glu_b.reshape(1, width))


def _pack_weights(in_w, pieces, small):
    wt = in_w.T
    d = wt.shape[1]
    starts, scales = [], []
    for a, b, scale in pieces:
        starts += list(range(a, b, PACK_ROWS))
        scales += [scale] * ((b - a) // PACK_ROWS)

    def pick(c, table):
        out = table[-1]
        for i in range(len(table) - 2, -1, -1):
            out = jnp.where(c == i, table[i], out)
        return out

    def pack_kernel(w_ref, o_ref):
        scale = pick(pl.program_id(0), [jnp.float32(s) for s in scales])
        o_ref[...] = (w_ref[...] * scale).astype(BF16)

    main = pl.pallas_call(
        pack_kernel,
        grid=(len(starts),),
        in_specs=[pl.BlockSpec((pl.Element(PACK_ROWS), pl.Element(d)),
                               lambda c: (pl.multiple_of(pick(c, [s // 8 for s in starts]) * 8, 8), 0))],
        out_specs=pl.BlockSpec((PACK_ROWS, d), lambda c: (c, 0)),
        out_shape=jax.ShapeDtypeStruct((len(starts) * PACK_ROWS, d), BF16),
        compiler_params=_params(("parallel",)),
        name="pack_w",
    )(wt)
    a, b = small
    rows = jnp.pad(wt[a:b], ((0, LANES - (b - a)), (0, 0)))
    hi = rows.astype(BF16)
    lo = (rows - hi.astype(F32)).astype(BF16)
    return main, jnp.concatenate([hi, lo], axis=0)


def _even_layer(x2, mod, pre_g, post_g, in_w, conv_w, conv_b, dt_bias, a_log, d_skip, norm_g, out_w, bsz, seq):
    d = D_MODEL
    shift, scale, gate = (mod[:, i * d:(i + 1) * d].reshape(bsz, 1, d) for i in range(3))
    o_xbc = 2 * SSD_WIDTH
    o_dt = o_xbc + SSD_WIDTH + 2 * SSD_GROUPS * SSD_STATE
    o_q = o_dt + SSD_HEADS
    w, ws = _pack_weights(in_w, (
        (0, o_xbc + SSD_WIDTH, 1.0),
        (o_q, o_q + MOBA_WIDTH, ATT_SCALE),
        (o_q + MOBA_WIDTH, in_w.shape[1], 1.0),
        (o_xbc + SSD_WIDTH, o_dt, 1.0),
    ), (o_dt, o_q))
    proj, small = _inproj(x2, scale, shift, pre_g.reshape(1, d), w, ws, seq)
    y_a = _ssd(proj, small, conv_w, conv_b, dt_bias, a_log, d_skip, norm_g, bsz, seq)
    y_b = _moba(proj, bsz, seq)
    ow = out_w.astype(BF16)
    return _outproj(y_a, y_b, ow[:SSD_WIDTH], ow[SSD_WIDTH:], x2, gate, post_g.reshape(1, d), seq)


def _odd_layer(x2, mod, pre_g, post_g, in_w, fgate_b, lam_re, lam_im, log_dt, b_re, b_im, c_re, c_im,
               d_skip, glu_w, glu_b, out_w, bsz, seq):
    d = D_MODEL
    shift, scale, gate = (mod[:, i * d:(i + 1) * d].reshape(bsz, 1, d) for i in range(3))
    o_f = D_MIX + 3 * FOX_WIDTH
    o_u = o_f + FOX_HEADS
    w, ws = _pack_weights(in_w, (
        (0, D_MIX, 1.0),
        (D_MIX, D_MIX + FOX_WIDTH, ATT_SCALE),
        (D_MIX + FOX_WIDTH, o_f, 1.0),
        (o_u, in_w.shape[1], 1.0),
    ), (o_f, o_u))
    proj, small = _inproj(x2, scale, shift, pre_g.reshape(1, d), w, ws, seq)

    negf = _fgate(small, fgate_b, bsz, seq)
    y_c = _fox(proj, negf, bsz, seq)

    y_d = _s5scan(proj.reshape(bsz, seq, OD_N), lam_re, lam_im, log_dt, b_re, b_im, c_re, c_im, d_skip,
                  glu_w, glu_b).reshape(bsz * seq, S5_WIDTH)

    ow = out_w.astype(BF16)
    return _outproj(y_c, y_d, ow[:FOX_WIDTH], ow[FOX_WIDTH:], x2, gate, post_g.reshape(1, d), seq)


def kernel(x, c, ada_w, ada_b, pre_g, post_g, even_in_w, even_conv_w, even_conv_b, even_dt_bias, even_a_log,
           even_d_skip, even_norm_g, even_out_w, odd_in_w, odd_fgate_b, odd_lam_re, odd_lam_im, odd_log_dt,
           odd_b_re, odd_b_im, odd_c_re, odd_c_im, odd_d_skip, odd_glu_w, odd_glu_b, odd_out_w):
    bsz, seq, d = x.shape
    depth = ada_w.shape[0]
    mod = _ada_mod(c, ada_w, ada_b)
    x2 = x.reshape(bsz * seq, d)
    for layer in range(depth):
        i = layer // 2
        if layer % 2 == 0:
            x2 = _even_layer(x2, mod[layer], pre_g[layer], post_g[layer], even_in_w[i], even_conv_w[i],
                             even_conv_b[i], even_dt_bias[i], even_a_log[i], even_d_skip[i], even_norm_g[i],
                             even_out_w[i], bsz, seq)
        else:
            x2 = _odd_layer(x2, mod[layer], pre_g[layer], post_g[layer], odd_in_w[i], odd_fgate_b[i],
                            odd_lam_re[i], odd_lam_im[i], odd_log_dt[i], odd_b_re[i], odd_b_im[i], odd_c_re[i],
                            odd_c_im[i], odd_d_skip[i], odd_glu_w[i], odd_glu_b[i], odd_out_w[i], bsz, seq)
    return x2.reshape(bsz, seq, d)
```

```python
import functools
import math

import jax
import jax.numpy as jnp
from jax import lax
from jax.experimental import pallas as pl
from jax.experimental.pallas import tpu as pltpu

F32 = jnp.float32
BF16 = jnp.bfloat16

D_MODEL = 1024
HEAD_DIM = 64
D_MIX = 2 * D_MODEL
SSD_WIDTH = 1024
SSD_HEADS = 16
SSD_GROUPS = 2
SSD_STATE = 128
SSD_CONV = 4
SSD_CHUNK = 128
SSD_SUB = 4
MOBA_WIDTH = 1024
MOBA_BLOCK = 256
MOBA_TOPK = 3
FOX_WIDTH = 1536
FOX_HEADS = 24
S5_WIDTH = 512
S5_GROUP = 16
S5_GROUP_SHIFT = 4
S5_GROUPS = 32
S5_STATE = 64
S5_STATE_SHIFT = 6
S5_SETS = 4
S5_STEPS = 64
RMS_EPS = 1e-6
ATT_BLOCK = MOBA_BLOCK
ATT_BLOCK_SHIFT = 8
LOG2E = math.log2(math.e)
ATT_SCALE = LOG2E / math.sqrt(HEAD_DIM)
VT_ROWS = HEAD_DIM + 16
LANES = 128
NEG = -1e30
VMEM_LIMIT = 48 * 1024 * 1024
PACK_ROWS = 512

EV_ZA, EV_ZB, EV_XS, EV_Q, EV_K, EV_BC = 0, 1024, 2048, 3072, 4096, 5120
OD_ZC, OD_ZD, OD_Q, OD_K, OD_U, OD_N = 0, 1536, 2048, 3584, 5120, 5632


def _nt(a, b):
    return lax.dot_general(a, b, (((1,), (1,)), ((), ())), preferred_element_type=F32)


def _dot(a, b):
    return jnp.dot(a, b, preferred_element_type=F32)


def _split3(x):
    hi = x.astype(BF16)
    r = x - hi.astype(F32)
    mid = r.astype(BF16)
    lo = (r - mid.astype(F32)).astype(BF16)
    return hi, mid, lo


def _dot_exact_rhs(x, m_bf16):
    hi, mid, lo = _split3(x)
    return _dot(hi, m_bf16) + _dot(mid, m_bf16) + _dot(lo, m_bf16)


def _sigmoid(x):
    return 0.5 + 0.5 * jnp.tanh(0.5 * x)


def _silu(x):
    h = 0.5 * x
    return h + h * jnp.tanh(h)


def _softplus(x):
    return jnp.maximum(x, 0.0) + jnp.log1p(jnp.exp(-jnp.abs(x)))


def _params(sem, limit=VMEM_LIMIT):
    return pltpu.CompilerParams(dimension_semantics=sem, vmem_limit_bytes=limit)


def _ada_kernel(c_ref, w_ref, b_ref, o_ref):
    cond = _silu(c_ref[...])
    hi, mid, lo = _split3(cond)
    w = w_ref[0]
    whi, wmid, wlo = _split3(w)
    acc = _dot(hi, whi) + _dot(hi, wmid) + _dot(mid, whi)
    acc = acc + _dot(hi, wlo) + _dot(mid, wmid) + _dot(lo, whi)
    o_ref[0] = acc + b_ref[0]


def _ada_mod(c, ada_w, ada_b):
    depth, d, d3 = ada_w.shape
    bsz = c.shape[0]
    nj = d3 // d
    return pl.pallas_call(
        _ada_kernel,
        grid=(depth, nj),
        in_specs=[
            pl.BlockSpec((bsz, d), lambda l, j: (0, 0)),
            pl.BlockSpec((1, d, d), lambda l, j: (l, 0, j)),
            pl.BlockSpec((1, 1, d), lambda l, j: (l, 0, j)),
        ],
        out_specs=pl.BlockSpec((1, bsz, d), lambda l, j: (l, 0, j)),
        out_shape=jax.ShapeDtypeStruct((depth, bsz, d3), F32),
        compiler_params=_params(("parallel", "parallel")),
        name="ada_mod",
    )(c, ada_w, ada_b.reshape(depth, 1, d3))


def _inproj_kernel(x_ref, sc_ref, sh_ref, g_ref, w_ref, ws_ref, wv_ref, o_ref, os_ref, vt_ref, h_scr):
    @pl.when(pl.program_id(1) == 0)
    def _():
        x = x_ref[...]
        ms = jnp.mean(x * x, axis=-1, keepdims=True)
        xn = x * lax.rsqrt(ms + RMS_EPS) * g_ref[...]
        h = xn * (1.0 + sc_ref[0]) + sh_ref[0]
        h_b = h.astype(BF16)
        h_scr[...] = h_b
        r = _nt(h_b, ws_ref[...])
        os_ref[...] = r[:, :LANES] + r[:, LANES:]
        vt_ref[...] = _nt(wv_ref[...], h_b).astype(BF16)

    o_ref[...] = _nt(h_scr[...], w_ref[...]).astype(BF16)


def _inproj(x2, scale, shift, g, w, ws, wv, seq, tm=1024):
    t, d = x2.shape
    dv = wv.shape[0]
    n = w.shape[0] - dv
    tn = n // 2
    per = seq // tm
    return pl.pallas_call(
        _inproj_kernel,
        grid=(t // tm, n // tn),
        in_specs=[
            pl.BlockSpec((tm, d), lambda i, j: (i, 0)),
            pl.BlockSpec((1, 1, d), lambda i, j: (i // per, 0, 0)),
            pl.BlockSpec((1, 1, d), lambda i, j: (i // per, 0, 0)),
            pl.BlockSpec((1, d), lambda i, j: (0, 0)),
            pl.BlockSpec((tn, d), lambda i, j: (j, 0)),
            pl.BlockSpec((2 * LANES, d), lambda i, j: (0, 0)),
            pl.BlockSpec((dv, d), lambda i, j: (0, 0)),
        ],
        out_specs=[
            pl.BlockSpec((tm, tn), lambda i, j: (i, j)),
            pl.BlockSpec((tm, LANES), lambda i, j: (i, 0)),
            pl.BlockSpec((dv, tm), lambda i, j: (0, i)),
        ],
        out_shape=[
            jax.ShapeDtypeStruct((t, n), BF16),
            jax.ShapeDtypeStruct((t, LANES), F32),
            jax.ShapeDtypeStruct((dv, t), BF16),
        ],
        scratch_shapes=[pltpu.VMEM((tm, d), BF16)],
        compiler_params=_params(("parallel", "arbitrary")),
        name="in_proj",
    )(x2, scale, shift, g, w, ws, wv)


def _outproj_kernel(a_ref, b_ref, wa_ref, wb_ref, x_ref, gate_ref, pg_ref, o_ref):
    y = _dot(a_ref[...], wa_ref[...]) + _dot(b_ref[...], wb_ref[...])
    ms = jnp.mean(y * y, axis=-1, keepdims=True)
    yn = y * lax.rsqrt(ms + RMS_EPS) * pg_ref[...]
    o_ref[...] = x_ref[...] + gate_ref[0] * yn


def _outproj(a, b, wa, wb, x2, gate, pg, seq, tm=1024):
    t, d = x2.shape
    ka, kb = a.shape[1], b.shape[1]
    per = seq // tm
    return pl.pallas_call(
        _outproj_kernel,
        grid=(t // tm,),
        in_specs=[
            pl.BlockSpec((tm, ka), lambda i: (i, 0)),
            pl.BlockSpec((tm, kb), lambda i: (i, 0)),
            pl.BlockSpec((ka, d), lambda i: (0, 0)),
            pl.BlockSpec((kb, d), lambda i: (0, 0)),
            pl.BlockSpec((tm, d), lambda i: (i, 0)),
            pl.BlockSpec((1, 1, d), lambda i: (i // per, 0, 0)),
            pl.BlockSpec((1, d), lambda i: (0, 0)),
        ],
        out_specs=pl.BlockSpec((tm, d), lambda i: (i, 0)),
        out_shape=jax.ShapeDtypeStruct((t, d), F32),
        compiler_params=_params(("parallel",)),
        name="out_proj",
    )(a, b, wa, wb, x2, gate, pg)


def _causal_conv_silu(raw, prev, shift, w, b):
    lc = raw.shape[0]
    ext = jnp.concatenate([prev, raw], axis=0)
    shifted = _dot(shift, ext)
    acc = b + w[SSD_CONV - 1:SSD_CONV, :] * raw.astype(F32)
    for j in range(SSD_CONV - 1):
        acc = acc + w[j:j + 1, :] * shifted[j * lc:(j + 1) * lc, :]
    return _silu(acc)


def _ssd_kernel(z_ref, xs_ref, bc_ref, xsp_ref, bcp_ref, dt_ref, cwx_ref, cwb_ref, cbx_ref, cbb_ref, dtb_ref,
                alog_ref, dexp_ref, ng_ref, e_ref, sh_ref, o_ref, state_scr):
    lc = SSD_CHUNK
    hw = SSD_HEADS // SSD_GROUPS * HEAD_DIM
    first = pl.program_id(1) == 0

    @pl.when(first)
    def _():
        state_scr[...] = jnp.zeros_like(state_scr)

    rows = lax.broadcasted_iota(jnp.int32, (lc, lc), 0)
    cols = lax.broadcasted_iota(jnp.int32, (lc, lc), 1)
    lower = rows >= cols
    state = [state_scr[:, g * hw:(g + 1) * hw] for g in range(SSD_GROUPS)]
    prev_x = jnp.where(first, jnp.zeros_like(xsp_ref[...]), xsp_ref[...])
    prev_b = jnp.where(first, jnp.zeros_like(bcp_ref[...]), bcp_ref[...])
    for j in range(xs_ref.shape[0] // lc):
        sub = slice(j * lc, (j + 1) * lc)
        raw_x = xs_ref[sub, :]
        raw_b = bc_ref[sub, :]
        y, state = _ssd_chunk(raw_x, raw_b, prev_x, prev_b, dt_ref[sub, :], z_ref[sub, :], state, lower,
                              cwx_ref, cwb_ref, cbx_ref, cbb_ref, dtb_ref, alog_ref, dexp_ref, e_ref, sh_ref)
        ms = jnp.mean(y * y, axis=-1, keepdims=True)
        o_ref[sub, :] = (y * lax.rsqrt(ms + RMS_EPS) * ng_ref[...]).astype(BF16)
        prev_x, prev_b = raw_x, raw_b
    for g in range(SSD_GROUPS):
        state_scr[:, g * hw:(g + 1) * hw] = state[g]


def _ssd_chunk(raw_x, raw_b, prev_x, prev_b, dt_raw, z, state, lower,
               cwx_ref, cwb_ref, cbx_ref, cbb_ref, dtb_ref, alog_ref, dexp_ref, e_ref, sh_ref):
    lc = SSD_CHUNK
    hw = SSD_HEADS // SSD_GROUPS * HEAD_DIM
    xs = _causal_conv_silu(raw_x, prev_x, sh_ref[...], cwx_ref[...], cbx_ref[...])
    bc = _causal_conv_silu(raw_b, prev_b, sh_ref[...], cwb_ref[...], cbb_ref[...])

    dt = _softplus(dt_raw + dtb_ref[...])
    adt = dt * (-jnp.exp(alog_ref[...]))
    a_cum = _dot_exact_rhs_lhs(lower.astype(BF16), adt)
    a_cum_t = a_cum.T
    a_last = a_cum[lc - 1:lc, :]
    stacked = jnp.concatenate([dt, jnp.exp(a_last - a_cum), jnp.exp(a_cum)], axis=0)
    expanded = _dot(stacked.astype(BF16), e_ref[...])
    dt_e = expanded[0:lc]
    dec_e = expanded[lc:2 * lc]
    ea_e = expanded[2 * lc:3 * lc]
    al_e = _dot_exact_rhs(jnp.broadcast_to(jnp.exp(a_last), (8, LANES)), e_ref[...])[0:1]

    xdt = xs * dt_e
    xdt_b = xdt.astype(BF16)
    xd_b = (xdt * dec_e).astype(BF16)
    lane = lax.broadcasted_iota(jnp.int32, (1, LANES), 1)
    pieces = []
    new_state = []
    for g in range(SSD_GROUPS):
        bg = bc[:, g * SSD_STATE:(g + 1) * SSD_STATE]
        cg = bc[:, (SSD_GROUPS + g) * SSD_STATE:(SSD_GROUPS + g + 1) * SSD_STATE]
        cg_b = cg.astype(BF16)
        cb = _nt(cg_b, bg.astype(BF16))
        st = state[g]
        y_off = _dot(cg_b, st.astype(BF16)) * ea_e[:, g * hw:(g + 1) * hw]
        new_st = _dot(bg.T.astype(BF16), xd_b[:, g * hw:(g + 1) * hw])
        new_state.append(al_e[:, g * hw:(g + 1) * hw] * st + new_st)
        for pair in range(SSD_HEADS // SSD_GROUPS // 2):
            acc = None
            c0 = g * hw + pair * LANES
            xp = xdt_b[:, c0:c0 + LANES]
            for half in range(2):
                h = g * (SSD_HEADS // SSD_GROUPS) + pair * 2 + half
                diff = a_cum[:, h:h + 1] - a_cum_t[h:h + 1, :]
                decay = jnp.exp(jnp.where(lower, diff, NEG))
                m = (cb * decay).astype(BF16)
                in_half = (lane >= half * HEAD_DIM) & (lane < (half + 1) * HEAD_DIM)
                part = _dot(m, jnp.where(in_half, xp, jnp.zeros_like(xp)))
                acc = part if acc is None else acc + part
            pieces.append(acc + y_off[:, pair * LANES:(pair + 1) * LANES])
    y = jnp.concatenate(pieces, axis=1) + dexp_ref[...] * xs
    return y * _silu(z.astype(F32)), new_state


def _dot_exact_rhs_lhs(m_bf16, x):
    hi, mid, lo = _split3(x)
    return _dot(m_bf16, hi) + _dot(m_bf16, mid) + _dot(m_bf16, lo)


def _ssd(proj, small, conv_w, conv_b, dt_bias, a_log, d_skip, norm_g, bsz, seq):
    lc = SSD_CHUNK
    nc = seq // lc
    t = bsz * seq
    pad = LANES - SSD_HEADS
    dtb = jnp.pad(dt_bias, (0, pad)).reshape(1, LANES)
    alog = jnp.pad(a_log, (0, pad)).reshape(1, LANES)
    dexp = jnp.repeat(d_skip, HEAD_DIM).reshape(1, SSD_WIDTH)
    expand = (jnp.arange(LANES)[:, None] == (jnp.arange(SSD_WIDTH) // HEAD_DIM)[None, :]).astype(BF16)
    nbc = 2 * SSD_GROUPS * SSD_STATE
    src = lc + jnp.arange(lc)[None, :] - (SSD_CONV - 1) + jnp.arange(SSD_CONV - 1)[:, None]
    shift = (src.reshape(-1, 1) == jnp.arange(2 * lc)[None, :]).astype(BF16)
    ns = nc // SSD_SUB
    rs = SSD_SUB * lc
    row = lambda b, c: b * ns + c
    prev = lambda b, c: b * nc + jnp.maximum(SSD_SUB * c - 1, 0)
    const = lambda b, c: (0, 0)
    return pl.pallas_call(
        _ssd_kernel,
        grid=(bsz, ns),
        in_specs=[
            pl.BlockSpec((rs, SSD_WIDTH), lambda b, c: (row(b, c), EV_ZA // SSD_WIDTH)),
            pl.BlockSpec((rs, SSD_WIDTH), lambda b, c: (row(b, c), EV_XS // SSD_WIDTH)),
            pl.BlockSpec((rs, nbc), lambda b, c: (row(b, c), EV_BC // nbc)),
            pl.BlockSpec((lc, SSD_WIDTH), lambda b, c: (prev(b, c), EV_XS // SSD_WIDTH)),
            pl.BlockSpec((lc, nbc), lambda b, c: (prev(b, c), EV_BC // nbc)),
            pl.BlockSpec((rs, LANES), lambda b, c: (row(b, c), 0)),
            pl.BlockSpec((SSD_CONV, SSD_WIDTH), const),
            pl.BlockSpec((SSD_CONV, nbc), const),
            pl.BlockSpec((1, SSD_WIDTH), const),
            pl.BlockSpec((1, nbc), const),
            pl.BlockSpec((1, LANES), const),
            pl.BlockSpec((1, LANES), const),
            pl.BlockSpec((1, SSD_WIDTH), const),
            pl.BlockSpec((1, SSD_WIDTH), const),
            pl.BlockSpec((LANES, SSD_WIDTH), const),
            pl.BlockSpec(((SSD_CONV - 1) * lc, 2 * lc), const),
        ],
        out_specs=pl.BlockSpec((rs, SSD_WIDTH), lambda b, c: (row(b, c), 0)),
        out_shape=jax.ShapeDtypeStruct((t, SSD_WIDTH), BF16),
        scratch_shapes=[pltpu.VMEM((SSD_STATE, SSD_WIDTH), F32)],
        compiler_params=_params(("parallel", "arbitrary")),
        name="ssd",
    )(proj, proj, proj, proj, proj, small,
      conv_w[:, :SSD_WIDTH], conv_w[:, SSD_WIDTH:], conv_b[:SSD_WIDTH].reshape(1, -1),
      conv_b[SSD_WIDTH:].reshape(1, -1), dtb, alog, dexp, norm_g.reshape(1, -1), expand, shift)


def _fold_rows(x, op):
    out = x[0:8, :]
    for i in range(1, x.shape[0] // 8):
        out = op(out, x[8 * i:8 * (i + 1), :])
    return out


def _attend_pair(qa_scr, ka_scr, vt_scr, s_scr, p_scr, ot_scr, bias_fn):
    blk = ATT_BLOCK
    nb = qa_scr.shape[1] // blk
    heads = (0, 1)
    dyn0 = jnp.minimum(pl.program_id(1), 0)
    causal = (lax.broadcasted_iota(jnp.int32, (blk, blk), 0)
              <= lax.broadcasted_iota(jnp.int32, (blk, blk), 1))

    def score_tile(half, qb, n, m8):
        s = _nt(ka_scr[half, n * blk:(n + 1) * blk, :], qa_scr[half, qb * blk:(qb + 1) * blk, :])
        bias = bias_fn(half, qb, n)
        if bias is not None:
            s = s + bias
        if n == qb:
            s = jnp.where(causal, s, NEG)
        s_scr[half, qb % 2 + dyn0, n] = s
        part = _fold_rows(s, jnp.maximum)
        return part if m8 is None else jnp.maximum(m8, part)

    def prob_tile(half, qb, n, m):
        p = jnp.exp2(s_scr[half, qb % 2 + dyn0, n] - m)
        p_scr[half, qb % 2, n * blk:(n + 1) * blk, :] = p.astype(BF16)

    m8 = [score_tile(half, 0, 0, None) for half in heads]
    for qb in range(nb):
        m = [jnp.max(m8[half], axis=0, keepdims=True) for half in heads]
        nxt = qb + 1
        m8 = [None, None]
        for n in range(nxt + 1):
            for half in heads:
                if n <= qb:
                    prob_tile(half, qb, n, m[half])
            for half in heads:
                if nxt < nb:
                    m8[half] = score_tile(half, nxt, n, m8[half])
        keys = (qb + 1) * blk
        for half in heads:
            acc = _dot(vt_scr[half, :, 0:keys], p_scr[half, qb % 2, 0:keys, :])
            out = acc[0:HEAD_DIM] / acc[HEAD_DIM:HEAD_DIM + 1]
            ot_scr[half * HEAD_DIM:(half + 1) * HEAD_DIM, qb * blk:(qb + 1) * blk] = out


def _store_vt(v_ref, vt_scr):
    seq = v_ref.shape[1]
    for half in range(2):
        vt_scr[half, 0:HEAD_DIM, :] = v_ref[half * HEAD_DIM:(half + 1) * HEAD_DIM, :]
        vt_scr[half, HEAD_DIM:VT_ROWS, :] = jnp.ones((VT_ROWS - HEAD_DIM, seq), BF16)


def _gated_output(ot_scr, z_ref, o_ref):
    for i in range(ot_scr.shape[1] // ATT_BLOCK):
        rows = slice(i * ATT_BLOCK, (i + 1) * ATT_BLOCK)
        o = ot_scr[:, rows].T
        o_ref[rows, :] = (o * _silu(z_ref[rows, :].astype(F32))).astype(BF16)


def _head_lanes(half):
    lane = lax.broadcasted_iota(jnp.int32, (1, LANES), 1)
    return (lane >= half * HEAD_DIM) & (lane < (half + 1) * HEAD_DIM)


def _moba_kernel(q_ref, k_ref, v_ref, z_ref, o_ref, vt_scr, kbar_scr, bias_scr, qa_scr, ka_scr, s_scr, p_scr,
                 ot_scr):
    blk = ATT_BLOCK
    seq = q_ref.shape[0]
    nb = seq // blk
    _store_vt(v_ref, vt_scr)
    for n in range(nb):
        kb = k_ref[n * blk:(n + 1) * blk, :].astype(F32)
        kbar_scr[n:n + 1, :] = jnp.mean(kb, axis=0, keepdims=True)
    n_idx = lax.broadcasted_iota(jnp.int32, (nb, seq), 0)
    q_blk = lax.broadcasted_iota(jnp.int32, (nb, seq), 1) >> ATT_BLOCK_SHIFT

    for half in range(2):
        in_half = _head_lanes(half)
        q = q_ref[...]
        qi = jnp.where(in_half, q, jnp.zeros_like(q))
        qa_scr[half] = qi
        ka_scr[half] = k_ref[...]
        kbar = jnp.where(in_half, kbar_scr[...], 0.0)
        kb_hi = kbar.astype(BF16)
        kb_lo = (kbar - kb_hi.astype(F32)).astype(BF16)
        gate = _nt(kb_hi, qi) + _nt(kb_lo, qi)
        rank = jnp.zeros((nb, seq), F32)
        for mth in range(nb):
            gm = gate[mth:mth + 1, :]
            beats = (gm > gate) | ((gm == gate) & (mth < n_idx))
            rank = rank + jnp.where(beats & (mth < q_blk), 1.0, 0.0)
        chosen = (rank < float(MOBA_TOPK)) & (n_idx < q_blk)
        bias_scr[half] = jnp.where(chosen, 0.0, NEG)

    def bias_fn(half, qb, n):
        if n == qb or qb <= MOBA_TOPK:
            return None
        return bias_scr[half, n:n + 1, qb * blk:(qb + 1) * blk]

    _attend_pair(qa_scr, ka_scr, vt_scr, s_scr, p_scr, ot_scr, bias_fn)
    _gated_output(ot_scr, z_ref, o_ref)


def _moba(proj, v_t, bsz, seq):
    blk = ATT_BLOCK
    nb = seq // blk
    pairs = MOBA_WIDTH // LANES
    t = bsz * seq
    return pl.pallas_call(
        _moba_kernel,
        grid=(bsz, pairs),
        in_specs=[
            pl.BlockSpec((seq, LANES), lambda b, p: (b, EV_Q // LANES + p)),
            pl.BlockSpec((seq, LANES), lambda b, p: (b, EV_K // LANES + p)),
            pl.BlockSpec((LANES, seq), lambda b, p: (p, b)),
            pl.BlockSpec((seq, LANES), lambda b, p: (b, EV_ZB // LANES + p)),
        ],
        out_specs=pl.BlockSpec((seq, LANES), lambda b, p: (b, p)),
        out_shape=jax.ShapeDtypeStruct((t, MOBA_WIDTH), BF16),
        scratch_shapes=[
            pltpu.VMEM((2, VT_ROWS, seq), BF16),
            pltpu.VMEM((nb, LANES), F32),
            pltpu.VMEM((2, nb, seq), F32),
            pltpu.VMEM((2, seq, LANES), BF16),
            pltpu.VMEM((2, seq, LANES), BF16),
            pltpu.VMEM((2, 2, nb, blk, blk), F32),
            pltpu.VMEM((2, 2, seq, blk), BF16),
            pltpu.VMEM((LANES, seq), F32),
        ],
        compiler_params=_params(("parallel", "parallel")),
        name="moba",
    )(proj, proj, v_t, proj)


def _fgate_kernel(f_ref, fb_ref, o_ref):
    blk = ATT_BLOCK
    rows = lax.broadcasted_iota(jnp.int32, (blk, blk), 0)
    cols = lax.broadcasted_iota(jnp.int32, (blk, blk), 1)
    lower = (rows >= cols).astype(BF16)
    carry = jnp.zeros((1, LANES), F32)
    for i in range(f_ref.shape[0] // blk):
        nlf = _softplus(-(f_ref[i * blk:(i + 1) * blk, :] + fb_ref[...]))
        csum = _dot_exact_rhs_lhs(lower, nlf) + carry
        o_ref[i * blk:(i + 1) * blk, :] = csum
        carry = csum[blk - 1:blk, :]


def _fgate(small, fgate_b, bsz, seq):
    fb = jnp.pad(fgate_b, (0, LANES - FOX_HEADS)).reshape(1, LANES)
    return pl.pallas_call(
        _fgate_kernel,
        grid=(bsz,),
        in_specs=[
            pl.BlockSpec((seq, LANES), lambda b: (b, 0)),
            pl.BlockSpec((1, LANES), lambda b: (0, 0)),
        ],
        out_specs=pl.BlockSpec((seq, LANES), lambda b: (b, 0)),
        out_shape=jax.ShapeDtypeStruct((bsz * seq, LANES), F32),
        compiler_params=_params(("parallel",)),
        name="fox_gate",
    )(small, fb)


def _fox_kernel(q_ref, k_ref, v_ref, z_ref, nf_ref, o_ref, vt_scr, fb_scr, qa_scr, ka_scr, s_scr, p_scr, ot_scr):
    pair = pl.program_id(1)
    nb = q_ref.shape[0] // ATT_BLOCK
    blk = ATT_BLOCK
    _store_vt(v_ref, vt_scr)
    lane = lax.broadcasted_iota(jnp.int32, (1, LANES), 1)
    for half in range(2):
        own_lane = lane == pair * 2 + half
        for n in range(nb):
            col = jnp.sum(jnp.where(own_lane, nf_ref[n * blk:(n + 1) * blk, :], 0.0), axis=1, keepdims=True)
            fb_scr[half, n] = jnp.broadcast_to(col * LOG2E, (blk, LANES))
        q = q_ref[...]
        qa_scr[half] = jnp.where(_head_lanes(half), q, jnp.zeros_like(q))
        ka_scr[half] = k_ref[...]

    def bias_fn(half, qb, n):
        fb = fb_scr[half, n]
        return jnp.concatenate([fb, fb], axis=1)

    _attend_pair(qa_scr, ka_scr, vt_scr, s_scr, p_scr, ot_scr, bias_fn)
    _gated_output(ot_scr, z_ref, o_ref)


def _fox(proj, v_t, negf, bsz, seq):
    blk = ATT_BLOCK
    nb = seq // blk
    pairs = FOX_WIDTH // LANES
    t = bsz * seq
    return pl.pallas_call(
        _fox_kernel,
        grid=(bsz, pairs),
        in_specs=[
            pl.BlockSpec((seq, LANES), lambda b, p: (b, OD_Q // LANES + p)),
            pl.BlockSpec((seq, LANES), lambda b, p: (b, OD_K // LANES + p)),
            pl.BlockSpec((LANES, seq), lambda b, p: (p, b)),
            pl.BlockSpec((seq, LANES), lambda b, p: (b, OD_ZC // LANES + p)),
            pl.BlockSpec((seq, LANES), lambda b, p: (b, 0)),
        ],
        out_specs=pl.BlockSpec((seq, LANES), lambda b, p: (b, p)),
        out_shape=jax.ShapeDtypeStruct((t, FOX_WIDTH), BF16),
        scratch_shapes=[
            pltpu.VMEM((2, VT_ROWS, seq), BF16),
            pltpu.VMEM((2, nb, blk, LANES), F32),
            pltpu.VMEM((2, seq, LANES), BF16),
            pltpu.VMEM((2, seq, LANES), BF16),
            pltpu.VMEM((2, 2, nb, blk, blk), F32),
            pltpu.VMEM((2, 2, seq, blk), BF16),
            pltpu.VMEM((LANES, seq), F32),
        ],
        compiler_params=_params(("parallel", "parallel")),
        name="fox",
    )(proj, proj, v_t, proj, negf)


def _s5scan_kernel(u_ref, z_ref, perm_ref, permt_ref, lr_ref, li_ref, ldt_ref, bwr_ref, bwi_ref, cwr_ref, cwi_ref,
                   d_ref, gw_ref, gb_ref, y_ref, bre_scr, bim_scr, cre_scr, cim_scr, ar_scr, ai_scr, xr_scr, xi_scr, zr_scr, zi_scr,
                   *, bsz):
    nset, cw, sw = bre_scr.shape

    @pl.when(pl.program_id(0) == 0)
    def _():
        chan_grp = lax.broadcasted_iota(jnp.int32, (cw, sw), 0) >> S5_GROUP_SHIFT
        state_grp = lax.broadcasted_iota(jnp.int32, (cw, sw), 1) >> S5_STATE_SHIFT
        same_b = chan_grp == state_grp
        same_c = ((lax.broadcasted_iota(jnp.int32, (sw, cw), 0) >> S5_STATE_SHIFT)
                  == (lax.broadcasted_iota(jnp.int32, (sw, cw), 1) >> S5_GROUP_SHIFT))
        for s in range(nset):
            lr = lr_ref[s]
            li = li_ref[s]
            dt = jnp.exp(ldt_ref[s])
            mag = jnp.exp(lr * dt)
            ar = mag * jnp.cos(li * dt)
            ai = mag * jnp.sin(li * dt)
            den = lr * lr + li * li
            qr = ((ar - 1.0) * lr + ai * li) / den
            qi = (ai * lr - (ar - 1.0) * li) / den
            ar_scr[s] = jnp.broadcast_to(ar, (bsz, sw))
            ai_scr[s] = jnp.broadcast_to(ai, (bsz, sw))
            bwr = bwr_ref[s]
            bwi = bwi_ref[s]
            bre_scr[s] = jnp.where(same_b, qr * bwr - qi * bwi, 0.0).astype(BF16)
            bim_scr[s] = jnp.where(same_b, qr * bwi + qi * bwr, 0.0).astype(BF16)
            cre_scr[s] = jnp.where(same_c, cwr_ref[s], 0.0).astype(BF16)
            cim_scr[s] = jnp.where(same_c, cwi_ref[s], 0.0).astype(BF16)
        xr_scr[...] = jnp.zeros_like(xr_scr)
        xi_scr[...] = jnp.zeros_like(xi_scr)

    steps = u_ref.shape[1]
    width = u_ref.shape[2]
    u_tb = _dot(perm_ref[...], u_ref[...].reshape(bsz * steps, width))
    u_b = u_tb.astype(BF16)
    half_rows = (steps // 2) * bsz

    def project(half, s):
        rows = slice(half * half_rows, (half + 1) * half_rows)
        us = u_b[rows, s * cw:(s + 1) * cw]
        zr_scr[s, rows, :] = _dot(us, bre_scr[s])
        zi_scr[s, rows, :] = _dot(us, bim_scr[s])

    for s in range(nset):
        project(0, s)

    def step(t, carry):
        r0 = t * bsz
        out = []
        for s in range(nset):
            xr, xi = carry[2 * s], carry[2 * s + 1]
            ar = ar_scr[s]
            ai = ai_scr[s]
            nr = ar * xr - ai * xi + zr_scr[s, r0:r0 + bsz, :]
            ni = ar * xi + ai * xr + zi_scr[s, r0:r0 + bsz, :]
            zr_scr[s, r0:r0 + bsz, :] = nr
            zi_scr[s, r0:r0 + bsz, :] = ni
            out += [nr, ni]
        return out

    def readout(half, s):
        rows = slice(half * half_rows, (half + 1) * half_rows)
        return _dot(zr_scr[s, rows, :].astype(BF16), cre_scr[s]) - _dot(zi_scr[s, rows, :].astype(BF16), cim_scr[s])

    every = steps // 2 // nset
    carry = []
    for s in range(nset):
        carry += [xr_scr[s], xi_scr[s]]
    for t in range(steps // 2):
        carry = step(t, carry)
        if t % every == 0:
            project(1, t // every)
    first_half = []
    for t in range(steps // 2, steps):
        carry = step(t, carry)
        k = t - steps // 2
        if k % every == 0:
            first_half.append(readout(0, k // every))
    for s in range(nset):
        xr_scr[s] = carry[2 * s]
        xi_scr[s] = carry[2 * s + 1]
    second_half = [readout(1, s) for s in range(nset)]
    xc = jnp.concatenate([jnp.concatenate(first_half, axis=1), jnp.concatenate(second_half, axis=1)], axis=0)
    y = (xc + d_ref[...] * u_tb).astype(BF16)
    y = _dot(permt_ref[...], y)
    y = 0.5 * y * (1.0 + jnp.tanh(math.sqrt(2.0 / math.pi) * (y + 0.044715 * (y * y * y))))
    y = y * _sigmoid(_dot(y.astype(BF16), gw_ref[...]) + gb_ref[...])
    z = z_ref[...].reshape(bsz * steps, width).astype(F32)
    y_ref[...] = (y * _silu(z)).astype(BF16).reshape(bsz, steps, width)


def _s5scan(proj3, lam_re, lam_im, log_dt, b_re, b_im, c_re, c_im, d_skip, glu_w, glu_b):
    bsz, seq, _ = proj3.shape
    width = S5_WIDTH
    nset = S5_SETS
    cw = width // nset
    gs = S5_GROUPS // nset
    sw = gs * S5_STATE
    blk = S5_STEPS * bsz
    tb = jnp.arange(blk)
    perm = ((tb % bsz) * S5_STEPS + tb // bsz)[:, None] == jnp.arange(blk)[None, :]
    vec = lambda a: a.reshape(nset, 1, sw)
    bw = lambda b: jnp.tile(jnp.swapaxes(b, 1, 2).reshape(nset, cw, S5_STATE), (1, 1, gs))
    cw_t = lambda c: jnp.tile(
        jnp.swapaxes(c, 1, 2).reshape(nset, gs, S5_STATE, S5_GROUP).transpose(0, 2, 1, 3).reshape(nset, S5_STATE, cw),
        (1, gs, 1))
    full3 = lambda a, b, c: pl.BlockSpec((a, b, c), lambda i: (0, 0, 0))
    return pl.pallas_call(
        functools.partial(_s5scan_kernel, bsz=bsz),
        grid=(seq // S5_STEPS,),
        in_specs=[
            pl.BlockSpec((bsz, S5_STEPS, width), lambda i: (0, i, OD_U // width)),
            pl.BlockSpec((bsz, S5_STEPS, width), lambda i: (0, i, OD_ZD // width)),
            pl.BlockSpec((blk, blk), lambda i: (0, 0)),
            pl.BlockSpec((blk, blk), lambda i: (0, 0)),
            full3(nset, 1, sw), full3(nset, 1, sw), full3(nset, 1, sw),
            full3(nset, cw, sw), full3(nset, cw, sw),
            full3(nset, sw, cw), full3(nset, sw, cw),
            pl.BlockSpec((1, width), lambda i: (0, 0)),
            pl.BlockSpec((width, width), lambda i: (0, 0)),
            pl.BlockSpec((1, width), lambda i: (0, 0)),
        ],
        out_specs=pl.BlockSpec((bsz, S5_STEPS, width), lambda i: (0, i, 0)),
        out_shape=jax.ShapeDtypeStruct((bsz, seq, width), BF16),
        scratch_shapes=[
            pltpu.VMEM((nset, cw, sw), BF16), pltpu.VMEM((nset, cw, sw), BF16),
            pltpu.VMEM((nset, sw, cw), BF16), pltpu.VMEM((nset, sw, cw), BF16),
            pltpu.VMEM((nset, bsz, sw), F32), pltpu.VMEM((nset, bsz, sw), F32),
            pltpu.VMEM((nset, bsz, sw), F32), pltpu.VMEM((nset, bsz, sw), F32),
            pltpu.VMEM((nset, blk, sw), F32), pltpu.VMEM((nset, blk, sw), F32),
        ],
        compiler_params=_params(("arbitrary",)),
        name="s5_scan",
    )(proj3, proj3, perm.astype(BF16), perm.T.astype(BF16), vec(lam_re), vec(lam_im),
      vec(jnp.repeat(log_dt, S5_STATE)), bw(b_re), bw(b_im), cw_t(c_re), cw_t(c_im), d_skip.reshape(1, width),
      glu_w.astype(BF16), glu_b.reshape(1, width))


def _pack_weights(in_w, pieces, small):
    wt = in_w.T
    d = wt.shape[1]
    starts, scales = [], []
    for a, b, scale in pieces:
        starts += list(range(a, b, PACK_ROWS))
        scales += [scale] * ((b - a) // PACK_ROWS)

    def pick(c, table):
        out = table[-1]
        for i in range(len(table) - 2, -1, -1):
            out = jnp.where(c == i, table[i], out)
        return out

    def pack_kernel(w_ref, o_ref):
        scale = pick(pl.program_id(0), [jnp.float32(s) for s in scales])
        o_ref[...] = (w_ref[...] * scale).astype(BF16)

    main = pl.pallas_call(
        pack_kernel,
        grid=(len(starts),),
        in_specs=[pl.BlockSpec((pl.Element(PACK_ROWS), pl.Element(d)),
                               lambda c: (pl.multiple_of(pick(c, [s // 8 for s in starts]) * 8, 8), 0))],
        out_specs=pl.BlockSpec((PACK_ROWS, d), lambda c: (c, 0)),
        out_shape=jax.ShapeDtypeStruct((len(starts) * PACK_ROWS, d), BF16),
        compiler_params=_params(("parallel",)),
        name="pack_w",
    )(wt)
    a, b = small
    rows = jnp.pad(wt[a:b], ((0, LANES - (b - a)), (0, 0)))
    hi = rows.astype(BF16)
    lo = (rows - hi.astype(F32)).astype(BF16)
    return main, jnp.concatenate([hi, lo], axis=0)


def _even_layer(x2, mod, pre_g, post_g, in_w, conv_w, conv_b, dt_bias, a_log, d_skip, norm_g, out_w, bsz, seq):
    d = D_MODEL
    shift, scale, gate = (mod[:, i * d:(i + 1) * d].reshape(bsz, 1, d) for i in range(3))
    o_xbc = 2 * SSD_WIDTH
    o_dt = o_xbc + SSD_WIDTH + 2 * SSD_GROUPS * SSD_STATE
    o_q = o_dt + SSD_HEADS
    w, ws = _pack_weights(in_w, (
        (0, o_xbc + SSD_WIDTH, 1.0),
        (o_q, o_q + MOBA_WIDTH, ATT_SCALE),
        (o_q + MOBA_WIDTH, o_q + 2 * MOBA_WIDTH, 1.0),
        (o_xbc + SSD_WIDTH, o_dt, 1.0),
        (o_q + 2 * MOBA_WIDTH, in_w.shape[1], 1.0),
    ), (o_dt, o_q))
    proj, small, v_t = _inproj(x2, scale, shift, pre_g.reshape(1, d), w, ws, w[-MOBA_WIDTH:], seq)
    y_a = _ssd(proj, small, conv_w, conv_b, dt_bias, a_log, d_skip, norm_g, bsz, seq)
    y_b = _moba(proj, v_t, bsz, seq)
    ow = out_w.astype(BF16)
    return _outproj(y_a, y_b, ow[:SSD_WIDTH], ow[SSD_WIDTH:], x2, gate, post_g.reshape(1, d), seq)


def _odd_layer(x2, mod, pre_g, post_g, in_w, fgate_b, lam_re, lam_im, log_dt, b_re, b_im, c_re, c_im,
               d_skip, glu_w, glu_b, out_w, bsz, seq):
    d = D_MODEL
    shift, scale, gate = (mod[:, i * d:(i + 1) * d].reshape(bsz, 1, d) for i in range(3))
    o_f = D_MIX + 3 * FOX_WIDTH
    o_u = o_f + FOX_HEADS
    w, ws = _pack_weights(in_w, (
        (0, D_MIX, 1.0),
        (D_MIX, D_MIX + FOX_WIDTH, ATT_SCALE),
        (D_MIX + FOX_WIDTH, D_MIX + 2 * FOX_WIDTH, 1.0),
        (o_u, in_w.shape[1], 1.0),
        (D_MIX + 2 * FOX_WIDTH, o_f, 1.0),
    ), (o_f, o_u))
    proj, small, v_t = _inproj(x2, scale, shift, pre_g.reshape(1, d), w, ws, w[-FOX_WIDTH:], seq)

    negf = _fgate(small, fgate_b, bsz, seq)
    y_c = _fox(proj, v_t, negf, bsz, seq)

    y_d = _s5scan(proj.reshape(bsz, seq, OD_N), lam_re, lam_im, log_dt, b_re, b_im, c_re, c_im, d_skip,
                  glu_w, glu_b).reshape(bsz * seq, S5_WIDTH)

    ow = out_w.astype(BF16)
    return _outproj(y_c, y_d, ow[:FOX_WIDTH], ow[FOX_WIDTH:], x2, gate, post_g.reshape(1, d), seq)


def kernel(x, c, ada_w, ada_b, pre_g, post_g, even_in_w, even_conv_w, even_conv_b, even_dt_bias, even_a_log,
           even_d_skip, even_norm_g, even_out_w, odd_in_w, odd_fgate_b, odd_lam_re, odd_lam_im, odd_log_dt,
           odd_b_re, odd_b_im, odd_c_re, odd_c_im, odd_d_skip, odd_glu_w, odd_glu_b, odd_out_w):
    bsz, seq, d = x.shape
    depth = ada_w.shape[0]
    mod = _ada_mod(c, ada_w, ada_b)
    x2 = x.reshape(bsz * seq, d)
    for layer in range(depth):
        i = layer // 2
        if layer % 2 == 0:
            x2 = _even_layer(x2, mod[layer], pre_g[layer], post_g[layer], even_in_w[i], even_conv_w[i],
                             even_conv_b[i], even_dt_bias[i], even_a_log[i], even_d_skip[i], even_norm_g[i],
                             even_out_w[i], bsz, seq)
        else:
            x2 = _odd_layer(x2, mod[layer], pre_g[layer], post_g[layer], odd_in_w[i], odd_fgate_b[i],
                            odd_lam_re[i], odd_lam_im[i], odd_log_dt[i], odd_b_re[i], odd_b_im[i], odd_c_re[i],
                            odd_c_im[i], odd_d_skip[i], odd_glu_w[i], odd_glu_b[i], odd_out_w[i], bsz, seq)
    return x2.reshape(bsz, seq, d)
```

```python
import functools
import math

import jax
import jax.numpy as jnp
from jax import lax
from jax.experimental import pallas as pl
from jax.experimental.pallas import tpu as pltpu

F32 = jnp.float32
BF16 = jnp.bfloat16

D_MODEL = 1024
HEAD_DIM = 64
D_MIX = 2 * D_MODEL
SSD_WIDTH = 1024
SSD_HEADS = 16
SSD_GROUPS = 2
SSD_STATE = 128
SSD_CONV = 4
SSD_CHUNK = 128
SSD_SUB = 4
MOBA_WIDTH = 1024
MOBA_BLOCK = 256
MOBA_TOPK = 3
FOX_WIDTH = 1536
FOX_HEADS = 24
S5_WIDTH = 512
S5_GROUP = 16
S5_GROUP_SHIFT = 4
S5_GROUPS = 32
S5_STATE = 64
S5_STATE_SHIFT = 6
S5_SETS = 4
S5_STEPS = 64
RMS_EPS = 1e-6
ATT_BLOCK = MOBA_BLOCK
ATT_BLOCK_SHIFT = 8
LOG2E = math.log2(math.e)
ATT_SCALE = LOG2E / math.sqrt(HEAD_DIM)
VT_ROWS = HEAD_DIM + 16
LANES = 128
NEG = -1e30
VMEM_LIMIT = 48 * 1024 * 1024
PACK_ROWS = 512

EV_ZA, EV_ZB, EV_XS, EV_Q, EV_K, EV_V, EV_BC = 0, 1024, 2048, 3072, 4096, 5120, 6144
OD_ZC, OD_ZD, OD_Q, OD_K, OD_V, OD_U, OD_N = 0, 1536, 2048, 3584, 5120, 6656, 7168


def _nt(a, b):
    return lax.dot_general(a, b, (((1,), (1,)), ((), ())), preferred_element_type=F32)


def _dot(a, b):
    return jnp.dot(a, b, preferred_element_type=F32)


def _split3(x):
    hi = x.astype(BF16)
    r = x - hi.astype(F32)
    mid = r.astype(BF16)
    lo = (r - mid.astype(F32)).astype(BF16)
    return hi, mid, lo


def _dot_exact_rhs(x, m_bf16):
    hi, mid, lo = _split3(x)
    return _dot(hi, m_bf16) + _dot(mid, m_bf16) + _dot(lo, m_bf16)


def _sigmoid(x):
    return 0.5 + 0.5 * jnp.tanh(0.5 * x)


def _silu(x):
    h = 0.5 * x
    return h + h * jnp.tanh(h)


def _softplus(x):
    return jnp.maximum(x, 0.0) + jnp.log1p(jnp.exp(-jnp.abs(x)))


def _params(sem, limit=VMEM_LIMIT):
    return pltpu.CompilerParams(dimension_semantics=sem, vmem_limit_bytes=limit)


def _ada_kernel(c_ref, w_ref, b_ref, o_ref):
    cond = _silu(c_ref[...])
    hi, mid, lo = _split3(cond)
    w = w_ref[0]
    whi, wmid, wlo = _split3(w)
    acc = _dot(hi, whi) + _dot(hi, wmid) + _dot(mid, whi)
    acc = acc + _dot(hi, wlo) + _dot(mid, wmid) + _dot(lo, whi)
    o_ref[0] = acc + b_ref[0]


def _ada_mod(c, ada_w, ada_b):
    depth, d, d3 = ada_w.shape
    bsz = c.shape[0]
    nj = d3 // d
    return pl.pallas_call(
        _ada_kernel,
        grid=(depth, nj),
        in_specs=[
            pl.BlockSpec((bsz, d), lambda l, j: (0, 0)),
            pl.BlockSpec((1, d, d), lambda l, j: (l, 0, j)),
            pl.BlockSpec((1, 1, d), lambda l, j: (l, 0, j)),
        ],
        out_specs=pl.BlockSpec((1, bsz, d), lambda l, j: (l, 0, j)),
        out_shape=jax.ShapeDtypeStruct((depth, bsz, d3), F32),
        compiler_params=_params(("parallel", "parallel")),
        name="ada_mod",
    )(c, ada_w, ada_b.reshape(depth, 1, d3))


def _inproj_kernel(x_ref, sc_ref, sh_ref, g_ref, w_ref, ws_ref, o_ref, os_ref, h_scr):
    @pl.when(pl.program_id(1) == 0)
    def _():
        x = x_ref[...]
        ms = jnp.mean(x * x, axis=-1, keepdims=True)
        xn = x * lax.rsqrt(ms + RMS_EPS) * g_ref[...]
        h = xn * (1.0 + sc_ref[0]) + sh_ref[0]
        h_b = h.astype(BF16)
        h_scr[...] = h_b
        r = _nt(h_b, ws_ref[...])
        os_ref[...] = r[:, :LANES] + r[:, LANES:]

    o_ref[...] = _nt(h_scr[...], w_ref[...]).astype(BF16)


def _inproj(x2, scale, shift, g, w, ws, seq, tm=1024):
    t, d = x2.shape
    n = w.shape[0]
    tn = n // 2
    per = seq // tm
    return pl.pallas_call(
        _inproj_kernel,
        grid=(t // tm, n // tn),
        in_specs=[
            pl.BlockSpec((tm, d), lambda i, j: (i, 0)),
            pl.BlockSpec((1, 1, d), lambda i, j: (i // per, 0, 0)),
            pl.BlockSpec((1, 1, d), lambda i, j: (i // per, 0, 0)),
            pl.BlockSpec((1, d), lambda i, j: (0, 0)),
            pl.BlockSpec((tn, d), lambda i, j: (j, 0)),
            pl.BlockSpec((2 * LANES, d), lambda i, j: (0, 0)),
        ],
        out_specs=[
            pl.BlockSpec((tm, tn), lambda i, j: (i, j)),
            pl.BlockSpec((tm, LANES), lambda i, j: (i, 0)),
        ],
        out_shape=[
            jax.ShapeDtypeStruct((t, n), BF16),
            jax.ShapeDtypeStruct((t, LANES), F32),
        ],
        scratch_shapes=[pltpu.VMEM((tm, d), BF16)],
        compiler_params=_params(("parallel", "arbitrary")),
        name="in_proj",
    )(x2, scale, shift, g, w, ws)


def _outproj_kernel(a_ref, b_ref, wa_ref, wb_ref, x_ref, gate_ref, pg_ref, o_ref):
    y = _dot(a_ref[...], wa_ref[...]) + _dot(b_ref[...], wb_ref[...])
    ms = jnp.mean(y * y, axis=-1, keepdims=True)
    yn = y * lax.rsqrt(ms + RMS_EPS) * pg_ref[...]
    o_ref[...] = x_ref[...] + gate_ref[0] * yn


def _outproj(a, b, wa, wb, x2, gate, pg, seq, tm=1024):
    t, d = x2.shape
    ka, kb = a.shape[1], b.shape[1]
    per = seq // tm
    return pl.pallas_call(
        _outproj_kernel,
        grid=(t // tm,),
        in_specs=[
            pl.BlockSpec((tm, ka), lambda i: (i, 0)),
            pl.BlockSpec((tm, kb), lambda i: (i, 0)),
            pl.BlockSpec((ka, d), lambda i: (0, 0)),
            pl.BlockSpec((kb, d), lambda i: (0, 0)),
            pl.BlockSpec((tm, d), lambda i: (i, 0)),
            pl.BlockSpec((1, 1, d), lambda i: (i // per, 0, 0)),
            pl.BlockSpec((1, d), lambda i: (0, 0)),
        ],
        out_specs=pl.BlockSpec((tm, d), lambda i: (i, 0)),
        out_shape=jax.ShapeDtypeStruct((t, d), F32),
        compiler_params=_params(("parallel",)),
        name="out_proj",
    )(a, b, wa, wb, x2, gate, pg)


def _causal_conv_silu(raw, prev, shift, w, b):
    lc = raw.shape[0]
    ext = jnp.concatenate([prev, raw], axis=0)
    shifted = _dot(shift, ext)
    acc = b + w[SSD_CONV - 1:SSD_CONV, :] * raw.astype(F32)
    for j in range(SSD_CONV - 1):
        acc = acc + w[j:j + 1, :] * shifted[j * lc:(j + 1) * lc, :]
    return _silu(acc)


def _ssd_kernel(z_ref, xs_ref, bc_ref, xsp_ref, bcp_ref, dt_ref, cwx_ref, cwb_ref, cbx_ref, cbb_ref, dtb_ref,
                alog_ref, dexp_ref, ng_ref, e_ref, sh_ref, o_ref, state_scr, xs_scr, bc_scr):
    lc = SSD_CHUNK
    hw = SSD_HEADS // SSD_GROUPS * HEAD_DIM
    first = pl.program_id(1) == 0
    dyn0 = jnp.minimum(pl.program_id(1), 0)

    @pl.when(first)
    def _():
        state_scr[...] = jnp.zeros_like(state_scr)

    rows = lax.broadcasted_iota(jnp.int32, (lc, lc), 0)
    cols = lax.broadcasted_iota(jnp.int32, (lc, lc), 1)
    lower = rows >= cols
    state = [state_scr[:, g * hw:(g + 1) * hw] for g in range(SSD_GROUPS)]
    prev_x = jnp.where(first, jnp.zeros_like(xsp_ref[...]), xsp_ref[...])
    prev_b = jnp.where(first, jnp.zeros_like(bcp_ref[...]), bcp_ref[...])
    for j in range(xs_ref.shape[0] // lc):
        sub = slice(j * lc, (j + 1) * lc)
        raw_x = xs_ref[sub, :]
        raw_b = bc_ref[sub, :]
        y, state = _ssd_chunk(raw_x, raw_b, prev_x, prev_b, dt_ref[sub, :], z_ref[sub, :], state, lower,
                              cwx_ref, cwb_ref, cbx_ref, cbb_ref, dtb_ref, alog_ref, dexp_ref, e_ref, sh_ref,
                              xs_scr, bc_scr, j + dyn0)
        ms = jnp.mean(y * y, axis=-1, keepdims=True)
        o_ref[sub, :] = (y * lax.rsqrt(ms + RMS_EPS) * ng_ref[...]).astype(BF16)
        prev_x, prev_b = raw_x, raw_b
    for g in range(SSD_GROUPS):
        state_scr[:, g * hw:(g + 1) * hw] = state[g]


def _ssd_chunk(raw_x, raw_b, prev_x, prev_b, dt_raw, z, state, lower,
               cwx_ref, cwb_ref, cbx_ref, cbb_ref, dtb_ref, alog_ref, dexp_ref, e_ref, sh_ref, xs_scr, bc_scr, slot):
    lc = SSD_CHUNK
    hw = SSD_HEADS // SSD_GROUPS * HEAD_DIM
    xs_scr[slot] = _causal_conv_silu(raw_x, prev_x, sh_ref[...], cwx_ref[...], cbx_ref[...])
    bc_scr[slot] = _causal_conv_silu(raw_b, prev_b, sh_ref[...], cwb_ref[...], cbb_ref[...])
    xs = xs_scr[slot]
    bc = bc_scr[slot]

    dt = _softplus(dt_raw + dtb_ref[...])
    adt = dt * (-jnp.exp(alog_ref[...]))
    a_cum = _dot_exact_rhs_lhs(lower.astype(BF16), adt)
    a_cum_t = a_cum.T
    a_last = a_cum[lc - 1:lc, :]
    stacked = jnp.concatenate([dt, jnp.exp(a_last - a_cum), jnp.exp(a_cum)], axis=0)
    expanded = _dot(stacked.astype(BF16), e_ref[...])
    dt_e = expanded[0:lc]
    dec_e = expanded[lc:2 * lc]
    ea_e = expanded[2 * lc:3 * lc]
    al_e = _dot_exact_rhs(jnp.broadcast_to(jnp.exp(a_last), (8, LANES)), e_ref[...])[0:1]

    xdt = xs * dt_e
    xdt_b = xdt.astype(BF16)
    xd_b = (xdt * dec_e).astype(BF16)
    lane = lax.broadcasted_iota(jnp.int32, (1, LANES), 1)
    pieces = []
    new_state = []
    for g in range(SSD_GROUPS):
        bg = bc[:, g * SSD_STATE:(g + 1) * SSD_STATE]
        cg = bc[:, (SSD_GROUPS + g) * SSD_STATE:(SSD_GROUPS + g + 1) * SSD_STATE]
        cg_b = cg.astype(BF16)
        cb = _nt(cg_b, bg.astype(BF16))
        st = state[g]
        y_off = _dot(cg_b, st.astype(BF16)) * ea_e[:, g * hw:(g + 1) * hw]
        new_st = _dot(bg.T.astype(BF16), xd_b[:, g * hw:(g + 1) * hw])
        new_state.append(al_e[:, g * hw:(g + 1) * hw] * st + new_st)
        for pair in range(SSD_HEADS // SSD_GROUPS // 2):
            acc = None
            c0 = g * hw + pair * LANES
            xp = xdt_b[:, c0:c0 + LANES]
            for half in range(2):
                h = g * (SSD_HEADS // SSD_GROUPS) + pair * 2 + half
                diff = a_cum[:, h:h + 1] - a_cum_t[h:h + 1, :]
                decay = jnp.exp(jnp.where(lower, diff, NEG))
                m = (cb * decay).astype(BF16)
                in_half = (lane >= half * HEAD_DIM) & (lane < (half + 1) * HEAD_DIM)
                part = _dot(m, jnp.where(in_half, xp, jnp.zeros_like(xp)))
                acc = part if acc is None else acc + part
            pieces.append(acc + y_off[:, pair * LANES:(pair + 1) * LANES])
    y = jnp.concatenate(pieces, axis=1) + dexp_ref[...] * xs_scr[slot]
    return y * _silu(z.astype(F32)), new_state


def _dot_exact_rhs_lhs(m_bf16, x):
    hi, mid, lo = _split3(x)
    return _dot(m_bf16, hi) + _dot(m_bf16, mid) + _dot(m_bf16, lo)


def _ssd(proj, small, conv_w, conv_b, dt_bias, a_log, d_skip, norm_g, bsz, seq):
    lc = SSD_CHUNK
    nc = seq // lc
    t = bsz * seq
    pad = LANES - SSD_HEADS
    dtb = jnp.pad(dt_bias, (0, pad)).reshape(1, LANES)
    alog = jnp.pad(a_log, (0, pad)).reshape(1, LANES)
    dexp = jnp.repeat(d_skip, HEAD_DIM).reshape(1, SSD_WIDTH)
    expand = (jnp.arange(LANES)[:, None] == (jnp.arange(SSD_WIDTH) // HEAD_DIM)[None, :]).astype(BF16)
    nbc = 2 * SSD_GROUPS * SSD_STATE
    src = lc + jnp.arange(lc)[None, :] - (SSD_CONV - 1) + jnp.arange(SSD_CONV - 1)[:, None]
    shift = (src.reshape(-1, 1) == jnp.arange(2 * lc)[None, :]).astype(BF16)
    ns = nc // SSD_SUB
    rs = SSD_SUB * lc
    row = lambda b, c: b * ns + c
    prev = lambda b, c: b * nc + jnp.maximum(SSD_SUB * c - 1, 0)
    const = lambda b, c: (0, 0)
    return pl.pallas_call(
        _ssd_kernel,
        grid=(bsz, ns),
        in_specs=[
            pl.BlockSpec((rs, SSD_WIDTH), lambda b, c: (row(b, c), EV_ZA // SSD_WIDTH)),
            pl.BlockSpec((rs, SSD_WIDTH), lambda b, c: (row(b, c), EV_XS // SSD_WIDTH)),
            pl.BlockSpec((rs, nbc), lambda b, c: (row(b, c), EV_BC // nbc)),
            pl.BlockSpec((lc, SSD_WIDTH), lambda b, c: (prev(b, c), EV_XS // SSD_WIDTH)),
            pl.BlockSpec((lc, nbc), lambda b, c: (prev(b, c), EV_BC // nbc)),
            pl.BlockSpec((rs, LANES), lambda b, c: (row(b, c), 0)),
            pl.BlockSpec((SSD_CONV, SSD_WIDTH), const),
            pl.BlockSpec((SSD_CONV, nbc), const),
            pl.BlockSpec((1, SSD_WIDTH), const),
            pl.BlockSpec((1, nbc), const),
            pl.BlockSpec((1, LANES), const),
            pl.BlockSpec((1, LANES), const),
            pl.BlockSpec((1, SSD_WIDTH), const),
            pl.BlockSpec((1, SSD_WIDTH), const),
            pl.BlockSpec((LANES, SSD_WIDTH), const),
            pl.BlockSpec(((SSD_CONV - 1) * lc, 2 * lc), const),
        ],
        out_specs=pl.BlockSpec((rs, SSD_WIDTH), lambda b, c: (row(b, c), 0)),
        out_shape=jax.ShapeDtypeStruct((t, SSD_WIDTH), BF16),
        scratch_shapes=[pltpu.VMEM((SSD_STATE, SSD_WIDTH), F32),
                        pltpu.VMEM((SSD_SUB, lc, SSD_WIDTH), F32),
                        pltpu.VMEM((SSD_SUB, lc, nbc), F32)],
        compiler_params=_params(("parallel", "arbitrary")),
        name="ssd",
    )(proj, proj, proj, proj, proj, small,
      conv_w[:, :SSD_WIDTH], conv_w[:, SSD_WIDTH:], conv_b[:SSD_WIDTH].reshape(1, -1),
      conv_b[SSD_WIDTH:].reshape(1, -1), dtb, alog, dexp, norm_g.reshape(1, -1), expand, shift)


def _fold_rows(x, op):
    out = x[0:8, :]
    for i in range(1, x.shape[0] // 8):
        out = op(out, x[8 * i:8 * (i + 1), :])
    return out


def _attend_pair(qa_scr, ka_scr, vt_scr, s_scr, p_scr, ot_scr, bias_fn):
    blk = ATT_BLOCK
    nb = qa_scr.shape[1] // blk
    heads = (0, 1)
    dyn0 = jnp.minimum(pl.program_id(1), 0)
    causal = (lax.broadcasted_iota(jnp.int32, (blk, blk), 0)
              <= lax.broadcasted_iota(jnp.int32, (blk, blk), 1))

    def score_tile(half, qb, n, m8):
        s = _nt(ka_scr[half, n * blk:(n + 1) * blk, :], qa_scr[half, qb * blk:(qb + 1) * blk, :])
        bias = bias_fn(half, qb, n)
        if bias is not None:
            s = s + bias
        if n == qb:
            s = jnp.where(causal, s, NEG)
        s_scr[half, qb % 2 + dyn0, n] = s
        part = _fold_rows(s, jnp.maximum)
        return part if m8 is None else jnp.maximum(m8, part)

    def prob_tile(half, qb, n, m):
        p = jnp.exp2(s_scr[half, qb % 2 + dyn0, n] - m)
        p_scr[half, qb % 2, n * blk:(n + 1) * blk, :] = p.astype(BF16)

    m8 = [score_tile(half, 0, 0, None) for half in heads]
    for qb in range(nb):
        m = [jnp.max(m8[half], axis=0, keepdims=True) for half in heads]
        nxt = qb + 1
        m8 = [None, None]
        for n in range(nxt + 1):
            for half in heads:
                if n <= qb:
                    prob_tile(half, qb, n, m[half])
            for half in heads:
                if nxt < nb:
                    m8[half] = score_tile(half, nxt, n, m8[half])
        keys = (qb + 1) * blk
        for half in heads:
            acc = _dot(vt_scr[half, :, 0:keys], p_scr[half, qb % 2, 0:keys, :])
            out = acc[0:HEAD_DIM] / acc[HEAD_DIM:HEAD_DIM + 1]
            ot_scr[half * HEAD_DIM:(half + 1) * HEAD_DIM, qb * blk:(qb + 1) * blk] = out


def _store_vt(v_ref, vt_scr):
    ones = jnp.ones((VT_ROWS - HEAD_DIM, ATT_BLOCK), BF16)
    for n in range(v_ref.shape[0] // ATT_BLOCK):
        cols = slice(n * ATT_BLOCK, (n + 1) * ATT_BLOCK)
        vt = v_ref[cols, :].astype(F32).T.astype(BF16)
        for half in range(2):
            vt_scr[half, 0:HEAD_DIM, cols] = vt[half * HEAD_DIM:(half + 1) * HEAD_DIM, :]
            vt_scr[half, HEAD_DIM:VT_ROWS, cols] = ones


def _gated_output(ot_scr, z_ref, o_ref):
    for i in range(ot_scr.shape[1] // ATT_BLOCK):
        rows = slice(i * ATT_BLOCK, (i + 1) * ATT_BLOCK)
        o = ot_scr[:, rows].T
        o_ref[rows, :] = (o * _silu(z_ref[rows, :].astype(F32))).astype(BF16)


def _head_lanes(half):
    lane = lax.broadcasted_iota(jnp.int32, (1, LANES), 1)
    return (lane >= half * HEAD_DIM) & (lane < (half + 1) * HEAD_DIM)


def _moba_kernel(q_ref, k_ref, v_ref, z_ref, o_ref, vt_scr, kbar_scr, bias_scr, qa_scr, ka_scr, s_scr, p_scr,
                 ot_scr):
    blk = ATT_BLOCK
    seq = q_ref.shape[0]
    nb = seq // blk
    _store_vt(v_ref, vt_scr)
    for n in range(nb):
        kb = k_ref[n * blk:(n + 1) * blk, :].astype(F32)
        kbar_scr[n:n + 1, :] = jnp.mean(kb, axis=0, keepdims=True)
    n_idx = lax.broadcasted_iota(jnp.int32, (nb, seq), 0)
    q_blk = lax.broadcasted_iota(jnp.int32, (nb, seq), 1) >> ATT_BLOCK_SHIFT

    for half in range(2):
        in_half = _head_lanes(half)
        q = q_ref[...]
        qi = jnp.where(in_half, q, jnp.zeros_like(q))
        qa_scr[half] = qi
        ka_scr[half] = k_ref[...]
        kbar = jnp.where(in_half, kbar_scr[...], 0.0)
        kb_hi = kbar.astype(BF16)
        kb_lo = (kbar - kb_hi.astype(F32)).astype(BF16)
        gate = _nt(kb_hi, qi) + _nt(kb_lo, qi)
        rank = jnp.zeros((nb, seq), F32)
        for mth in range(nb):
            gm = gate[mth:mth + 1, :]
            beats = (gm > gate) | ((gm == gate) & (mth < n_idx))
            rank = rank + jnp.where(beats & (mth < q_blk), 1.0, 0.0)
        chosen = (rank < float(MOBA_TOPK)) & (n_idx < q_blk)
        bias_scr[half] = jnp.where(chosen, 0.0, NEG)

    def bias_fn(half, qb, n):
        if n == qb or qb <= MOBA_TOPK:
            return None
        return bias_scr[half, n:n + 1, qb * blk:(qb + 1) * blk]

    _attend_pair(qa_scr, ka_scr, vt_scr, s_scr, p_scr, ot_scr, bias_fn)
    _gated_output(ot_scr, z_ref, o_ref)


def _moba(proj, bsz, seq):
    blk = ATT_BLOCK
    nb = seq // blk
    pairs = MOBA_WIDTH // LANES
    t = bsz * seq
    return pl.pallas_call(
        _moba_kernel,
        grid=(bsz, pairs),
        in_specs=[
            pl.BlockSpec((seq, LANES), lambda b, p: (b, EV_Q // LANES + p)),
            pl.BlockSpec((seq, LANES), lambda b, p: (b, EV_K // LANES + p)),
            pl.BlockSpec((seq, LANES), lambda b, p: (b, EV_V // LANES + p)),
            pl.BlockSpec((seq, LANES), lambda b, p: (b, EV_ZB // LANES + p)),
        ],
        out_specs=pl.BlockSpec((seq, LANES), lambda b, p: (b, p)),
        out_shape=jax.ShapeDtypeStruct((t, MOBA_WIDTH), BF16),
        scratch_shapes=[
            pltpu.VMEM((2, VT_ROWS, seq), BF16),
            pltpu.VMEM((nb, LANES), F32),
            pltpu.VMEM((2, nb, seq), F32),
            pltpu.VMEM((2, seq, LANES), BF16),
            pltpu.VMEM((2, seq, LANES), BF16),
            pltpu.VMEM((2, 2, nb, blk, blk), F32),
            pltpu.VMEM((2, 2, seq, blk), BF16),
            pltpu.VMEM((LANES, seq), F32),
        ],
        compiler_params=_params(("parallel", "parallel")),
        name="moba",
    )(proj, proj, proj, proj)


def _fgate_kernel(f_ref, fb_ref, o_ref):
    blk = ATT_BLOCK
    rows = lax.broadcasted_iota(jnp.int32, (blk, blk), 0)
    cols = lax.broadcasted_iota(jnp.int32, (blk, blk), 1)
    lower = (rows >= cols).astype(BF16)
    carry = jnp.zeros((1, LANES), F32)
    for i in range(f_ref.shape[0] // blk):
        nlf = _softplus(-(f_ref[i * blk:(i + 1) * blk, :] + fb_ref[...]))
        csum = _dot_exact_rhs_lhs(lower, nlf) + carry
        o_ref[i * blk:(i + 1) * blk, :] = csum
        carry = csum[blk - 1:blk, :]


def _fgate(small, fgate_b, bsz, seq):
    fb = jnp.pad(fgate_b, (0, LANES - FOX_HEADS)).reshape(1, LANES)
    return pl.pallas_call(
        _fgate_kernel,
        grid=(bsz,),
        in_specs=[
            pl.BlockSpec((seq, LANES), lambda b: (b, 0)),
            pl.BlockSpec((1, LANES), lambda b: (0, 0)),
        ],
        out_specs=pl.BlockSpec((seq, LANES), lambda b: (b, 0)),
        out_shape=jax.ShapeDtypeStruct((bsz * seq, LANES), F32),
        compiler_params=_params(("parallel",)),
        name="fox_gate",
    )(small, fb)


def _fox_kernel(q_ref, k_ref, v_ref, z_ref, nf_ref, o_ref, vt_scr, fb_scr, qa_scr, ka_scr, s_scr, p_scr, ot_scr):
    pair = pl.program_id(1)
    nb = q_ref.shape[0] // ATT_BLOCK
    blk = ATT_BLOCK
    _store_vt(v_ref, vt_scr)
    lane = lax.broadcasted_iota(jnp.int32, (1, LANES), 1)
    for half in range(2):
        own_lane = lane == pair * 2 + half
        for n in range(nb):
            col = jnp.sum(jnp.where(own_lane, nf_ref[n * blk:(n + 1) * blk, :], 0.0), axis=1, keepdims=True)
            fb_scr[half, n] = jnp.broadcast_to(col * LOG2E, (blk, LANES))
        q = q_ref[...]
        qa_scr[half] = jnp.where(_head_lanes(half), q, jnp.zeros_like(q))
        ka_scr[half] = k_ref[...]

    def bias_fn(half, qb, n):
        fb = fb_scr[half, n]
        return jnp.concatenate([fb, fb], axis=1)

    _attend_pair(qa_scr, ka_scr, vt_scr, s_scr, p_scr, ot_scr, bias_fn)
    _gated_output(ot_scr, z_ref, o_ref)


def _fox(proj, negf, bsz, seq):
    blk = ATT_BLOCK
    nb = seq // blk
    pairs = FOX_WIDTH // LANES
    t = bsz * seq
    return pl.pallas_call(
        _fox_kernel,
        grid=(bsz, pairs),
        in_specs=[
            pl.BlockSpec((seq, LANES), lambda b, p: (b, OD_Q // LANES + p)),
            pl.BlockSpec((seq, LANES), lambda b, p: (b, OD_K // LANES + p)),
            pl.BlockSpec((seq, LANES), lambda b, p: (b, OD_V // LANES + p)),
            pl.BlockSpec((seq, LANES), lambda b, p: (b, OD_ZC // LANES + p)),
            pl.BlockSpec((seq, LANES), lambda b, p: (b, 0)),
        ],
        out_specs=pl.BlockSpec((seq, LANES), lambda b, p: (b, p)),
        out_shape=jax.ShapeDtypeStruct((t, FOX_WIDTH), BF16),
        scratch_shapes=[
            pltpu.VMEM((2, VT_ROWS, seq), BF16),
            pltpu.VMEM((2, nb, blk, LANES), F32),
            pltpu.VMEM((2, seq, LANES), BF16),
            pltpu.VMEM((2, seq, LANES), BF16),
            pltpu.VMEM((2, 2, nb, blk, blk), F32),
            pltpu.VMEM((2, 2, seq, blk), BF16),
            pltpu.VMEM((LANES, seq), F32),
        ],
        compiler_params=_params(("parallel", "parallel")),
        name="fox",
    )(proj, proj, proj, proj, negf)


def _s5scan_kernel(u_ref, z_ref, perm_ref, permt_ref, lr_ref, li_ref, ldt_ref, bwr_ref, bwi_ref, cwr_ref, cwi_ref,
                   d_ref, gw_ref, gb_ref, y_ref, bre_scr, bim_scr, cre_scr, cim_scr, ar_scr, ai_scr, xr_scr, xi_scr, zr_scr, zi_scr,
                   *, bsz):
    nset, cw, sw = bre_scr.shape

    @pl.when(pl.program_id(0) == 0)
    def _():
        chan_grp = lax.broadcasted_iota(jnp.int32, (cw, sw), 0) >> S5_GROUP_SHIFT
        state_grp = lax.broadcasted_iota(jnp.int32, (cw, sw), 1) >> S5_STATE_SHIFT
        same_b = chan_grp == state_grp
        same_c = ((lax.broadcasted_iota(jnp.int32, (sw, cw), 0) >> S5_STATE_SHIFT)
                  == (lax.broadcasted_iota(jnp.int32, (sw, cw), 1) >> S5_GROUP_SHIFT))
        for s in range(nset):
            lr = lr_ref[s]
            li = li_ref[s]
            dt = jnp.exp(ldt_ref[s])
            mag = jnp.exp(lr * dt)
            ar = mag * jnp.cos(li * dt)
            ai = mag * jnp.sin(li * dt)
            den = lr * lr + li * li
            qr = ((ar - 1.0) * lr + ai * li) / den
            qi = (ai * lr - (ar - 1.0) * li) / den
            ar_scr[s] = jnp.broadcast_to(ar, (bsz, sw))
            ai_scr[s] = jnp.broadcast_to(ai, (bsz, sw))
            bwr = bwr_ref[s]
            bwi = bwi_ref[s]
            bre_scr[s] = jnp.where(same_b, qr * bwr - qi * bwi, 0.0).astype(BF16)
            bim_scr[s] = jnp.where(same_b, qr * bwi + qi * bwr, 0.0).astype(BF16)
            cre_scr[s] = jnp.where(same_c, cwr_ref[s], 0.0).astype(BF16)
            cim_scr[s] = jnp.where(same_c, cwi_ref[s], 0.0).astype(BF16)
        xr_scr[...] = jnp.zeros_like(xr_scr)
        xi_scr[...] = jnp.zeros_like(xi_scr)

    steps = u_ref.shape[1]
    width = u_ref.shape[2]
    u_tb = _dot(perm_ref[...], u_ref[...].reshape(bsz * steps, width))
    u_b = u_tb.astype(BF16)
    half_rows = (steps // 2) * bsz

    def project(half, s):
        rows = slice(half * half_rows, (half + 1) * half_rows)
        us = u_b[rows, s * cw:(s + 1) * cw]
        zr_scr[s, rows, :] = _dot(us, bre_scr[s])
        zi_scr[s, rows, :] = _dot(us, bim_scr[s])

    for s in range(nset):
        project(0, s)

    def step(t, carry):
        r0 = t * bsz
        out = []
        for s in range(nset):
            xr, xi = carry[2 * s], carry[2 * s + 1]
            ar = ar_scr[s]
            ai = ai_scr[s]
            nr = ar * xr - ai * xi + zr_scr[s, r0:r0 + bsz, :]
            ni = ar * xi + ai * xr + zi_scr[s, r0:r0 + bsz, :]
            zr_scr[s, r0:r0 + bsz, :] = nr
            zi_scr[s, r0:r0 + bsz, :] = ni
            out += [nr, ni]
        return out

    def readout(half, s):
        rows = slice(half * half_rows, (half + 1) * half_rows)
        return _dot(zr_scr[s, rows, :].astype(BF16), cre_scr[s]) - _dot(zi_scr[s, rows, :].astype(BF16), cim_scr[s])

    every = steps // 2 // nset
    carry = []
    for s in range(nset):
        carry += [xr_scr[s], xi_scr[s]]
    for t in range(steps // 2):
        carry = step(t, carry)
        if t % every == 0:
            project(1, t // every)
    first_half = []
    for t in range(steps // 2, steps):
        carry = step(t, carry)
        k = t - steps // 2
        if k % every == 0:
            first_half.append(readout(0, k // every))
    for s in range(nset):
        xr_scr[s] = carry[2 * s]
        xi_scr[s] = carry[2 * s + 1]
    second_half = [readout(1, s) for s in range(nset)]
    xc = jnp.concatenate([jnp.concatenate(first_half, axis=1), jnp.concatenate(second_half, axis=1)], axis=0)
    y = (xc + d_ref[...] * u_tb).astype(BF16)
    y = _dot(permt_ref[...], y)
    y = 0.5 * y * (1.0 + jnp.tanh(math.sqrt(2.0 / math.pi) * (y + 0.044715 * (y * y * y))))
    y = y * _sigmoid(_dot(y.astype(BF16), gw_ref[...]) + gb_ref[...])
    z = z_ref[...].reshape(bsz * steps, width).astype(F32)
    y_ref[...] = (y * _silu(z)).astype(BF16).reshape(bsz, steps, width)


def _s5scan(proj3, lam_re, lam_im, log_dt, b_re, b_im, c_re, c_im, d_skip, glu_w, glu_b):
    bsz, seq, _ = proj3.shape
    width = S5_WIDTH
    nset = S5_SETS
    cw = width // nset
    gs = S5_GROUPS // nset
    sw = gs * S5_STATE
    blk = S5_STEPS * bsz
    tb = jnp.arange(blk)
    perm = ((tb % bsz) * S5_STEPS + tb // bsz)[:, None] == jnp.arange(blk)[None, :]
    vec = lambda a: a.reshape(nset, 1, sw)
    bw = lambda b: jnp.tile(jnp.swapaxes(b, 1, 2).reshape(nset, cw, S5_STATE), (1, 1, gs))
    cw_t = lambda c: jnp.tile(
        jnp.swapaxes(c, 1, 2).reshape(nset, gs, S5_STATE, S5_GROUP).transpose(0, 2, 1, 3).reshape(nset, S5_STATE, cw),
        (1, gs, 1))
    full3 = lambda a, b, c: pl.BlockSpec((a, b, c), lambda i: (0, 0, 0))
    return pl.pallas_call(
        functools.partial(_s5scan_kernel, bsz=bsz),
        grid=(seq // S5_STEPS,),
        in_specs=[
            pl.BlockSpec((bsz, S5_STEPS, width), lambda i: (0, i, OD_U // width)),
            pl.BlockSpec((bsz, S5_STEPS, width), lambda i: (0, i, OD_ZD // width)),
            pl.BlockSpec((blk, blk), lambda i: (0, 0)),
            pl.BlockSpec((blk, blk), lambda i: (0, 0)),
            full3(nset, 1, sw), full3(nset, 1, sw), full3(nset, 1, sw),
            full3(nset, cw, sw), full3(nset, cw, sw),
            full3(nset, sw, cw), full3(nset, sw, cw),
            pl.BlockSpec((1, width), lambda i: (0, 0)),
            pl.BlockSpec((width, width), lambda i: (0, 0)),
            pl.BlockSpec((1, width), lambda i: (0, 0)),
        ],
        out_specs=pl.BlockSpec((bsz, S5_STEPS, width), lambda i: (0, i, 0)),
        out_shape=jax.ShapeDtypeStruct((bsz, seq, width), BF16),
        scratch_shapes=[
            pltpu.VMEM((nset, cw, sw), BF16), pltpu.VMEM((nset, cw, sw), BF16),
            pltpu.VMEM((nset, sw, cw), BF16), pltpu.VMEM((nset, sw, cw), BF16),
            pltpu.VMEM((nset, bsz, sw), F32), pltpu.VMEM((nset, bsz, sw), F32),
            pltpu.VMEM((nset, bsz, sw), F32), pltpu.VMEM((nset, bsz, sw), F32),
            pltpu.VMEM((nset, blk, sw), F32), pltpu.VMEM((nset, blk, sw), F32),
        ],
        compiler_params=_params(("arbitrary",)),
        name="s5_scan",
    )(proj3, proj3, perm.astype(BF16), perm.T.astype(BF16), vec(lam_re), vec(lam_im),
      vec(jnp.repeat(log_dt, S5_STATE)), bw(b_re), bw(b_im), cw_t(c_re), cw_t(c_im), d_skip.reshape(1, width),
      glu_w.astype(BF16), glu_b.reshape(1, width))


def _pack_weights(in_w, pieces, small):
    wt = in_w.T
    d = wt.shape[1]
    starts, scales = [], []
    for a, b, scale in pieces:
        starts += list(range(a, b, PACK_ROWS))
        scales += [scale] * ((b - a) // PACK_ROWS)

    def pick(c, table):
        out = table[-1]
        for i in range(len(table) - 2, -1, -1):
            out = jnp.where(c == i, table[i], out)
        return out

    def pack_kernel(w_ref, o_ref):
        scale = pick(pl.program_id(0), [jnp.float32(s) for s in scales])
        o_ref[...] = (w_ref[...] * scale).astype(BF16)

    main = pl.pallas_call(
        pack_kernel,
        grid=(len(starts),),
        in_specs=[pl.BlockSpec((pl.Element(PACK_ROWS), pl.Element(d)),
                               lambda c: (pl.multiple_of(pick(c, [s // 8 for s in starts]) * 8, 8), 0))],
        out_specs=pl.BlockSpec((PACK_ROWS, d), lambda c: (c, 0)),
        out_shape=jax.ShapeDtypeStruct((len(starts) * PACK_ROWS, d), BF16),
        compiler_params=_params(("parallel",)),
        name="pack_w",
    )(wt)
    a, b = small
    rows = jnp.pad(wt[a:b], ((0, LANES - (b - a)), (0, 0)))
    hi = rows.astype(BF16)
    lo = (rows - hi.astype(F32)).astype(BF16)
    return main, jnp.concatenate([hi, lo], axis=0)


def _even_layer(x2, mod, pre_g, post_g, in_w, conv_w, conv_b, dt_bias, a_log, d_skip, norm_g, out_w, bsz, seq):
    d = D_MODEL
    shift, scale, gate = (mod[:, i * d:(i + 1) * d].reshape(bsz, 1, d) for i in range(3))
    o_xbc = 2 * SSD_WIDTH
    o_dt = o_xbc + SSD_WIDTH + 2 * SSD_GROUPS * SSD_STATE
    o_q = o_dt + SSD_HEADS
    w, ws = _pack_weights(in_w, (
        (0, o_xbc + SSD_WIDTH, 1.0),
        (o_q, o_q + MOBA_WIDTH, ATT_SCALE),
        (o_q + MOBA_WIDTH, in_w.shape[1], 1.0),
        (o_xbc + SSD_WIDTH, o_dt, 1.0),
    ), (o_dt, o_q))
    proj, small = _inproj(x2, scale, shift, pre_g.reshape(1, d), w, ws, seq)
    y_a = _ssd(proj, small, conv_w, conv_b, dt_bias, a_log, d_skip, norm_g, bsz, seq)
    y_b = _moba(proj, bsz, seq)
    ow = out_w.astype(BF16)
    return _outproj(y_a, y_b, ow[:SSD_WIDTH], ow[SSD_WIDTH:], x2, gate, post_g.reshape(1, d), seq)


def _odd_layer(x2, mod, pre_g, post_g, in_w, fgate_b, lam_re, lam_im, log_dt, b_re, b_im, c_re, c_im,
               d_skip, glu_w, glu_b, out_w, bsz, seq):
    d = D_MODEL
    shift, scale, gate = (mod[:, i * d:(i + 1) * d].reshape(bsz, 1, d) for i in range(3))
    o_f = D_MIX + 3 * FOX_WIDTH
    o_u = o_f + FOX_HEADS
    w, ws = _pack_weights(in_w, (
        (0, D_MIX, 1.0),
        (D_MIX, D_MIX + FOX_WIDTH, ATT_SCALE),
        (D_MIX + FOX_WIDTH, o_f, 1.0),
        (o_u, in_w.shape[1], 1.0),
    ), (o_f, o_u))
    proj, small = _inproj(x2, scale, shift, pre_g.reshape(1, d), w, ws, seq)

    negf = _fgate(small, fgate_b, bsz, seq)
    y_c = _fox(proj, negf, bsz, seq)

    y_d = _s5scan(proj.reshape(bsz, seq, OD_N), lam_re, lam_im, log_dt, b_re, b_im, c_re, c_im, d_skip,
                  glu_w, glu_b).reshape(bsz * seq, S5_WIDTH)

    ow = out_w.astype(BF16)
    return _outproj(y_c, y_d, ow[:FOX_WIDTH], ow[FOX_WIDTH:], x2, gate, post_g.reshape(1, d), seq)


def kernel(x, c, ada_w, ada_b, pre_g, post_g, even_in_w, even_conv_w, even_conv_b, even_dt_bias, even_a_log,
           even_d_skip, even_norm_g, even_out_w, odd_in_w, odd_fgate_b, odd_lam_re, odd_lam_im, odd_log_dt,
           odd_b_re, odd_b_im, odd_c_re, odd_c_im, odd_d_skip, odd_glu_w, odd_glu_b, odd_out_w):
    bsz, seq, d = x.shape
    depth = ada_w.shape[0]
    mod = _ada_mod(c, ada_w, ada_b)
    x2 = x.reshape(bsz * seq, d)
    for layer in range(depth):
        i = layer // 2
        if layer % 2 == 0:
            x2 = _even_layer(x2, mod[layer], pre_g[layer], post_g[layer], even_in_w[i], even_conv_w[i],
                             even_conv_b[i], even_dt_bias[i], even_a_log[i], even_d_skip[i], even_norm_g[i],
                             even_out_w[i], bsz, seq)
        else:
            x2 = _odd_layer(x2, mod[layer], pre_g[layer], post_g[layer], odd_in_w[i], odd_fgate_b[i],
                            odd_lam_re[i], odd_lam_im[i], odd_log_dt[i], odd_b_re[i], odd_b_im[i], odd_c_re[i],
                            odd_c_im[i], odd_d_skip[i], odd_glu_w[i], odd_glu_b[i], odd_out_w[i], bsz, seq)
    return x2.reshape(bsz, seq, d)
```

```python
import functools
import math

import jax
import jax.numpy as jnp
from jax import lax
from jax.experimental import pallas as pl
from jax.experimental.pallas import tpu as pltpu

F32 = jnp.float32
BF16 = jnp.bfloat16

D_MODEL = 1024
HEAD_DIM = 64
D_MIX = 2 * D_MODEL
SSD_WIDTH = 1024
SSD_HEADS = 16
SSD_GROUPS = 2
SSD_STATE = 128
SSD_CONV = 4
SSD_CHUNK = 128
SSD_SUB = 4
MOBA_WIDTH = 1024
MOBA_BLOCK = 256
MOBA_TOPK = 3
FOX_WIDTH = 1536
FOX_HEADS = 24
S5_WIDTH = 512
S5_GROUP = 16
S5_GROUP_SHIFT = 4
S5_GROUPS = 32
S5_STATE = 64
S5_STATE_SHIFT = 6
S5_SETS = 4
S5_STEPS = 64
RMS_EPS = 1e-6
ATT_BLOCK = MOBA_BLOCK
ATT_BLOCK_SHIFT = 8
LOG2E = math.log2(math.e)
ATT_SCALE = LOG2E / math.sqrt(HEAD_DIM)
VT_ROWS = HEAD_DIM + 16
LANES = 128
NEG = -1e30
VMEM_LIMIT = 48 * 1024 * 1024
PACK_ROWS = 512

EV_ZA, EV_ZB, EV_XS, EV_Q, EV_K, EV_V, EV_BC = 0, 1024, 2048, 3072, 4096, 5120, 6144
OD_ZC, OD_ZD, OD_Q, OD_K, OD_V, OD_U, OD_N = 0, 1536, 2048, 3584, 5120, 6656, 7168


def _nt(a, b):
    return lax.dot_general(a, b, (((1,), (1,)), ((), ())), preferred_element_type=F32)


def _dot(a, b):
    return jnp.dot(a, b, preferred_element_type=F32)


def _split3(x):
    hi = x.astype(BF16)
    r = x - hi.astype(F32)
    mid = r.astype(BF16)
    lo = (r - mid.astype(F32)).astype(BF16)
    return hi, mid, lo


def _dot_exact_rhs(x, m_bf16):
    hi, mid, lo = _split3(x)
    return _dot(hi, m_bf16) + _dot(mid, m_bf16) + _dot(lo, m_bf16)


def _sigmoid(x):
    return 0.5 + 0.5 * jnp.tanh(0.5 * x)


def _silu(x):
    h = 0.5 * x
    return h + h * jnp.tanh(h)


def _softplus(x):
    return jnp.maximum(x, 0.0) + jnp.log1p(jnp.exp(-jnp.abs(x)))


def _params(sem, limit=VMEM_LIMIT):
    return pltpu.CompilerParams(dimension_semantics=sem, vmem_limit_bytes=limit)


def _ada_kernel(c_ref, w_ref, b_ref, o_ref):
    cond = _silu(c_ref[...])
    hi, mid, lo = _split3(cond)
    w = w_ref[0]
    whi, wmid, wlo = _split3(w)
    acc = _dot(hi, whi) + _dot(hi, wmid) + _dot(mid, whi)
    acc = acc + _dot(hi, wlo) + _dot(mid, wmid) + _dot(lo, whi)
    o_ref[0] = acc + b_ref[0]


def _ada_mod(c, ada_w, ada_b):
    depth, d, d3 = ada_w.shape
    bsz = c.shape[0]
    nj = d3 // d
    return pl.pallas_call(
        _ada_kernel,
        grid=(depth, nj),
        in_specs=[
            pl.BlockSpec((bsz, d), lambda l, j: (0, 0)),
            pl.BlockSpec((1, d, d), lambda l, j: (l, 0, j)),
            pl.BlockSpec((1, 1, d), lambda l, j: (l, 0, j)),
        ],
        out_specs=pl.BlockSpec((1, bsz, d), lambda l, j: (l, 0, j)),
        out_shape=jax.ShapeDtypeStruct((depth, bsz, d3), F32),
        compiler_params=_params(("parallel", "parallel")),
        name="ada_mod",
    )(c, ada_w, ada_b.reshape(depth, 1, d3))


def _inproj_kernel(x_ref, sc_ref, sh_ref, g_ref, w_ref, ws_ref, o_ref, os_ref, h_scr):
    @pl.when(pl.program_id(1) == 0)
    def _():
        x = x_ref[...]
        ms = jnp.mean(x * x, axis=-1, keepdims=True)
        xn = x * lax.rsqrt(ms + RMS_EPS) * g_ref[...]
        h = xn * (1.0 + sc_ref[0]) + sh_ref[0]
        h_b = h.astype(BF16)
        h_scr[...] = h_b
        r = _nt(h_b, ws_ref[...])
        os_ref[...] = r[:, :LANES] + r[:, LANES:]

    o_ref[...] = _nt(h_scr[...], w_ref[...]).astype(BF16)


def _inproj(x2, scale, shift, g, w, ws, seq, tm=1024):
    t, d = x2.shape
    n = w.shape[0]
    tn = n // 2
    per = seq // tm
    return pl.pallas_call(
        _inproj_kernel,
        grid=(t // tm, n // tn),
        in_specs=[
            pl.BlockSpec((tm, d), lambda i, j: (i, 0)),
            pl.BlockSpec((1, 1, d), lambda i, j: (i // per, 0, 0)),
            pl.BlockSpec((1, 1, d), lambda i, j: (i // per, 0, 0)),
            pl.BlockSpec((1, d), lambda i, j: (0, 0)),
            pl.BlockSpec((tn, d), lambda i, j: (j, 0)),
            pl.BlockSpec((2 * LANES, d), lambda i, j: (0, 0)),
        ],
        out_specs=[
            pl.BlockSpec((tm, tn), lambda i, j: (i, j)),
            pl.BlockSpec((tm, LANES), lambda i, j: (i, 0)),
        ],
        out_shape=[
            jax.ShapeDtypeStruct((t, n), BF16),
            jax.ShapeDtypeStruct((t, LANES), F32),
        ],
        scratch_shapes=[pltpu.VMEM((tm, d), BF16)],
        compiler_params=_params(("parallel", "arbitrary")),
        name="in_proj",
    )(x2, scale, shift, g, w, ws)


def _outproj_kernel(a_ref, b_ref, wa_ref, wb_ref, x_ref, gate_ref, pg_ref, o_ref):
    half = a_ref.shape[0] // 2
    for r in range(2):
        rows = slice(r * half, (r + 1) * half)
        y = _dot(a_ref[rows, :], wa_ref[...]) + _dot(b_ref[rows, :], wb_ref[...])
        ms = jnp.mean(y * y, axis=-1, keepdims=True)
        yn = y * lax.rsqrt(ms + RMS_EPS) * pg_ref[...]
        o_ref[rows, :] = x_ref[rows, :] + gate_ref[0] * yn


def _outproj(a, b, wa, wb, x2, gate, pg, seq, tm=1024):
    t, d = x2.shape
    ka, kb = a.shape[1], b.shape[1]
    per = seq // tm
    return pl.pallas_call(
        _outproj_kernel,
        grid=(t // tm,),
        in_specs=[
            pl.BlockSpec((tm, ka), lambda i: (i, 0)),
            pl.BlockSpec((tm, kb), lambda i: (i, 0)),
            pl.BlockSpec((ka, d), lambda i: (0, 0)),
            pl.BlockSpec((kb, d), lambda i: (0, 0)),
            pl.BlockSpec((tm, d), lambda i: (i, 0)),
            pl.BlockSpec((1, 1, d), lambda i: (i // per, 0, 0)),
            pl.BlockSpec((1, d), lambda i: (0, 0)),
        ],
        out_specs=pl.BlockSpec((tm, d), lambda i: (i, 0)),
        out_shape=jax.ShapeDtypeStruct((t, d), F32),
        compiler_params=_params(("parallel",)),
        name="out_proj",
    )(a, b, wa, wb, x2, gate, pg)


def _causal_conv_silu(raw, prev, shift, w, b):
    lc = raw.shape[0]
    ext = jnp.concatenate([prev, raw], axis=0)
    shifted = _dot(shift, ext)
    acc = b + w[SSD_CONV - 1:SSD_CONV, :] * raw.astype(F32)
    for j in range(SSD_CONV - 1):
        acc = acc + w[j:j + 1, :] * shifted[j * lc:(j + 1) * lc, :]
    return _silu(acc)


def _ssd_kernel(z_ref, xs_ref, bc_ref, xsp_ref, bcp_ref, dt_ref, cwx_ref, cwb_ref, cbx_ref, cbb_ref, dtb_ref,
                alog_ref, dexp_ref, ng_ref, e_ref, sh_ref, o_ref, state_scr):
    lc = SSD_CHUNK
    hw = SSD_HEADS // SSD_GROUPS * HEAD_DIM
    first = pl.program_id(1) == 0

    @pl.when(first)
    def _():
        state_scr[...] = jnp.zeros_like(state_scr)

    rows = lax.broadcasted_iota(jnp.int32, (lc, lc), 0)
    cols = lax.broadcasted_iota(jnp.int32, (lc, lc), 1)
    lower = rows >= cols
    state = [state_scr[:, g * hw:(g + 1) * hw] for g in range(SSD_GROUPS)]
    prev_x = jnp.where(first, jnp.zeros_like(xsp_ref[...]), xsp_ref[...])
    prev_b = jnp.where(first, jnp.zeros_like(bcp_ref[...]), bcp_ref[...])
    for j in range(xs_ref.shape[0] // lc):
        sub = slice(j * lc, (j + 1) * lc)
        raw_x = xs_ref[sub, :]
        raw_b = bc_ref[sub, :]
        y, state = _ssd_chunk(raw_x, raw_b, prev_x, prev_b, dt_ref[sub, :], z_ref[sub, :], state, lower,
                              cwx_ref, cwb_ref, cbx_ref, cbb_ref, dtb_ref, alog_ref, dexp_ref, e_ref, sh_ref)
        ms = jnp.mean(y * y, axis=-1, keepdims=True)
        o_ref[sub, :] = (y * lax.rsqrt(ms + RMS_EPS) * ng_ref[...]).astype(BF16)
        prev_x, prev_b = raw_x, raw_b
    for g in range(SSD_GROUPS):
        state_scr[:, g * hw:(g + 1) * hw] = state[g]


def _ssd_chunk(raw_x, raw_b, prev_x, prev_b, dt_raw, z, state, lower,
               cwx_ref, cwb_ref, cbx_ref, cbb_ref, dtb_ref, alog_ref, dexp_ref, e_ref, sh_ref):
    lc = SSD_CHUNK
    hw = SSD_HEADS // SSD_GROUPS * HEAD_DIM
    xs = _causal_conv_silu(raw_x, prev_x, sh_ref[...], cwx_ref[...], cbx_ref[...])
    bc = _causal_conv_silu(raw_b, prev_b, sh_ref[...], cwb_ref[...], cbb_ref[...])

    dt = _softplus(dt_raw + dtb_ref[...])
    adt = dt * (-jnp.exp(alog_ref[...]))
    a_cum = _dot_exact_rhs_lhs(lower.astype(BF16), adt)
    a_cum_t = a_cum.T
    a_last = a_cum[lc - 1:lc, :]
    stacked = jnp.concatenate([dt, jnp.exp(a_last - a_cum), jnp.exp(a_cum)], axis=0)
    expanded = _dot(stacked.astype(BF16), e_ref[...])
    dt_e = expanded[0:lc]
    dec_e = expanded[lc:2 * lc]
    ea_e = expanded[2 * lc:3 * lc]
    al_e = _dot_exact_rhs(jnp.broadcast_to(jnp.exp(a_last), (8, LANES)), e_ref[...])[0:1]

    xdt = xs * dt_e
    xdt_b = xdt.astype(BF16)
    xd_b = (xdt * dec_e).astype(BF16)
    lane = lax.broadcasted_iota(jnp.int32, (1, LANES), 1)
    pieces = []
    new_state = []
    for g in range(SSD_GROUPS):
        bg = bc[:, g * SSD_STATE:(g + 1) * SSD_STATE]
        cg = bc[:, (SSD_GROUPS + g) * SSD_STATE:(SSD_GROUPS + g + 1) * SSD_STATE]
        cg_b = cg.astype(BF16)
        cb = _nt(cg_b, bg.astype(BF16))
        st = state[g]
        y_off = _dot(cg_b, st.astype(BF16)) * ea_e[:, g * hw:(g + 1) * hw]
        new_st = _dot(bg.T.astype(BF16), xd_b[:, g * hw:(g + 1) * hw])
        new_state.append(al_e[:, g * hw:(g + 1) * hw] * st + new_st)
        for pair in range(SSD_HEADS // SSD_GROUPS // 2):
            acc = None
            c0 = g * hw + pair * LANES
            xp = xdt_b[:, c0:c0 + LANES]
            for half in range(2):
                h = g * (SSD_HEADS // SSD_GROUPS) + pair * 2 + half
                diff = a_cum[:, h:h + 1] - a_cum_t[h:h + 1, :]
                decay = jnp.exp(jnp.where(lower, diff, NEG))
                m = (cb * decay).astype(BF16)
                in_half = (lane >= half * HEAD_DIM) & (lane < (half + 1) * HEAD_DIM)
                part = _dot(m, jnp.where(in_half, xp, jnp.zeros_like(xp)))
                acc = part if acc is None else acc + part
            pieces.append(acc + y_off[:, pair * LANES:(pair + 1) * LANES])
    y = jnp.concatenate(pieces, axis=1) + dexp_ref[...] * xs
    return y * _silu(z.astype(F32)), new_state


def _dot_exact_rhs_lhs(m_bf16, x):
    hi, mid, lo = _split3(x)
    return _dot(m_bf16, hi) + _dot(m_bf16, mid) + _dot(m_bf16, lo)


def _ssd(proj, small, conv_w, conv_b, dt_bias, a_log, d_skip, norm_g, bsz, seq):
    lc = SSD_CHUNK
    nc = seq // lc
    t = bsz * seq
    pad = LANES - SSD_HEADS
    dtb = jnp.pad(dt_bias, (0, pad)).reshape(1, LANES)
    alog = jnp.pad(a_log, (0, pad)).reshape(1, LANES)
    dexp = jnp.repeat(d_skip, HEAD_DIM).reshape(1, SSD_WIDTH)
    expand = (jnp.arange(LANES)[:, None] == (jnp.arange(SSD_WIDTH) // HEAD_DIM)[None, :]).astype(BF16)
    nbc = 2 * SSD_GROUPS * SSD_STATE
    src = lc + jnp.arange(lc)[None, :] - (SSD_CONV - 1) + jnp.arange(SSD_CONV - 1)[:, None]
    shift = (src.reshape(-1, 1) == jnp.arange(2 * lc)[None, :]).astype(BF16)
    ns = nc // SSD_SUB
    rs = SSD_SUB * lc
    row = lambda b, c: b * ns + c
    prev = lambda b, c: b * nc + jnp.maximum(SSD_SUB * c - 1, 0)
    const = lambda b, c: (0, 0)
    return pl.pallas_call(
        _ssd_kernel,
        grid=(bsz, ns),
        in_specs=[
            pl.BlockSpec((rs, SSD_WIDTH), lambda b, c: (row(b, c), EV_ZA // SSD_WIDTH)),
            pl.BlockSpec((rs, SSD_WIDTH), lambda b, c: (row(b, c), EV_XS // SSD_WIDTH)),
            pl.BlockSpec((rs, nbc), lambda b, c: (row(b, c), EV_BC // nbc)),
            pl.BlockSpec((lc, SSD_WIDTH), lambda b, c: (prev(b, c), EV_XS // SSD_WIDTH)),
            pl.BlockSpec((lc, nbc), lambda b, c: (prev(b, c), EV_BC // nbc)),
            pl.BlockSpec((rs, LANES), lambda b, c: (row(b, c), 0)),
            pl.BlockSpec((SSD_CONV, SSD_WIDTH), const),
            pl.BlockSpec((SSD_CONV, nbc), const),
            pl.BlockSpec((1, SSD_WIDTH), const),
            pl.BlockSpec((1, nbc), const),
            pl.BlockSpec((1, LANES), const),
            pl.BlockSpec((1, LANES), const),
            pl.BlockSpec((1, SSD_WIDTH), const),
            pl.BlockSpec((1, SSD_WIDTH), const),
            pl.BlockSpec((LANES, SSD_WIDTH), const),
            pl.BlockSpec(((SSD_CONV - 1) * lc, 2 * lc), const),
        ],
        out_specs=pl.BlockSpec((rs, SSD_WIDTH), lambda b, c: (row(b, c), 0)),
        out_shape=jax.ShapeDtypeStruct((t, SSD_WIDTH), BF16),
        scratch_shapes=[pltpu.VMEM((SSD_STATE, SSD_WIDTH), F32)],
        compiler_params=_params(("parallel", "arbitrary")),
        name="ssd",
    )(proj, proj, proj, proj, proj, small,
      conv_w[:, :SSD_WIDTH], conv_w[:, SSD_WIDTH:], conv_b[:SSD_WIDTH].reshape(1, -1),
      conv_b[SSD_WIDTH:].reshape(1, -1), dtb, alog, dexp, norm_g.reshape(1, -1), expand, shift)


def _fold_rows(x, op):
    out = x[0:8, :]
    for i in range(1, x.shape[0] // 8):
        out = op(out, x[8 * i:8 * (i + 1), :])
    return out


def _attend_pair(qa_scr, ka_scr, vt_scr, s_scr, p_scr, ot_scr, bias_fn):
    blk = ATT_BLOCK
    nb = qa_scr.shape[1] // blk
    heads = (0, 1)
    dyn0 = jnp.minimum(pl.program_id(1), 0)
    causal = (lax.broadcasted_iota(jnp.int32, (blk, blk), 0)
              <= lax.broadcasted_iota(jnp.int32, (blk, blk), 1))

    def score_tile(half, qb, n, m8):
        s = _nt(ka_scr[half, n * blk:(n + 1) * blk, :], qa_scr[half, qb * blk:(qb + 1) * blk, :])
        bias = bias_fn(half, qb, n)
        if bias is not None:
            s = s + bias
        if n == qb:
            s = jnp.where(causal, s, NEG)
        s_scr[half, qb % 2 + dyn0, n] = s
        part = _fold_rows(s, jnp.maximum)
        return part if m8 is None else jnp.maximum(m8, part)

    def prob_tile(half, qb, n, m):
        p = jnp.exp2(s_scr[half, qb % 2 + dyn0, n] - m)
        p_scr[half, qb % 2, n * blk:(n + 1) * blk, :] = p.astype(BF16)

    m8 = [score_tile(half, 0, 0, None) for half in heads]
    for qb in range(nb):
        m = [jnp.max(m8[half], axis=0, keepdims=True) for half in heads]
        nxt = qb + 1
        m8 = [None, None]
        for n in range(nxt + 1):
            for half in heads:
                if n <= qb:
                    prob_tile(half, qb, n, m[half])
            for half in heads:
                if nxt < nb:
                    m8[half] = score_tile(half, nxt, n, m8[half])
        keys = (qb + 1) * blk
        for half in heads:
            acc = _dot(vt_scr[half, :, 0:keys], p_scr[half, qb % 2, 0:keys, :])
            out = acc[0:HEAD_DIM] / acc[HEAD_DIM:HEAD_DIM + 1]
            ot_scr[half * HEAD_DIM:(half + 1) * HEAD_DIM, qb * blk:(qb + 1) * blk] = out


def _store_vt(v_ref, vt_scr):
    ones = jnp.ones((VT_ROWS - HEAD_DIM, ATT_BLOCK), BF16)
    for n in range(v_ref.shape[0] // ATT_BLOCK):
        cols = slice(n * ATT_BLOCK, (n + 1) * ATT_BLOCK)
        vt = v_ref[cols, :].astype(F32).T.astype(BF16)
        for half in range(2):
            vt_scr[half, 0:HEAD_DIM, cols] = vt[half * HEAD_DIM:(half + 1) * HEAD_DIM, :]
            vt_scr[half, HEAD_DIM:VT_ROWS, cols] = ones


def _gated_output(ot_scr, z_ref, o_ref):
    for i in range(ot_scr.shape[1] // ATT_BLOCK):
        rows = slice(i * ATT_BLOCK, (i + 1) * ATT_BLOCK)
        o = ot_scr[:, rows].T
        o_ref[rows, :] = (o * _silu(z_ref[rows, :].astype(F32))).astype(BF16)


def _head_lanes(half):
    lane = lax.broadcasted_iota(jnp.int32, (1, LANES), 1)
    return (lane >= half * HEAD_DIM) & (lane < (half + 1) * HEAD_DIM)


def _moba_kernel(q_ref, k_ref, v_ref, z_ref, o_ref, vt_scr, kbar_scr, bias_scr, qa_scr, ka_scr, s_scr, p_scr,
                 ot_scr):
    blk = ATT_BLOCK
    seq = q_ref.shape[0]
    nb = seq // blk
    _store_vt(v_ref, vt_scr)
    for n in range(nb):
        kb = k_ref[n * blk:(n + 1) * blk, :].astype(F32)
        kbar_scr[n:n + 1, :] = jnp.mean(kb, axis=0, keepdims=True)
    n_idx = lax.broadcasted_iota(jnp.int32, (nb, seq), 0)
    q_blk = lax.broadcasted_iota(jnp.int32, (nb, seq), 1) >> ATT_BLOCK_SHIFT

    for half in range(2):
        in_half = _head_lanes(half)
        q = q_ref[...]
        qi = jnp.where(in_half, q, jnp.zeros_like(q))
        qa_scr[half] = qi
        ka_scr[half] = k_ref[...]
        kbar = jnp.where(in_half, kbar_scr[...], 0.0)
        kb_hi = kbar.astype(BF16)
        kb_lo = (kbar - kb_hi.astype(F32)).astype(BF16)
        gate = _nt(kb_hi, qi) + _nt(kb_lo, qi)
        rank = jnp.zeros((nb, seq), F32)
        for mth in range(nb):
            gm = gate[mth:mth + 1, :]
            beats = (gm > gate) | ((gm == gate) & (mth < n_idx))
            rank = rank + jnp.where(beats & (mth < q_blk), 1.0, 0.0)
        chosen = (rank < float(MOBA_TOPK)) & (n_idx < q_blk)
        bias_scr[half] = jnp.where(chosen, 0.0, NEG)

    def bias_fn(half, qb, n):
        if n == qb or qb <= MOBA_TOPK:
            return None
        return bias_scr[half, n:n + 1, qb * blk:(qb + 1) * blk]

    _attend_pair(qa_scr, ka_scr, vt_scr, s_scr, p_scr, ot_scr, bias_fn)
    _gated_output(ot_scr, z_ref, o_ref)


def _moba(proj, bsz, seq):
    blk = ATT_BLOCK
    nb = seq // blk
    pairs = MOBA_WIDTH // LANES
    t = bsz * seq
    return pl.pallas_call(
        _moba_kernel,
        grid=(bsz, pairs),
        in_specs=[
            pl.BlockSpec((seq, LANES), lambda b, p: (b, EV_Q // LANES + p)),
            pl.BlockSpec((seq, LANES), lambda b, p: (b, EV_K // LANES + p)),
            pl.BlockSpec((seq, LANES), lambda b, p: (b, EV_V // LANES + p)),
            pl.BlockSpec((seq, LANES), lambda b, p: (b, EV_ZB // LANES + p)),
        ],
        out_specs=pl.BlockSpec((seq, LANES), lambda b, p: (b, p)),
        out_shape=jax.ShapeDtypeStruct((t, MOBA_WIDTH), BF16),
        scratch_shapes=[
            pltpu.VMEM((2, VT_ROWS, seq), BF16),
            pltpu.VMEM((nb, LANES), F32),
            pltpu.VMEM((2, nb, seq), F32),
            pltpu.VMEM((2, seq, LANES), BF16),
            pltpu.VMEM((2, seq, LANES), BF16),
            pltpu.VMEM((2, 2, nb, blk, blk), F32),
            pltpu.VMEM((2, 2, seq, blk), BF16),
            pltpu.VMEM((LANES, seq), F32),
        ],
        compiler_params=_params(("parallel", "parallel")),
        name="moba",
    )(proj, proj, proj, proj)


def _fgate_kernel(f_ref, fb_ref, o_ref):
    blk = ATT_BLOCK
    rows = lax.broadcasted_iota(jnp.int32, (blk, blk), 0)
    cols = lax.broadcasted_iota(jnp.int32, (blk, blk), 1)
    lower = (rows >= cols).astype(BF16)
    carry = jnp.zeros((1, LANES), F32)
    for i in range(f_ref.shape[0] // blk):
        nlf = _softplus(-(f_ref[i * blk:(i + 1) * blk, :] + fb_ref[...]))
        csum = _dot_exact_rhs_lhs(lower, nlf) + carry
        o_ref[i * blk:(i + 1) * blk, :] = csum
        carry = csum[blk - 1:blk, :]


def _fgate(small, fgate_b, bsz, seq):
    fb = jnp.pad(fgate_b, (0, LANES - FOX_HEADS)).reshape(1, LANES)
    return pl.pallas_call(
        _fgate_kernel,
        grid=(bsz,),
        in_specs=[
            pl.BlockSpec((seq, LANES), lambda b: (b, 0)),
            pl.BlockSpec((1, LANES), lambda b: (0, 0)),
        ],
        out_specs=pl.BlockSpec((seq, LANES), lambda b: (b, 0)),
        out_shape=jax.ShapeDtypeStruct((bsz * seq, LANES), F32),
        compiler_params=_params(("parallel",)),
        name="fox_gate",
    )(small, fb)


def _fox_kernel(q_ref, k_ref, v_ref, z_ref, nf_ref, o_ref, vt_scr, fb_scr, qa_scr, ka_scr, s_scr, p_scr, ot_scr):
    pair = pl.program_id(1)
    nb = q_ref.shape[0] // ATT_BLOCK
    blk = ATT_BLOCK
    _store_vt(v_ref, vt_scr)
    lane = lax.broadcasted_iota(jnp.int32, (1, LANES), 1)
    for half in range(2):
        own_lane = lane == pair * 2 + half
        for n in range(nb):
            col = jnp.sum(jnp.where(own_lane, nf_ref[n * blk:(n + 1) * blk, :], 0.0), axis=1, keepdims=True)
            fb_scr[half, n] = jnp.broadcast_to(col * LOG2E, (blk, LANES))
        q = q_ref[...]
        qa_scr[half] = jnp.where(_head_lanes(half), q, jnp.zeros_like(q))
        ka_scr[half] = k_ref[...]

    def bias_fn(half, qb, n):
        fb = fb_scr[half, n]
        return jnp.concatenate([fb, fb], axis=1)

    _attend_pair(qa_scr, ka_scr, vt_scr, s_scr, p_scr, ot_scr, bias_fn)
    _gated_output(ot_scr, z_ref, o_ref)


def _fox(proj, negf, bsz, seq):
    blk = ATT_BLOCK
    nb = seq // blk
    pairs = FOX_WIDTH // LANES
    t = bsz * seq
    return pl.pallas_call(
        _fox_kernel,
        grid=(bsz, pairs),
        in_specs=[
            pl.BlockSpec((seq, LANES), lambda b, p: (b, OD_Q // LANES + p)),
            pl.BlockSpec((seq, LANES), lambda b, p: (b, OD_K // LANES + p)),
            pl.BlockSpec((seq, LANES), lambda b, p: (b, OD_V // LANES + p)),
            pl.BlockSpec((seq, LANES), lambda b, p: (b, OD_ZC // LANES + p)),
            pl.BlockSpec((seq, LANES), lambda b, p: (b, 0)),
        ],
        out_specs=pl.BlockSpec((seq, LANES), lambda b, p: (b, p)),
        out_shape=jax.ShapeDtypeStruct((t, FOX_WIDTH), BF16),
        scratch_shapes=[
            pltpu.VMEM((2, VT_ROWS, seq), BF16),
            pltpu.VMEM((2, nb, blk, LANES), F32),
            pltpu.VMEM((2, seq, LANES), BF16),
            pltpu.VMEM((2, seq, LANES), BF16),
            pltpu.VMEM((2, 2, nb, blk, blk), F32),
            pltpu.VMEM((2, 2, seq, blk), BF16),
            pltpu.VMEM((LANES, seq), F32),
        ],
        compiler_params=_params(("parallel", "parallel")),
        name="fox",
    )(proj, proj, proj, proj, negf)


def _s5scan_kernel(u_ref, z_ref, perm_ref, permt_ref, lr_ref, li_ref, ldt_ref, bwr_ref, bwi_ref, cwr_ref, cwi_ref,
                   d_ref, gw_ref, gb_ref, y_ref, bre_scr, bim_scr, cre_scr, cim_scr, ar_scr, ai_scr, xr_scr, xi_scr, zr_scr, zi_scr,
                   *, bsz):
    nset, cw, sw = bre_scr.shape

    @pl.when(pl.program_id(0) == 0)
    def _():
        chan_grp = lax.broadcasted_iota(jnp.int32, (cw, sw), 0) >> S5_GROUP_SHIFT
        state_grp = lax.broadcasted_iota(jnp.int32, (cw, sw), 1) >> S5_STATE_SHIFT
        same_b = chan_grp == state_grp
        same_c = ((lax.broadcasted_iota(jnp.int32, (sw, cw), 0) >> S5_STATE_SHIFT)
                  == (lax.broadcasted_iota(jnp.int32, (sw, cw), 1) >> S5_GROUP_SHIFT))
        for s in range(nset):
            lr = lr_ref[s]
            li = li_ref[s]
            dt = jnp.exp(ldt_ref[s])
            mag = jnp.exp(lr * dt)
            ar = mag * jnp.cos(li * dt)
            ai = mag * jnp.sin(li * dt)
            den = lr * lr + li * li
            qr = ((ar - 1.0) * lr + ai * li) / den
            qi = (ai * lr - (ar - 1.0) * li) / den
            ar_scr[s] = jnp.broadcast_to(ar, (bsz, sw))
            ai_scr[s] = jnp.broadcast_to(ai, (bsz, sw))
            bwr = bwr_ref[s]
            bwi = bwi_ref[s]
            bre_scr[s] = jnp.where(same_b, qr * bwr - qi * bwi, 0.0).astype(BF16)
            bim_scr[s] = jnp.where(same_b, qr * bwi + qi * bwr, 0.0).astype(BF16)
            cre_scr[s] = jnp.where(same_c, cwr_ref[s], 0.0).astype(BF16)
            cim_scr[s] = jnp.where(same_c, cwi_ref[s], 0.0).astype(BF16)
        xr_scr[...] = jnp.zeros_like(xr_scr)
        xi_scr[...] = jnp.zeros_like(xi_scr)

    steps = u_ref.shape[1]
    width = u_ref.shape[2]
    u_tb = _dot(perm_ref[...], u_ref[...].reshape(bsz * steps, width))
    u_b = u_tb.astype(BF16)
    half_rows = (steps // 2) * bsz

    def project(half, s):
        rows = slice(half * half_rows, (half + 1) * half_rows)
        us = u_b[rows, s * cw:(s + 1) * cw]
        zr_scr[s, rows, :] = _dot(us, bre_scr[s])
        zi_scr[s, rows, :] = _dot(us, bim_scr[s])

    for s in range(nset):
        project(0, s)

    def step(t, carry):
        r0 = t * bsz
        out = []
        for s in range(nset):
            xr, xi = carry[2 * s], carry[2 * s + 1]
            ar = ar_scr[s]
            ai = ai_scr[s]
            nr = ar * xr - ai * xi + zr_scr[s, r0:r0 + bsz, :]
            ni = ar * xi + ai * xr + zi_scr[s, r0:r0 + bsz, :]
            zr_scr[s, r0:r0 + bsz, :] = nr
            zi_scr[s, r0:r0 + bsz, :] = ni
            out += [nr, ni]
        return out

    def readout(half, s):
        rows = slice(half * half_rows, (half + 1) * half_rows)
        return _dot(zr_scr[s, rows, :].astype(BF16), cre_scr[s]) - _dot(zi_scr[s, rows, :].astype(BF16), cim_scr[s])

    every = steps // 2 // nset
    carry = []
    for s in range(nset):
        carry += [xr_scr[s], xi_scr[s]]
    for t in range(steps // 2):
        carry = step(t, carry)
        if t % every == 0:
            project(1, t // every)
    first_half = []
    for t in range(steps // 2, steps):
        carry = step(t, carry)
        k = t - steps // 2
        if k % every == 0:
            first_half.append(readout(0, k // every))
    for s in range(nset):
        xr_scr[s] = carry[2 * s]
        xi_scr[s] = carry[2 * s + 1]
    second_half = [readout(1, s) for s in range(nset)]
    xc = jnp.concatenate([jnp.concatenate(first_half, axis=1), jnp.concatenate(second_half, axis=1)], axis=0)
    y = (xc + d_ref[...] * u_tb).astype(BF16)
    y = _dot(permt_ref[...], y)
    y = 0.5 * y * (1.0 + jnp.tanh(math.sqrt(2.0 / math.pi) * (y + 0.044715 * (y * y * y))))
    y = y * _sigmoid(_dot(y.astype(BF16), gw_ref[...]) + gb_ref[...])
    z = z_ref[...].reshape(bsz * steps, width).astype(F32)
    y_ref[...] = (y * _silu(z)).astype(BF16).reshape(bsz, steps, width)


def _s5scan(proj3, lam_re, lam_im, log_dt, b_re, b_im, c_re, c_im, d_skip, glu_w, glu_b):
    bsz, seq, _ = proj3.shape
    width = S5_WIDTH
    nset = S5_SETS
    cw = width // nset
    gs = S5_GROUPS // nset
    sw = gs * S5_STATE
    blk = S5_STEPS * bsz
    tb = jnp.arange(blk)
    perm = ((tb % bsz) * S5_STEPS + tb // bsz)[:, None] == jnp.arange(blk)[None, :]
    vec = lambda a: a.reshape(nset, 1, sw)
    bw = lambda b: jnp.tile(jnp.swapaxes(b, 1, 2).reshape(nset, cw, S5_STATE), (1, 1, gs))
    cw_t = lambda c: jnp.tile(
        jnp.swapaxes(c, 1, 2).reshape(nset, gs, S5_STATE, S5_GROUP).transpose(0, 2, 1, 3).reshape(nset, S5_STATE, cw),
        (1, gs, 1))
    full3 = lambda a, b, c: pl.BlockSpec((a, b, c), lambda i: (0, 0, 0))
    return pl.pallas_call(
        functools.partial(_s5scan_kernel, bsz=bsz),
        grid=(seq // S5_STEPS,),
        in_specs=[
            pl.BlockSpec((bsz, S5_STEPS, width), lambda i: (0, i, OD_U // width)),
            pl.BlockSpec((bsz, S5_STEPS, width), lambda i: (0, i, OD_ZD // width)),
            pl.BlockSpec((blk, blk), lambda i: (0, 0)),
            pl.BlockSpec((blk, blk), lambda i: (0, 0)),
            full3(nset, 1, sw), full3(nset, 1, sw), full3(nset, 1, sw),
            full3(nset, cw, sw), full3(nset, cw, sw),
            full3(nset, sw, cw), full3(nset, sw, cw),
            pl.BlockSpec((1, width), lambda i: (0, 0)),
            pl.BlockSpec((width, width), lambda i: (0, 0)),
            pl.BlockSpec((1, width), lambda i: (0, 0)),
        ],
        out_specs=pl.BlockSpec((bsz, S5_STEPS, width), lambda i: (0, i, 0)),
        out_shape=jax.ShapeDtypeStruct((bsz, seq, width), BF16),
        scratch_shapes=[
            pltpu.VMEM((nset, cw, sw), BF16), pltpu.VMEM((nset, cw, sw), BF16),
            pltpu.VMEM((nset, sw, cw), BF16), pltpu.VMEM((nset, sw, cw), BF16),
            pltpu.VMEM((nset, bsz, sw), F32), pltpu.VMEM((nset, bsz, sw), F32),
            pltpu.VMEM((nset, bsz, sw), F32), pltpu.VMEM((nset, bsz, sw), F32),
            pltpu.VMEM((nset, blk, sw), F32), pltpu.VMEM((nset, blk, sw), F32),
        ],
        compiler_params=_params(("arbitrary",)),
        name="s5_scan",
    )(proj3, proj3, perm.astype(BF16), perm.T.astype(BF16), vec(lam_re), vec(lam_im),
      vec(jnp.repeat(log_dt, S5_STATE)), bw(b_re), bw(b_im), cw_t(c_re), cw_t(c_im), d_skip.reshape(1, width),
      glu_w.astype(BF16), glu_b.reshape(1, width))


def _pack_weights(in_w, pieces, small):
    wt = in_w.T
    d = wt.shape[1]
    starts, scales = [], []
    for a, b, scale in pieces:
        starts += list(range(a, b, PACK_ROWS))
        scales += [scale] * ((b - a) // PACK_ROWS)

    def pick(c, table):
        out = table[-1]
        for i in range(len(table) - 2, -1, -1):
            out = jnp.where(c == i, table[i], out)
        return out

    def pack_kernel(w_ref, o_ref):
        scale = pick(pl.program_id(0), [jnp.float32(s) for s in scales])
        o_ref[...] = (w_ref[...] * scale).astype(BF16)

    main = pl.pallas_call(
        pack_kernel,
        grid=(len(starts),),
        in_specs=[pl.BlockSpec((pl.Element(PACK_ROWS), pl.Element(d)),
                               lambda c: (pl.multiple_of(pick(c, [s // 8 for s in starts]) * 8, 8), 0))],
        out_specs=pl.BlockSpec((PACK_ROWS, d), lambda c: (c, 0)),
        out_shape=jax.ShapeDtypeStruct((len(starts) * PACK_ROWS, d), BF16),
        compiler_params=_params(("parallel",)),
        name="pack_w",
    )(wt)
    a, b = small
    rows = jnp.pad(wt[a:b], ((0, LANES - (b - a)), (0, 0)))
    hi = rows.astype(BF16)
    lo = (rows - hi.astype(F32)).astype(BF16)
    return main, jnp.concatenate([hi, lo], axis=0)


def _even_layer(x2, mod, pre_g, post_g, in_w, conv_w, conv_b, dt_bias, a_log, d_skip, norm_g, out_w, bsz, seq):
    d = D_MODEL
    shift, scale, gate = (mod[:, i * d:(i + 1) * d].reshape(bsz, 1, d) for i in range(3))
    o_xbc = 2 * SSD_WIDTH
    o_dt = o_xbc + SSD_WIDTH + 2 * SSD_GROUPS * SSD_STATE
    o_q = o_dt + SSD_HEADS
    w, ws = _pack_weights(in_w, (
        (0, o_xbc + SSD_WIDTH, 1.0),
        (o_q, o_q + MOBA_WIDTH, ATT_SCALE),
        (o_q + MOBA_WIDTH, in_w.shape[1], 1.0),
        (o_xbc + SSD_WIDTH, o_dt, 1.0),
    ), (o_dt, o_q))
    proj, small = _inproj(x2, scale, shift, pre_g.reshape(1, d), w, ws, seq)
    y_a = _ssd(proj, small, conv_w, conv_b, dt_bias, a_log, d_skip, norm_g, bsz, seq)
    y_b = _moba(proj, bsz, seq)
    ow = out_w.astype(BF16)
    return _outproj(y_a, y_b, ow[:SSD_WIDTH], ow[SSD_WIDTH:], x2, gate, post_g.reshape(1, d), seq)


def _odd_layer(x2, mod, pre_g, post_g, in_w, fgate_b, lam_re, lam_im, log_dt, b_re, b_im, c_re, c_im,
               d_skip, glu_w, glu_b, out_w, bsz, seq):
    d = D_MODEL
    shift, scale, gate = (mod[:, i * d:(i + 1) * d].reshape(bsz, 1, d) for i in range(3))
    o_f = D_MIX + 3 * FOX_WIDTH
    o_u = o_f + FOX_HEADS
    w, ws = _pack_weights(in_w, (
        (0, D_MIX, 1.0),
        (D_MIX, D_MIX + FOX_WIDTH, ATT_SCALE),
        (D_MIX + FOX_WIDTH, o_f, 1.0),
        (o_u, in_w.shape[1], 1.0),
    ), (o_f, o_u))
    proj, small = _inproj(x2, scale, shift, pre_g.reshape(1, d), w, ws, seq)

    negf = _fgate(small, fgate_b, bsz, seq)
    y_c = _fox(proj, negf, bsz, seq)

    y_d = _s5scan(proj.reshape(bsz, seq, OD_N), lam_re, lam_im, log_dt, b_re, b_im, c_re, c_im, d_skip,
                  glu_w, glu_b).reshape(bsz * seq, S5_WIDTH)

    ow = out_w.astype(BF16)
    return _outproj(y_c, y_d, ow[:FOX_WIDTH], ow[FOX_WIDTH:], x2, gate, post_g.reshape(1, d), seq)


def kernel(x, c, ada_w, ada_b, pre_g, post_g, even_in_w, even_conv_w, even_conv_b, even_dt_bias, even_a_log,
           even_d_skip, even_norm_g, even_out_w, odd_in_w, odd_fgate_b, odd_lam_re, odd_lam_im, odd_log_dt,
           odd_b_re, odd_b_im, odd_c_re, odd_c_im, odd_d_skip, odd_glu_w, odd_glu_b, odd_out_w):
    bsz, seq, d = x.shape
    depth = ada_w.shape[0]
    mod = _ada_mod(c, ada_w, ada_b)
    x2 = x.reshape(bsz * seq, d)
    for layer in range(depth):
        i = layer // 2
        if layer % 2 == 0:
            x2 = _even_layer(x2, mod[layer], pre_g[layer], post_g[layer], even_in_w[i], even_conv_w[i],
                             even_conv_b[i], even_dt_bias[i], even_a_log[i], even_d_skip[i], even_norm_g[i],
                             even_out_w[i], bsz, seq)
        else:
            x2 = _odd_layer(x2, mod[layer], pre_g[layer], post_g[layer], odd_in_w[i], odd_fgate_b[i],
                            odd_lam_re[i], odd_lam_im[i], odd_log_dt[i], odd_b_re[i], odd_b_im[i], odd_c_re[i],
                            odd_c_im[i], odd_d_skip[i], odd_glu_w[i], odd_glu_b[i], odd_out_w[i], bsz, seq)
    return x2.reshape(bsz, seq, d)
```
